```python
import numpy as np
import jax, jax.numpy as jnp
from jax import lax

D_MODEL = 2048
BATCH = 2
SEQ = 4096
DEPTH = 4

N_MIXERS = 4
HEAD_DIM = 128
N_HEADS = D_MODEL // HEAD_DIM
ROT_DIM = HEAD_DIM // 4
ROPE_THETA = 500000.0
NORM_EPS = 1e-6
D_FF = 4 * D_MODEL
PLE_DIM = 256

MOBA_BLOCK = 256
MOBA_TOPK = 3
MOBA_QCHUNK = 32

POOL_WINDOWS = (2, 4, 8, 16)
POOL_GROUP = D_MODEL // len(POOL_WINDOWS)

NSA_KV_GROUPS = 4
NSA_Q_PER_KV = N_HEADS // NSA_KV_GROUPS
NSA_CMP_LEN = 32
NSA_CMP_STRIDE = 16
NSA_SLC_LEN = 64
NSA_SLC_TOPK = 16
NSA_WINDOW = 512
NSA_QCHUNK = 64
NSA_KV_WIDTH = NSA_KV_GROUPS * HEAD_DIM

CONV_WIDTH = 3

N_LAYERS_MOBA = (DEPTH + 3) // 4
N_LAYERS_POOL = (DEPTH + 2) // 4
N_LAYERS_NSA = (DEPTH + 1) // 4
N_LAYERS_CONV = DEPTH // 4

kernel_name = "hybrid_moba_pool_nsa_conv_trunk"


def rmsnorm(x, gain):
    xf = x.astype(jnp.float32)
    y = xf * lax.rsqrt(jnp.mean(xf * xf, axis=-1, keepdims=True) + NORM_EPS)
    return (y * gain.astype(jnp.float32)).astype(x.dtype)


def partial_rope(x, positions):
    half = ROT_DIM // 2
    freqs = jnp.float32(ROPE_THETA) ** (-jnp.arange(half, dtype=jnp.float32) * 2.0 / ROT_DIM)
    ang = positions.astype(jnp.float32)[..., None] * freqs
    cos = jnp.cos(ang)[:, :, None, :]
    sin = jnp.sin(ang)[:, :, None, :]
    xr = x[..., :ROT_DIM].astype(jnp.float32)
    x1, x2 = xr[..., :half], xr[..., half:]
    rot = jnp.concatenate([x1 * cos - x2 * sin, x2 * cos + x1 * sin], axis=-1).astype(x.dtype)
    return jnp.concatenate([rot, x[..., ROT_DIM:]], axis=-1)


def masked_softmax(logits, mask):
    s = jnp.where(mask, logits.astype(jnp.float32), -jnp.inf)
    m = jnp.max(s, axis=-1, keepdims=True)
    m = jnp.where(jnp.isfinite(m), m, 0.0)
    e = jnp.where(mask, jnp.exp(s - m), 0.0)
    return e / jnp.maximum(jnp.sum(e, axis=-1, keepdims=True), jnp.finfo(jnp.float32).tiny)


def moba_mixer(xn, positions, w_qkv, q_gain, k_gain, w_o):
    B, S, _ = xn.shape
    H, hd, L = N_HEADS, HEAD_DIM, MOBA_BLOCK
    qkv = (xn @ w_qkv).reshape(B, S, 3, H, hd)
    q = partial_rope(rmsnorm(qkv[:, :, 0], q_gain), positions)
    k = partial_rope(rmsnorm(qkv[:, :, 1], k_gain), positions)
    v = qkv[:, :, 2]
    q, k, v = (t.transpose(0, 2, 1, 3) for t in (q, k, v))
    nb = -(-S // L)
    pad = nb * L - S
    kp = jnp.pad(k, ((0, 0), (0, 0), (0, pad), (0, 0)))
    vp = jnp.pad(v, ((0, 0), (0, 0), (0, pad), (0, 0)))
    kb = kp.reshape(B, H, nb, L, hd)
    vb = vp.reshape(B, H, nb, L, hd)
    k_mean = jnp.mean(kb.astype(jnp.float32), axis=3).astype(k.dtype)
    n_sel = min(MOBA_TOPK, nb)
    scale = HEAD_DIM ** -0.5
    bi = jnp.arange(B)[:, None, None, None]
    hi = jnp.arange(H)[None, :, None, None]
    blk_ids = jnp.arange(nb)
    in_blk = jnp.arange(L)
    Qc = MOBA_QCHUNK

    def chunk(c):
        q0 = c * Qc
        t = q0 + jnp.arange(Qc)
        own = q0 // L
        qc = lax.dynamic_slice_in_dim(q, q0, Qc, axis=2)
        gate = jnp.einsum('bhqd,bhnd->bhqn', qc, k_mean).astype(jnp.float32)
        gate = jnp.where(blk_ids < own, gate, -jnp.inf)
        _, sel = lax.top_k(gate, n_sel)
        sel_ok = jnp.arange(n_sel) < own
        k_sel = kb[bi, hi, sel]
        v_sel = vb[bi, hi, sel]
        s_sel = jnp.einsum('bhqd,bhqnld->bhqnl', qc, k_sel).astype(jnp.float32) * scale
        k_own = lax.dynamic_slice_in_dim(kp, own * L, L, axis=2)
        v_own = lax.dynamic_slice_in_dim(vp, own * L, L, axis=2)
        s_own = jnp.einsum('bhqd,bhld->bhql', qc, k_own).astype(jnp.float32) * scale
        own_ok = (own * L + in_blk)[None, :] <= t[:, None]
        logits = jnp.concatenate([s_sel.reshape(B, H, Qc, n_sel * L), s_own], axis=-1)
        sel_mask = jnp.broadcast_to(jnp.repeat(sel_ok, L)[None, :], (Qc, n_sel * L))
        mask = jnp.concatenate([sel_mask, own_ok], axis=-1)
        probs = masked_softmax(logits, mask).astype(v.dtype)
        p_sel = probs[..., :n_sel * L].reshape(B, H, Qc, n_sel, L)
        p_own = probs[..., n_sel * L:]
        return (jnp.einsum('bhqnl,bhqnld->bhqd', p_sel, v_sel)
                + jnp.einsum('bhql,bhld->bhqd', p_own, v_own))

    o = lax.map(chunk, jnp.arange(S // Qc))
    o = o.transpose(1, 0, 3, 2, 4).reshape(B, S, H * hd)
    return o @ w_o


def pool_mixer(xn, w_groups, scale):
    B, S, D = xn.shape
    xf = xn.astype(jnp.float32)
    cs = jnp.concatenate([jnp.zeros((B, 1, D), jnp.float32), lax.cumsum(xf, axis=1)], axis=1)
    hi = jnp.arange(1, S + 1)
    outs = []
    for g, w in enumerate(POOL_WINDOWS):
        sl = slice(g * POOL_GROUP, (g + 1) * POOL_GROUP)
        lo = jnp.maximum(hi - w, 0)
        cnt = (hi - lo).astype(jnp.float32)[None, :, None]
        mean = (cs[:, hi, sl] - cs[:, lo, sl]) / cnt
        outs.append((mean - xf[:, :, sl]).astype(xn.dtype) @ w_groups[g])
    return jnp.concatenate(outs, axis=-1) * scale


def nsa_mixer(xn, positions, w_q, w_kv, q_gain, k_gain, cmp_pos, cmp_w1, cmp_w2, w_gate, w_o):
    B, S, _ = xn.shape
    G, R, hd, H = NSA_KV_GROUPS, NSA_Q_PER_KV, HEAD_DIM, N_HEADS
    scale = HEAD_DIM ** -0.5
    q = rmsnorm((xn @ w_q).reshape(B, S, H, hd), q_gain)
    q_cmp = q.reshape(B, S, G, R, hd)
    q_rot = partial_rope(q, positions).reshape(B, S, G, R, hd)
    kv = (xn @ w_kv).reshape(B, S, 6, G, hd)
    k_slc = partial_rope(rmsnorm(kv[:, :, 2], k_gain[1]), positions)
    v_slc = kv[:, :, 3]
    k_win = partial_rope(rmsnorm(kv[:, :, 4], k_gain[2]), positions)
    v_win = kv[:, :, 5]

    n_cmp = (S - NSA_CMP_LEN) // NSA_CMP_STRIDE + 1
    starts = np.arange(n_cmp) * NSA_CMP_STRIDE
    idx = starts[:, None] + np.arange(NSA_CMP_LEN)[None, :]

    def compress(t, pos, w1, w2):
        blk = t[:, idx] + pos[None, None, :, None, :]
        blk = blk.transpose(0, 1, 3, 2, 4).reshape(B, n_cmp, G, NSA_CMP_LEN * hd)
        return jax.nn.gelu(blk @ w1) @ w2

    k_cmp = rmsnorm(compress(kv[:, :, 0], cmp_pos[0], cmp_w1[0], cmp_w2[0]), k_gain[0])
    v_cmp = compress(kv[:, :, 1], cmp_pos[1], cmp_w1[1], cmp_w2[1])
    t_all = jnp.arange(S)
    cmp_ok = jnp.asarray(starts + NSA_CMP_LEN - 1)[None, :] <= t_all[:, None]
    s_cmp = jnp.einsum('bsgrd,bngd->bgrsn', q_cmp, k_cmp).astype(jnp.float32) * scale
    p_cmp = masked_softmax(s_cmp, cmp_ok)
    o_cmp = jnp.einsum('bgrsn,bngd->bsgrd', p_cmp.astype(v_cmp.dtype), v_cmp)

    n_slc = S // NSA_SLC_LEN
    slc_lo = np.arange(n_slc) * NSA_SLC_LEN
    overlap = ((starts[:, None] < (slc_lo + NSA_SLC_LEN)[None, :])
               & (slc_lo[None, :] < (starts + NSA_CMP_LEN)[:, None])).astype(np.float32)
    imp = jnp.einsum('bgrsn,nj->bgsj', p_cmp, jnp.asarray(overlap))
    cur = t_all // NSA_SLC_LEN
    jb = jnp.arange(n_slc)
    forced = (jb[None, :] == cur[:, None]) | (jb[None, :] == 0)
    imp = jnp.where(forced, jnp.inf, jnp.where(jb[None, :] <= cur[:, None], imp, -jnp.inf))
    n_top = min(NSA_SLC_TOPK, n_slc)
    top_val, top_idx = lax.top_k(imp, n_top)
    top_ok = top_val > -jnp.inf

    k_slc_b = k_slc.reshape(B, n_slc, NSA_SLC_LEN, G, hd).transpose(0, 3, 1, 2, 4)
    v_slc_b = v_slc.reshape(B, n_slc, NSA_SLC_LEN, G, hd).transpose(0, 3, 1, 2, 4)
    W = NSA_WINDOW
    k_win_p = jnp.pad(k_win, ((0, 0), (W, 0), (0, 0), (0, 0)))
    v_win_p = jnp.pad(v_win, ((0, 0), (W, 0), (0, 0), (0, 0)))
    bi = jnp.arange(B)[:, None, None, None]
    gi = jnp.arange(G)[None, :, None, None]
    in_slc = jnp.arange(NSA_SLC_LEN)
    Qc = NSA_QCHUNK
    win_off = jnp.arange(Qc + W)

    def chunk(c):
        q0 = c * Qc
        t = q0 + jnp.arange(Qc)
        qc = lax.dynamic_slice_in_dim(q_rot, q0, Qc, axis=1)
        idx_c = lax.dynamic_slice_in_dim(top_idx, q0, Qc, axis=2)
        ok_c = lax.dynamic_slice_in_dim(top_ok, q0, Qc, axis=2)
        k_sel = k_slc_b[bi, gi, idx_c]
        v_sel = v_slc_b[bi, gi, idx_c]
        s = jnp.einsum('bqgrd,bgqnld->bgrqnl', qc, k_sel).astype(jnp.float32) * scale
        kpos = idx_c[..., None] * NSA_SLC_LEN + in_slc
        m = ok_c[..., None] & (kpos <= t[None, None, :, None, None])
        p = masked_softmax(s.reshape(B, G, R, Qc, n_top * NSA_SLC_LEN),
                           m[:, :, None].reshape(B, G, 1, Qc, n_top * NSA_SLC_LEN))
        o_s = jnp.einsum('bgrqnl,bgqnld->bqgrd',
                         p.reshape(B, G, R, Qc, n_top, NSA_SLC_LEN).astype(v_sel.dtype), v_sel)
        kw = lax.dynamic_slice_in_dim(k_win_p, q0, Qc + W, axis=1)
        vw = lax.dynamic_slice_in_dim(v_win_p, q0, Qc + W, axis=1)
        kpos_w = q0 - W + win_off
        dist = t[:, None] - kpos_w[None, :]
        wm = (kpos_w[None, :] >= 0) & (dist >= 0) & (dist < W)
        sw = jnp.einsum('bqgrd,bkgd->bgrqk', qc, kw).astype(jnp.float32) * scale
        pw = masked_softmax(sw, wm)
        o_w = jnp.einsum('bgrqk,bkgd->bqgrd', pw.astype(vw.dtype), vw)
        return o_s, o_w

    o_slc, o_win = lax.map(chunk, jnp.arange(S // Qc))
    o_slc = o_slc.transpose(1, 0, 2, 3, 4, 5).reshape(B, S, H, hd)
    o_win = o_win.transpose(1, 0, 2, 3, 4, 5).reshape(B, S, H, hd)
    gates = jax.nn.sigmoid(xn @ w_gate).reshape(B, S, H, 3)
    o = (gates[..., 0:1] * o_cmp.reshape(B, S, H, hd)
         + gates[..., 1:2] * o_slc + gates[..., 2:3] * o_win)
    return o.reshape(B, S, H * hd) @ w_o


def conv_mixer(xn, w_in, conv_w, conv_b, w_o):
    D = xn.shape[-1]
    bch = xn @ w_in
    b_gate, c_gate, h = bch[..., :D], bch[..., D:2 * D], bch[..., 2 * D:]
    u = c_gate * h
    conv = lax.conv_general_dilated(u, conv_w[:, None, :], window_strides=(1,),
                                    padding=[(CONV_WIDTH - 1, 0)],
                                    dimension_numbers=('NWC', 'WIO', 'NWC'),
                                    feature_group_count=D) + conv_b
    return (b_gate * conv) @ w_o


def squared_relu_mlp(xn, w1, w2):
    return jnp.square(jax.nn.relu(xn @ w1)) @ w2


def setup_inputs(seed: int = 0) -> dict:
    key = jax.random.key(seed)
    ks = iter(jax.random.split(key, 48))

    def nrm(shape, scale):
        return jax.random.normal(next(ks), shape, jnp.float32) * scale

    def gain(shape):
        return 1.0 + 0.02 * jax.random.normal(next(ks), shape, jnp.float32)

    D, hd = D_MODEL, HEAD_DIM
    NA, NP, NN, NC = N_LAYERS_MOBA, N_LAYERS_POOL, N_LAYERS_NSA, N_LAYERS_CONV
    return {
        "x": nrm((BATCH, SEQ, D), 1.0),
        "p": nrm((DEPTH, BATCH, SEQ, PLE_DIM), 1.0),
        "positions": jnp.broadcast_to(jnp.arange(SEQ, dtype=jnp.int32), (BATCH, SEQ)),
        "mixer_norm": gain((DEPTH, D)),
        "mlp_norm": gain((DEPTH, D)),
        "mlp_w1": nrm((DEPTH, D, D_FF), D ** -0.5),
        "mlp_w2": nrm((DEPTH, D_FF, D), 0.5 * D_FF ** -0.5),
        "ple_norm": gain((DEPTH, D)),
        "ple_gate": nrm((DEPTH, D, D), D ** -0.5),
        "ple_proj": nrm((DEPTH, PLE_DIM, D), 0.5 * PLE_DIM ** -0.5),
        "moba_w_qkv": nrm((NA, D, 3 * D), D ** -0.5),
        "moba_q_gain": gain((NA, hd)),
        "moba_k_gain": gain((NA, hd)),
        "moba_w_o": nrm((NA, D, D), D ** -0.5),
        "pool_w": nrm((NP, len(POOL_WINDOWS), POOL_GROUP, POOL_GROUP), POOL_GROUP ** -0.5),
        "pool_scale": gain((NP, D)),
        "nsa_w_q": nrm((NN, D, D), D ** -0.5),
        "nsa_w_kv": nrm((NN, D, 6 * NSA_KV_WIDTH), D ** -0.5),
        "nsa_q_gain": gain((NN, hd)),
        "nsa_k_gain": gain((NN, 3, hd)),
        "nsa_cmp_pos": nrm((NN, 2, NSA_CMP_LEN, hd), 0.1),
        "nsa_cmp_w1": nrm((NN, 2, NSA_CMP_LEN * hd, hd), (NSA_CMP_LEN * hd) ** -0.5),
        "nsa_cmp_w2": nrm((NN, 2, hd, hd), hd ** -0.5),
        "nsa_w_gate": nrm((NN, D, 3 * N_HEADS), D ** -0.5),
        "nsa_w_o": nrm((NN, D, D), D ** -0.5),
        "conv_w_in": nrm((NC, D, 3 * D), D ** -0.5),
        "conv_w": nrm((NC, CONV_WIDTH, D), CONV_WIDTH ** -0.5),
        "conv_b": nrm((NC, D), 0.01),
        "conv_w_o": nrm((NC, D, D), D ** -0.5),
    }


def reference(x, p, positions, mixer_norm, mlp_norm, mlp_w1, mlp_w2, ple_norm, ple_gate, ple_proj,
              moba_w_qkv, moba_q_gain, moba_k_gain, moba_w_o, pool_w, pool_scale,
              nsa_w_q, nsa_w_kv, nsa_q_gain, nsa_k_gain, nsa_cmp_pos, nsa_cmp_w1, nsa_cmp_w2,
              nsa_w_gate, nsa_w_o, conv_w_in, conv_w, conv_b, conv_w_o):
    h = x
    for i in range(DEPTH):
        kind, j = i % N_MIXERS, i // N_MIXERS
        xn = rmsnorm(h, mixer_norm[i])
        if kind == 0:
            mix = moba_mixer(xn, positions, moba_w_qkv[j], moba_q_gain[j], moba_k_gain[j], moba_w_o[j])
        elif kind == 1:
            mix = pool_mixer(xn, pool_w[j], pool_scale[j])
        elif kind == 2:
            mix = nsa_mixer(xn, positions, nsa_w_q[j], nsa_w_kv[j], nsa_q_gain[j], nsa_k_gain[j],
                            nsa_cmp_pos[j], nsa_cmp_w1[j], nsa_cmp_w2[j], nsa_w_gate[j], nsa_w_o[j])
        else:
            mix = conv_mixer(xn, conv_w_in[j], conv_w[j], conv_b[j], conv_w_o[j])
        h = h + mix
        h = h + squared_relu_mlp(rmsnorm(h, mlp_norm[i]), mlp_w1[i], mlp_w2[i])
        gate = jax.nn.sigmoid(rmsnorm(h, ple_norm[i]) @ ple_gate[i])
        h = h + gate * (p[i] @ ple_proj[i])
    return h
```

```python
import functools

import jax
import jax.numpy as jnp
from jax import lax
from jax.experimental import pallas as pl
from jax.experimental.pallas import tpu as pltpu

F32 = jnp.float32
BF16 = jnp.bfloat16

HEAD_DIM = 128
ROT_DIM = HEAD_DIM // 4
ROPE_THETA = 500000.0
NORM_EPS = 1e-6
MOBA_BLOCK = 256
MOBA_TOPK = 3
POOL_WINDOWS = (2, 4, 8, 16)
NSA_KV_GROUPS = 4
NSA_CMP_LEN = 32
NSA_CMP_STRIDE = 16
NSA_SLC_LEN = 64
NSA_SLC_TOPK = 16
NSA_WINDOW = 512
CONV_WIDTH = 3

MASKED_LOGIT = -1e30
KV_CHUNK = 256
MIB = 1024 * 1024


def _params(semantics, vmem_mib):
    return pltpu.CompilerParams(dimension_semantics=semantics,
                                vmem_limit_bytes=vmem_mib * MIB)


def _resident(shape, index_map):
    return pl.BlockSpec(shape, index_map, pipeline_mode=pl.Buffered(1))


def _rms(x, gain):
    ms = jnp.mean(x * x, axis=-1, keepdims=True)
    return x * lax.rsqrt(ms + NORM_EPS) * gain


def _rope(x, cos, sin_lo, sin_hi):
    half = ROT_DIM // 2
    return (x * cos + pltpu.roll(x, HEAD_DIM - half, 1) * sin_lo
            + pltpu.roll(x, half, 1) * sin_hi)


def _nt_dot(a, b):
    return lax.dot_general(a, b, (((1,), (1,)), ((), ())), preferred_element_type=F32)


def _rope_table_body(pos_ref, freq_ref, lo_ref, hi_ref, cos_ref, slo_ref, shi_ref):
    ang = pos_ref[...].astype(F32) * freq_ref[...]
    s = jnp.sin(ang)
    cos_ref[...] = jnp.cos(ang)
    slo_ref[...] = s * lo_ref[...]
    shi_ref[...] = s * hi_ref[...]


def rope_tables(positions):
    m = positions.size
    half = ROT_DIM // 2
    freqs = jnp.float32(ROPE_THETA) ** (-jnp.arange(half, dtype=F32) * 2.0 / ROT_DIM)
    zeros = jnp.zeros((HEAD_DIM - ROT_DIM,), F32)
    freq_row = jnp.concatenate([freqs, freqs, zeros])[None]
    lo_row = jnp.concatenate([-jnp.ones((half,), F32), jnp.zeros((half,), F32), zeros])[None]
    hi_row = jnp.concatenate([jnp.zeros((half,), F32), jnp.ones((half,), F32), zeros])[None]
    tm = min(m, 1024)
    row = pl.BlockSpec((1, HEAD_DIM), lambda i: (0, 0))
    tab = pl.BlockSpec((tm, HEAD_DIM), lambda i: (i, 0))
    return pl.pallas_call(
        _rope_table_body,
        grid=(m // tm,),
        in_specs=[pl.BlockSpec((tm, 1), lambda i: (i, 0)), row, row, row],
        out_specs=[tab, tab, tab],
        out_shape=[jax.ShapeDtypeStruct((m, HEAD_DIM), F32)] * 3,
        compiler_params=_params(("parallel",), 32),
        name="rope_tables",
    )(positions.reshape(m, 1), freq_row, lo_row, hi_row)


def _nmm_body(*refs, n_w, n_extra, epilogue):
    x_ref, g_ref = refs[0], refs[1]
    w_refs = refs[2:2 + n_w]
    extra = refs[2 + n_w:2 + n_w + n_extra]
    outs = refs[2 + n_w + n_extra:-1]
    xn_ref = refs[-1]
    j = pl.program_id(1)

    @pl.when(j == 0)
    def _():
        xn_ref[...] = _rms(x_ref[...], g_ref[...]).astype(BF16)

    xn = xn_ref[...]
    accs = [jnp.dot(xn, w[...], preferred_element_type=F32) for w in w_refs]
    epilogue(accs, j, extra, outs)


def norm_matmul(h, gain, w, w_maps, n_col_tiles, tn, extra, outs, epilogue, tm=512, vmem_mib=48):
    m, d = h.shape
    tm = min(tm, m)
    in_specs = [pl.BlockSpec((tm, d), lambda i, j: (i, 0)),
                pl.BlockSpec((1, d), lambda i, j: (0, 0))]
    in_specs += [pl.BlockSpec((d, tn), wm) for wm in w_maps]
    in_specs += [spec for _, spec in extra]
    return pl.pallas_call(
        functools.partial(_nmm_body, n_w=len(w_maps), n_extra=len(extra), epilogue=epilogue),
        grid=(m // tm, n_col_tiles),
        in_specs=in_specs,
        out_specs=[spec for _, spec in outs],
        out_shape=[shape for shape, _ in outs],
        scratch_shapes=[pltpu.VMEM((tm, d), BF16)],
        compiler_params=_params(("parallel", "arbitrary"), vmem_mib),
        name="norm_matmul",
    )(h, gain.reshape(1, d), *([w] * len(w_maps)), *[a for a, _ in extra])


def _head_norm_rope_store(acc, gain_row, tables, out_plain, out_rot):
    cos, slo, shi = tables
    for hh in range(acc.shape[1] // HEAD_DIM):
        sl = slice(hh * HEAD_DIM, (hh + 1) * HEAD_DIM)
        xh = _rms(acc[:, sl], gain_row[:, sl])
        if out_plain is not None:
            out_plain[:, sl] = xh.astype(BF16)
        if out_rot is not None:
            out_rot[:, sl] = _rope(xh, cos, slo, shi).astype(BF16)


def _omm_body(*refs, n_in, prologue):
    ins = refs[:n_in]
    w_ref, res_ref, o_ref = refs[n_in:n_in + 3]
    a = prologue(ins)
    o_ref[...] = res_ref[...] + jnp.dot(a, w_ref[...], preferred_element_type=F32)


def out_matmul(ins, prologue, w, res, tm=512, vmem_mib=48):
    m, d = res.shape
    tm = min(tm, m)
    k = w.shape[0]
    return pl.pallas_call(
        functools.partial(_omm_body, n_in=len(ins), prologue=prologue),
        grid=(m // tm,),
        in_specs=[spec for _, spec in ins] + [
            _resident((k, d), lambda i: (0, 0)),
            pl.BlockSpec((tm, d), lambda i: (i, 0))],
        out_specs=pl.BlockSpec((tm, d), lambda i: (i, 0)),
        out_shape=jax.ShapeDtypeStruct((m, d), F32),
        compiler_params=_params(("parallel",), vmem_mib),
        name="out_matmul",
    )(*[a for a, _ in ins], w, res)


def _mlp_body(x_ref, g_ref, w1_ref, w2_ref, o_ref, xn_ref, acc_ref):
    f = pl.program_id(1)

    @pl.when(f == 0)
    def _():
        xn_ref[...] = _rms(x_ref[...], g_ref[...]).astype(BF16)
        acc_ref[...] = jnp.zeros_like(acc_ref)

    a = jnp.dot(xn_ref[...], w1_ref[...], preferred_element_type=F32)
    a = jnp.square(jnp.maximum(a, 0.0)).astype(BF16)
    acc_ref[...] += jnp.dot(a, w2_ref[...], preferred_element_type=F32)

    @pl.when(f == pl.num_programs(1) - 1)
    def _():
        o_ref[...] = x_ref[...] + acc_ref[...]


def mlp_layer(h, gain, w1, w2, tm=512, tf=512):
    m, d = h.shape
    dff = w1.shape[1]
    tm = min(tm, m)
    return pl.pallas_call(
        _mlp_body,
        grid=(m // tm, dff // tf),
        in_specs=[pl.BlockSpec((tm, d), lambda i, f: (i, 0)),
                  pl.BlockSpec((1, d), lambda i, f: (0, 0)),
                  pl.BlockSpec((d, tf), lambda i, f: (0, f)),
                  pl.BlockSpec((tf, d), lambda i, f: (f, 0))],
        out_specs=pl.BlockSpec((tm, d), lambda i, f: (i, 0)),
        out_shape=jax.ShapeDtypeStruct((m, d), F32),
        scratch_shapes=[pltpu.VMEM((tm, d), BF16), pltpu.VMEM((tm, d), F32)],
        compiler_params=_params(("parallel", "arbitrary"), 48),
        name="mlp",
    )(h, gain.reshape(1, d), w1, w2)


def _ple_body(x_ref, g_ref, wg_ref, p_ref, wp_ref, o_ref):
    x = x_ref[...]
    xn = _rms(x, g_ref[...]).astype(BF16)
    gate = jax.nn.sigmoid(jnp.dot(xn, wg_ref[...], preferred_element_type=F32))
    emb = jnp.dot(p_ref[...].astype(BF16), wp_ref[...], preferred_element_type=F32)
    o_ref[...] = x + gate * emb


def ple_layer(h, gain, wg, p, wp, tm=512):
    m, d = h.shape
    pd = p.shape[1]
    tm = min(tm, m)
    return pl.pallas_call(
        _ple_body,
        grid=(m // tm,),
        in_specs=[pl.BlockSpec((tm, d), lambda i: (i, 0)),
                  pl.BlockSpec((1, d), lambda i: (0, 0)),
                  _resident((d, d), lambda i: (0, 0)),
                  pl.BlockSpec((tm, pd), lambda i: (i, 0)),
                  _resident((pd, d), lambda i: (0, 0))],
        out_specs=pl.BlockSpec((tm, d), lambda i: (i, 0)),
        out_shape=jax.ShapeDtypeStruct((m, d), F32),
        compiler_params=_params(("parallel",), 48),
        name="ple",
    )(h, gain.reshape(1, d), wg, p, wp)


def _flash_step(q, k, v, mask, scale, carry):
    m, l, acc = carry
    s = _nt_dot(q, k) * scale
    m_new = jnp.maximum(m, jnp.max(jnp.where(mask, s, MASKED_LOGIT), axis=-1, keepdims=True))
    p = jnp.where(mask, jnp.exp(s - m_new), 0.0)
    alpha = jnp.exp(m - m_new)
    l = alpha * l + jnp.sum(p, axis=-1, keepdims=True)
    acc = alpha * acc + jnp.dot(p.astype(BF16), v, preferred_element_type=F32)
    return m_new, l, acc


def _flash_init(rows):
    return (jnp.full((rows, 1), MASKED_LOGIT, F32), jnp.zeros((rows, 1), F32),
            jnp.zeros((rows, HEAD_DIM), F32))


def _top_rank(v, lane):
    rank = jnp.zeros(v.shape, jnp.int32)
    for j in range(v.shape[1]):
        col = v[:, j:j + 1]
        before = (col > v) | ((col == v) & (lane > j))
        rank = rank + before.astype(jnp.int32)
    return rank


def _moba_body(q_ref, k_ref, v_ref, o_ref, kmean_ref, *, n_blocks):
    blk = MOBA_BLOCK
    own = pl.program_id(2)
    scale = HEAD_DIM ** -0.5

    @pl.when(own == 0)
    def _():
        for n in range(n_blocks):
            kb = k_ref[n * blk:(n + 1) * blk, :].astype(F32)
            kmean_ref[n:n + 1, :] = jnp.mean(kb, axis=0, keepdims=True)

    q = q_ref[...]
    gate = _nt_dot(q, kmean_ref[...].astype(BF16))
    lane = lax.broadcasted_iota(jnp.int32, gate.shape, 1)
    past = lane < own
    gate = jnp.where(past, gate, -jnp.inf)
    chosen = (past & (_top_rank(gate, lane) < MOBA_TOPK)).astype(F32)

    def past_block(j, carry):
        start = pl.multiple_of(j * blk, blk)
        picked = jnp.sum(jnp.where(lane == j, chosen, 0.0), axis=-1, keepdims=True) > 0.5
        mask = jnp.broadcast_to(picked, (blk, blk))
        return _flash_step(q, k_ref[pl.ds(start, blk), :], v_ref[pl.ds(start, blk), :],
                           mask, scale, carry)

    carry = lax.fori_loop(0, own, past_block, _flash_init(blk))
    start = pl.multiple_of(own * blk, blk)
    causal = (lax.broadcasted_iota(jnp.int32, (blk, blk), 1)
              <= lax.broadcasted_iota(jnp.int32, (blk, blk), 0))
    _, l, acc = _flash_step(q, k_ref[pl.ds(start, blk), :], v_ref[pl.ds(start, blk), :],
                            causal, scale, carry)
    o_ref[...] = (acc / l).astype(BF16)


def moba_attention(qkv, batch, seq, n_heads):
    blk = MOBA_BLOCK
    nq = seq // blk
    return pl.pallas_call(
        functools.partial(_moba_body, n_blocks=nq),
        grid=(batch, n_heads, nq),
        in_specs=[pl.BlockSpec((blk, HEAD_DIM), lambda b, h, i: (b * nq + i, h)),
                  pl.BlockSpec((seq, HEAD_DIM), lambda b, h, i: (b, n_heads + h)),
                  pl.BlockSpec((seq, HEAD_DIM), lambda b, h, i: (b, 2 * n_heads + h))],
        out_specs=pl.BlockSpec((blk, HEAD_DIM), lambda b, h, i: (b * nq + i, h)),
        out_shape=jax.ShapeDtypeStruct((batch * seq, n_heads * HEAD_DIM), BF16),
        scratch_shapes=[pltpu.VMEM((nq, HEAD_DIM), F32)],
        compiler_params=_params(("parallel", "parallel", "arbitrary"), 32),
        name="moba_attention",
    )(qkv, qkv, qkv)


def moba_layer(h, gain, w_qkv, q_gain, k_gain, w_o, tables, batch, seq):
    m, d = h.shape
    n_heads = d // HEAD_DIM
    tn = 512
    n_qk_tiles = 2 * d // tn
    gain_row = jnp.concatenate([jnp.tile(q_gain, n_heads), jnp.tile(k_gain, n_heads),
                                jnp.ones((d,), F32)])[None]

    def epilogue(accs, j, extra, outs):
        g_ref, cos_ref, slo_ref, shi_ref = extra
        (o_ref,) = outs

        @pl.when(j < n_qk_tiles)
        def _():
            _head_norm_rope_store(accs[0], g_ref[...], (cos_ref[...], slo_ref[...], shi_ref[...]),
                                  None, o_ref)

        @pl.when(j >= n_qk_tiles)
        def _():
            o_ref[...] = accs[0].astype(BF16)

    tm = min(512, m)
    tab = pl.BlockSpec((tm, HEAD_DIM), lambda i, j: (i, 0))
    (qkv,) = norm_matmul(
        h, gain, w_qkv, [lambda i, j: (0, j)], 3 * d // tn, tn,
        extra=[(gain_row, pl.BlockSpec((1, tn), lambda i, j: (0, j)))] + [(t, tab) for t in tables],
        outs=[(jax.ShapeDtypeStruct((m, 3 * d), BF16), pl.BlockSpec((tm, tn), lambda i, j: (i, j)))],
        epilogue=epilogue, tm=tm)
    o = moba_attention(qkv, batch, seq, n_heads)
    tmo = min(512, m)
    return out_matmul([(o, pl.BlockSpec((tmo, d), lambda i: (i, 0)))],
                      lambda ins: ins[0][...], w_o, h, tm=tmo)


def _pool_body(x_ref, halo_ref, g_ref, w_ref, s_ref, o_ref, *, tiles_per_seq, halo):
    i = pl.program_id(0)
    tm, d = x_ref.shape
    group = d // len(POOL_WINDOWS)
    x = x_ref[...]
    gain = g_ref[...]
    xn = _rms(x, gain)
    prev = jnp.where(i % tiles_per_seq == 0, 0.0, _rms(halo_ref[...], gain))
    pos = (i % tiles_per_seq) * tm + lax.broadcasted_iota(jnp.int32, (tm, 1), 0)
    for g, win in enumerate(POOL_WINDOWS):
        sl = slice(g * group, (g + 1) * group)
        run = jnp.concatenate([prev[:, sl], xn[:, sl]], axis=0)
        span = 1
        while span < win:
            run = run + pltpu.roll(run, span, 0)
            span *= 2
        cnt = jnp.minimum(pos + 1, win).astype(F32)
        mean = run[halo:, :] / cnt
        mix = jnp.dot((mean - xn[:, sl]).astype(BF16), w_ref[g], preferred_element_type=F32)
        o_ref[:, sl] = x[:, sl] + mix * s_ref[:, sl]


def pool_layer(h, gain, w_groups, scale, seq, tm=512):
    m, d = h.shape
    halo = 16
    assert max(POOL_WINDOWS) <= halo
    tm = min(tm, seq)
    group = d // len(POOL_WINDOWS)
    return pl.pallas_call(
        functools.partial(_pool_body, tiles_per_seq=seq // tm, halo=halo),
        grid=(m // tm,),
        in_specs=[pl.BlockSpec((tm, d), lambda i: (i, 0)),
                  pl.BlockSpec((halo, d), lambda i: (jnp.maximum(i * (tm // halo) - 1, 0), 0)),
                  pl.BlockSpec((1, d), lambda i: (0, 0)),
                  _resident((len(POOL_WINDOWS), group, group), lambda i: (0, 0, 0)),
                  pl.BlockSpec((1, d), lambda i: (0, 0))],
        out_specs=pl.BlockSpec((tm, d), lambda i: (i, 0)),
        out_shape=jax.ShapeDtypeStruct((m, d), F32),
        compiler_params=_params(("parallel",), 48),
        name="pool_mixer",
    )(h, h, gain.reshape(1, d), w_groups, scale.reshape(1, d))


def conv_layer(h, gain, w_in, conv_w, conv_b, w_o, seq):
    m, d = h.shape
    tn = 512
    nj = d // tn

    def epilogue(accs, j, extra, outs):
        b_ref, u_ref = outs
        b_ref[...] = accs[0]
        u_ref[...] = accs[1] * accs[2]

    tm = min(512, m)
    col = pl.BlockSpec((tm, tn), lambda i, j: (i, j))
    act = jax.ShapeDtypeStruct((m, d), F32)
    b_gate, u = norm_matmul(
        h, gain, w_in, [lambda i, j: (0, j), lambda i, j: (0, j + nj), lambda i, j: (0, j + 2 * nj)],
        nj, tn, extra=[], outs=[(act, col), (act, col)], epilogue=epilogue, tm=tm)

    tmo = min(256, seq)
    tiles_per_seq = seq // tmo
    halo = 8

    def prologue(ins):
        b_ref, u_ref, halo_ref, cw_ref, cb_ref = ins
        i = pl.program_id(0)
        u0 = u_ref[...]
        prev = jnp.where(i % tiles_per_seq == 0, 0.0, halo_ref[...])
        row = lax.broadcasted_iota(jnp.int32, u0.shape, 0)
        u1 = jnp.where(row == 0, prev[halo - 1:halo, :], pltpu.roll(u0, 1, 0))
        u2 = jnp.where(row == 0, prev[halo - 2:halo - 1, :],
                       jnp.where(row == 1, prev[halo - 1:halo, :], pltpu.roll(u0, 2, 0)))
        conv = cw_ref[0:1, :] * u2 + cw_ref[1:2, :] * u1 + cw_ref[2:3, :] * u0 + cb_ref[...]
        return (b_ref[...] * conv).astype(BF16)

    full = pl.BlockSpec((tmo, d), lambda i: (i, 0))
    return out_matmul(
        [(b_gate, full), (u, full),
         (u, pl.BlockSpec((halo, d), lambda i: (jnp.maximum(i * (tmo // halo) - 1, 0), 0))),
         (conv_w, pl.BlockSpec((CONV_WIDTH, d), lambda i: (0, 0))),
         (conv_b.reshape(1, d), pl.BlockSpec((1, d), lambda i: (0, 0)))],
        prologue, w_o, h, tm=tmo)


def _compress_body(x_ref, pos_ref, w1_ref, w2_ref, g_ref, kc_ref, vc_ref, lo_ref, hi_ref, *, groups):
    step = pl.program_id(1)
    stride = NSA_CMP_STRIDE

    @pl.when(step == 0)
    def _():
        lo_ref[...] = jnp.zeros_like(lo_ref)
        hi_ref[...] = jnp.zeros_like(hi_ref)

    for kv in range(2):
        p_lo = pos_ref[kv, pl.ds(step, 1), :]
        p_hi = pos_ref[kv, pl.ds(stride + step, 1), :]
        w_lo = w1_ref[kv, step]
        w_hi = w1_ref[kv, stride + step]
        for g in range(groups):
            c = kv * groups + g
            t = x_ref[:, c * HEAD_DIM:(c + 1) * HEAD_DIM]
            lo_ref[c] += jnp.dot((t + p_lo).astype(BF16), w_lo, preferred_element_type=F32)
            hi_ref[c] += jnp.dot((t + p_hi).astype(BF16), w_hi, preferred_element_type=F32)

    @pl.when(step == stride - 1)
    def _():
        n_half = lo_ref.shape[1]
        for kv in range(2):
            for g in range(groups):
                c = kv * groups + g
                pre = lo_ref[c] + pltpu.roll(hi_ref[c], n_half - 1, 0)
                out = jnp.dot(jax.nn.gelu(pre).astype(BF16), w2_ref[kv], preferred_element_type=F32)
                if kv == 0:
                    kc_ref[g] = _rms(out, g_ref[...]).astype(BF16)
                else:
                    vc_ref[g] = out.astype(BF16)


def nsa_compress(raw, cmp_pos, cmp_w1, cmp_w2, k_gain0, batch, seq):
    assert NSA_CMP_LEN == 2 * NSA_CMP_STRIDE
    stride = NSA_CMP_STRIDE
    groups = NSA_KV_GROUPS
    width = raw.shape[1]
    n_half = seq // stride
    x = raw.reshape(batch * n_half, stride * width)
    w1 = cmp_w1.reshape(2, NSA_CMP_LEN, HEAD_DIM, HEAD_DIM)
    out = jax.ShapeDtypeStruct((batch, groups, n_half, HEAD_DIM), BF16)
    out_spec = pl.BlockSpec((None, groups, n_half, HEAD_DIM), lambda b, s: (b, 0, 0, 0))
    return pl.pallas_call(
        functools.partial(_compress_body, groups=groups),
        grid=(batch, stride),
        in_specs=[pl.BlockSpec((n_half, width), lambda b, s: (b, s)),
                  pl.BlockSpec((2, NSA_CMP_LEN, HEAD_DIM), lambda b, s: (0, 0, 0)),
                  pl.BlockSpec((2, NSA_CMP_LEN, HEAD_DIM, HEAD_DIM), lambda b, s: (0, 0, 0, 0)),
                  pl.BlockSpec((2, HEAD_DIM, HEAD_DIM), lambda b, s: (0, 0, 0)),
                  pl.BlockSpec((1, HEAD_DIM), lambda b, s: (0, 0))],
        out_specs=[out_spec, out_spec],
        out_shape=[out, out],
        scratch_shapes=[pltpu.VMEM((2 * groups, n_half, HEAD_DIM), F32)] * 2,
        compiler_params=_params(("parallel", "arbitrary"), 32),
        name="nsa_compress",
    )(x, cmp_pos, w1, cmp_w2, k_gain0.reshape(1, HEAD_DIM))


def _cmp_select_body(q_ref, kc_ref, vc_ref, o_ref, sel_ref, *, n_cmp, n_top, q_per_kv):
    tq = q_ref.shape[0]
    n_pad = kc_ref.shape[0]
    n_slc = sel_ref.shape[1]
    scale = HEAD_DIM ** -0.5
    t = pl.program_id(2) * tq + lax.broadcasted_iota(jnp.int32, (tq, 1), 0)
    n = lax.broadcasted_iota(jnp.int32, (1, n_pad), 1)
    ok = (n < n_cmp) & (n * NSA_CMP_STRIDE + (NSA_CMP_LEN - 1) <= t)
    kc = kc_ref[...]
    vc = vc_ref[...]
    p_sum = jnp.zeros((tq, n_pad), F32)
    for r in range(q_per_kv):
        sl = slice(r * HEAD_DIM, (r + 1) * HEAD_DIM)
        s = jnp.where(ok, _nt_dot(q_ref[:, sl], kc) * scale, -jnp.inf)
        mx = jnp.max(s, axis=-1, keepdims=True)
        mx = jnp.where(mx > -jnp.inf, mx, 0.0)
        e = jnp.where(ok, jnp.exp(s - mx), 0.0)
        p = e / jnp.maximum(jnp.sum(e, axis=-1, keepdims=True), jnp.finfo(F32).tiny)
        o_ref[:, sl] = jnp.dot(p.astype(BF16), vc, preferred_element_type=F32).astype(BF16)
        p_sum = p_sum + p

    nn = lax.broadcasted_iota(jnp.int32, (n_pad, n_slc), 0) * NSA_CMP_STRIDE
    jj = lax.broadcasted_iota(jnp.int32, (n_pad, n_slc), 1) * NSA_SLC_LEN
    overlap = jnp.where((nn < jj + NSA_SLC_LEN) & (jj < nn + NSA_CMP_LEN)
                        & (nn < n_cmp * NSA_CMP_STRIDE), 1.0, 0.0).astype(BF16)
    p_hi = p_sum.astype(BF16)
    p_lo = (p_sum - p_hi.astype(F32)).astype(BF16)
    imp = (jnp.dot(p_hi, overlap, preferred_element_type=F32)
           + jnp.dot(p_lo, overlap, preferred_element_type=F32))

    cur = t // NSA_SLC_LEN
    jb = lax.broadcasted_iota(jnp.int32, (tq, n_slc), 1)
    forced = (jb == cur) | (jb == 0)
    val = jnp.where(forced, jnp.inf, jnp.where(jb <= cur, imp, -jnp.inf))
    keep = (_top_rank(val, jb) < n_top) & (val > -jnp.inf)
    sel_ref[...] = jnp.where(keep, 1.0, 0.0)


def nsa_cmp_select(q_cmp, k_cmp, v_cmp, batch, seq, tq=256):
    m, d = q_cmp.shape
    groups = NSA_KV_GROUPS
    q_per_kv = d // HEAD_DIM // groups
    gw = q_per_kv * HEAD_DIM
    tq = min(tq, seq)
    nq = seq // tq
    n_pad = k_cmp.shape[2]
    n_cmp = (seq - NSA_CMP_LEN) // NSA_CMP_STRIDE + 1
    n_slc = seq // NSA_SLC_LEN
    kv_spec = pl.BlockSpec((None, None, n_pad, HEAD_DIM), lambda b, g, i: (b, g, 0, 0))
    return pl.pallas_call(
        functools.partial(_cmp_select_body, n_cmp=n_cmp, n_top=min(NSA_SLC_TOPK, n_slc),
                          q_per_kv=q_per_kv),
        grid=(batch, groups, nq),
        in_specs=[pl.BlockSpec((tq, gw), lambda b, g, i: (b * nq + i, g)), kv_spec, kv_spec],
        out_specs=[pl.BlockSpec((tq, gw), lambda b, g, i: (b * nq + i, g)),
                   pl.BlockSpec((None, None, tq, n_slc), lambda b, g, i: (b, g, i, 0))],
        out_shape=[jax.ShapeDtypeStruct((m, d), BF16),
                   jax.ShapeDtypeStruct((batch, groups, seq, n_slc), F32)],
        compiler_params=_params(("parallel", "parallel", "parallel"), 32),
        name="nsa_cmp_select",
    )(q_cmp, k_cmp, v_cmp)


def _nsa_attn_body(q_ref, ks_ref, vs_ref, kw_ref, vw_ref, sel_ref, os_ref, ow_ref, *, q_per_kv):
    tq = q_ref.shape[0]
    n_slc = sel_ref.shape[1]
    kc = KV_CHUNK
    scale = HEAD_DIM ** -0.5
    q0 = pl.program_id(2) * tq
    q = jnp.concatenate([q_ref[:, r * HEAD_DIM:(r + 1) * HEAD_DIM] for r in range(q_per_kv)], axis=0)
    t = q0 + lax.broadcasted_iota(jnp.int32, (tq, 1), 0)
    sel = sel_ref[...].astype(BF16)
    rows = q_per_kv * tq

    def heads(mask):
        flag = jnp.where(mask, 1.0, 0.0)
        return jnp.concatenate([flag] * q_per_kv, axis=0) > 0.5

    def selected_chunk(c, carry):
        start = pl.multiple_of(c * kc, kc)
        kpos = start + lax.broadcasted_iota(jnp.int32, (1, kc), 1)
        spread = jnp.where(lax.broadcasted_iota(jnp.int32, (n_slc, kc), 0)
                           == (start + lax.broadcasted_iota(jnp.int32, (n_slc, kc), 1)) // NSA_SLC_LEN,
                           1.0, 0.0).astype(BF16)
        picked = jnp.dot(sel, spread, preferred_element_type=F32) > 0.5
        mask = picked & (kpos <= t)
        return _flash_step(q, ks_ref[pl.ds(start, kc), :], vs_ref[pl.ds(start, kc), :],
                           heads(mask), scale, carry)

    last = (q0 + tq - 1) // kc
    _, l, acc = lax.fori_loop(0, last + 1, selected_chunk, _flash_init(rows))
    out = acc / l
    for r in range(q_per_kv):
        os_ref[:, r * HEAD_DIM:(r + 1) * HEAD_DIM] = out[r * tq:(r + 1) * tq, :].astype(BF16)

    def window_chunk(c, carry):
        start = pl.multiple_of(c * kc, kc)
        kpos = start + lax.broadcasted_iota(jnp.int32, (1, kc), 1)
        mask = (kpos <= t) & (t - kpos < NSA_WINDOW)
        return _flash_step(q, kw_ref[pl.ds(start, kc), :], vw_ref[pl.ds(start, kc), :],
                           heads(mask), scale, carry)

    first = jnp.maximum(q0 - (NSA_WINDOW - 1), 0) // kc
    _, l, acc = lax.fori_loop(first, last + 1, window_chunk, _flash_init(rows))
    out = acc / l
    for r in range(q_per_kv):
        ow_ref[:, r * HEAD_DIM:(r + 1) * HEAD_DIM] = out[r * tq:(r + 1) * tq, :].astype(BF16)


def nsa_attention(q_rot, kv, sel, batch, seq, tq=128):
    m, d = q_rot.shape
    groups = NSA_KV_GROUPS
    q_per_kv = d // HEAD_DIM // groups
    gw = q_per_kv * HEAD_DIM
    tq = min(tq, seq)
    nq = seq // tq
    n_slc = sel.shape[3]
    assert seq % KV_CHUNK == 0 and KV_CHUNK % NSA_SLC_LEN == 0

    def kv_spec(part):
        return pl.BlockSpec((seq, HEAD_DIM), lambda b, g, i: (b, part * groups + g))

    q_spec = pl.BlockSpec((tq, gw), lambda b, g, i: (b * nq + i, g))
    out = jax.ShapeDtypeStruct((m, d), BF16)
    return pl.pallas_call(
        functools.partial(_nsa_attn_body, q_per_kv=q_per_kv),
        grid=(batch, groups, nq),
        in_specs=[q_spec, kv_spec(0), kv_spec(1), kv_spec(2), kv_spec(3),
                  pl.BlockSpec((None, None, tq, n_slc), lambda b, g, i: (b, g, i, 0))],
        out_specs=[q_spec, q_spec],
        out_shape=[out, out],
        compiler_params=_params(("parallel", "parallel", "arbitrary"), 32),
        name="nsa_attention",
    )(q_rot, kv, kv, kv, kv, sel)


def nsa_layer(h, gain, w_q, w_kv_raw, w_kv_rot, q_gain, k_gain, cmp_pos, cmp_w1, cmp_w2, w_gate, w_o,
              tables, batch, seq):
    m, d = h.shape
    n_heads = d // HEAD_DIM
    groups = NSA_KV_GROUPS
    gwk = groups * HEAD_DIM
    tn = 512
    tm = min(512, m)
    tab = pl.BlockSpec((tm, HEAD_DIM), lambda i, j: (i, 0))
    col = pl.BlockSpec((tm, tn), lambda i, j: (i, j))
    one_w = [lambda i, j: (0, j)]

    def q_epilogue(accs, j, extra, outs):
        g_ref, cos_ref, slo_ref, shi_ref = extra
        _head_norm_rope_store(accs[0], g_ref[...], (cos_ref[...], slo_ref[...], shi_ref[...]),
                              outs[0], outs[1])

    q_shape = jax.ShapeDtypeStruct((m, d), BF16)
    q_cmp, q_rot = norm_matmul(
        h, gain, w_q, one_w, d // tn, tn,
        extra=[(jnp.tile(q_gain, n_heads)[None], pl.BlockSpec((1, tn), lambda i, j: (0, j)))]
        + [(t, tab) for t in tables],
        outs=[(q_shape, col), (q_shape, col)], epilogue=q_epilogue, tm=tm)

    def raw_epilogue(accs, j, extra, outs):
        outs[0][...] = accs[0]

    (raw,) = norm_matmul(
        h, gain, w_kv_raw, one_w, 2 * gwk // tn, tn, extra=[],
        outs=[(jax.ShapeDtypeStruct((m, 2 * gwk), F32), col)], epilogue=raw_epilogue, tm=tm)

    assert tn == gwk
    kv_gain_row = jnp.concatenate([jnp.tile(k_gain[1], groups), jnp.ones((gwk,), F32),
                                   jnp.tile(k_gain[2], groups), jnp.ones((gwk,), F32)])[None]

    def kv_epilogue(accs, j, extra, outs):
        g_ref, cos_ref, slo_ref, shi_ref = extra

        @pl.when(j % 2 == 0)
        def _():
            _head_norm_rope_store(accs[0], g_ref[...], (cos_ref[...], slo_ref[...], shi_ref[...]),
                                  None, outs[0])

        @pl.when(j % 2 == 1)
        def _():
            outs[0][...] = accs[0].astype(BF16)

    (kv,) = norm_matmul(
        h, gain, w_kv_rot, one_w, 4, tn,
        extra=[(kv_gain_row, pl.BlockSpec((1, tn), lambda i, j: (0, j)))] + [(t, tab) for t in tables],
        outs=[(jax.ShapeDtypeStruct((m, 4 * gwk), BF16), col)], epilogue=kv_epilogue, tm=tm)

    def gate_epilogue(accs, j, extra, outs):
        outs[0][...] = jax.nn.sigmoid(accs[0])

    gate_w = w_gate.shape[1]
    (gates,) = norm_matmul(
        h, gain, w_gate, one_w, 1, gate_w, extra=[],
        outs=[(jax.ShapeDtypeStruct((m, gate_w), F32), pl.BlockSpec((tm, gate_w), lambda i, j: (i, j)))],
        epilogue=gate_epilogue, tm=tm)

    k_cmp, v_cmp = nsa_compress(raw, cmp_pos, cmp_w1, cmp_w2, k_gain[0], batch, seq)
    o_cmp, sel = nsa_cmp_select(q_cmp, k_cmp, v_cmp, batch, seq)
    o_slc, o_win = nsa_attention(q_rot, kv, sel, batch, seq)

    def prologue(ins):
        g_ref, oc_ref, os_ref, ow_ref = ins
        parts = []
        for hh in range(n_heads):
            sl = slice(hh * HEAD_DIM, (hh + 1) * HEAD_DIM)
            mixed = (g_ref[:, 3 * hh:3 * hh + 1] * oc_ref[:, sl].astype(F32)
                     + g_ref[:, 3 * hh + 1:3 * hh + 2] * os_ref[:, sl].astype(F32)
                     + g_ref[:, 3 * hh + 2:3 * hh + 3] * ow_ref[:, sl].astype(F32))
            parts.append(mixed.astype(BF16))
        return jnp.concatenate(parts, axis=1)

    tmo = min(512, m)
    full = pl.BlockSpec((tmo, d), lambda i: (i, 0))
    return out_matmul([(gates, pl.BlockSpec((tmo, gate_w), lambda i: (i, 0))),
                       (o_cmp, full), (o_slc, full), (o_win, full)], prologue, w_o, h, tm=tmo)


def kernel(x, p, positions, mixer_norm, mlp_norm, mlp_w1, mlp_w2, ple_norm, ple_gate, ple_proj,
           moba_w_qkv, moba_q_gain, moba_k_gain, moba_w_o, pool_w, pool_scale,
           nsa_w_q, nsa_w_kv, nsa_q_gain, nsa_k_gain, nsa_cmp_pos, nsa_cmp_w1, nsa_cmp_w2,
           nsa_w_gate, nsa_w_o, conv_w_in, conv_w, conv_b, conv_w_o):
    batch, seq, d = x.shape
    depth = p.shape[0]
    m = batch * seq
    n_heads = d // HEAD_DIM
    gwk = NSA_KV_GROUPS * HEAD_DIM
    tables = rope_tables(positions)
    bf = lambda w: w.astype(BF16)
    gate_pad = (-3 * n_heads) % HEAD_DIM

    h = x.reshape(m, d)
    for i in range(depth):
        kind, j = i % 4, i // 4
        if kind == 0:
            h = moba_layer(h, mixer_norm[i], bf(moba_w_qkv[j]), moba_q_gain[j], moba_k_gain[j],
                           bf(moba_w_o[j]), tables, batch, seq)
        elif kind == 1:
            h = pool_layer(h, mixer_norm[i], bf(pool_w[j]), pool_scale[j], seq)
        elif kind == 2:
            w_kv = nsa_w_kv[j]
            w_gate = jnp.pad(nsa_w_gate[j], ((0, 0), (0, gate_pad)))
            h = nsa_layer(h, mixer_norm[i], bf(nsa_w_q[j]), bf(w_kv[:, :2 * gwk]), bf(w_kv[:, 2 * gwk:]),
                          nsa_q_gain[j], nsa_k_gain[j], nsa_cmp_pos[j], bf(nsa_cmp_w1[j]),
                          bf(nsa_cmp_w2[j]), bf(w_gate), bf(nsa_w_o[j]), tables, batch, seq)
        else:
            h = conv_layer(h, mixer_norm[i], bf(conv_w_in[j]), conv_w[j], conv_b[j], bf(conv_w_o[j]), seq)
        h = mlp_layer(h, mlp_norm[i], bf(mlp_w1[i]), bf(mlp_w2[i]))
        h = ple_layer(h, ple_norm[i], bf(ple_gate[i]), p[i].reshape(m, -1), bf(ple_proj[i]))
    return h.reshape(batch, seq, d)
```

```python
import functools

import jax
import jax.numpy as jnp
from jax import lax
from jax.experimental import pallas as pl
from jax.experimental.pallas import tpu as pltpu

F32 = jnp.float32
BF16 = jnp.bfloat16

HEAD_DIM = 128
ROT_DIM = HEAD_DIM // 4
ROPE_THETA = 500000.0
NORM_EPS = 1e-6
MOBA_BLOCK = 256
MOBA_TOPK = 3
POOL_WINDOWS = (2, 4, 8, 16)
NSA_KV_GROUPS = 4
NSA_CMP_LEN = 32
NSA_CMP_STRIDE = 16
NSA_SLC_LEN = 64
NSA_SLC_TOPK = 16
NSA_WINDOW = 512
CONV_WIDTH = 3

MASKED_LOGIT = -1e30
SOFTMAX_EXP2_SCALE = HEAD_DIM ** -0.5 * 1.4426950408889634
KV_CHUNK = 256
MIB = 1024 * 1024


def _params(semantics, vmem_mib):
    return pltpu.CompilerParams(dimension_semantics=semantics,
                                vmem_limit_bytes=vmem_mib * MIB)


def _resident(shape, index_map):
    return pl.BlockSpec(shape, index_map, pipeline_mode=pl.Buffered(1))


def _rms(x, gain):
    ms = jnp.mean(x * x, axis=-1, keepdims=True)
    return x * lax.rsqrt(ms + NORM_EPS) * gain


def _rope(x, cos, sin_lo, sin_hi):
    half = ROT_DIM // 2
    return (x * cos + pltpu.roll(x, HEAD_DIM - half, 1) * sin_lo
            + pltpu.roll(x, half, 1) * sin_hi)


def _nt_dot(a, b):
    return lax.dot_general(a, b, (((1,), (1,)), ((), ())), preferred_element_type=F32)


def _rope_table_body(pos_ref, freq_ref, lo_ref, hi_ref, cos_ref, slo_ref, shi_ref):
    ang = pos_ref[...].astype(F32) * freq_ref[...]
    s = jnp.sin(ang)
    cos_ref[...] = jnp.cos(ang)
    slo_ref[...] = s * lo_ref[...]
    shi_ref[...] = s * hi_ref[...]


def rope_tables(positions):
    m = positions.size
    half = ROT_DIM // 2
    freqs = jnp.float32(ROPE_THETA) ** (-jnp.arange(half, dtype=F32) * 2.0 / ROT_DIM)
    zeros = jnp.zeros((HEAD_DIM - ROT_DIM,), F32)
    freq_row = jnp.concatenate([freqs, freqs, zeros])[None]
    lo_row = jnp.concatenate([-jnp.ones((half,), F32), jnp.zeros((half,), F32), zeros])[None]
    hi_row = jnp.concatenate([jnp.zeros((half,), F32), jnp.ones((half,), F32), zeros])[None]
    tm = min(m, 1024)
    row = pl.BlockSpec((1, HEAD_DIM), lambda i: (0, 0))
    tab = pl.BlockSpec((tm, HEAD_DIM), lambda i: (i, 0))
    return pl.pallas_call(
        _rope_table_body,
        grid=(m // tm,),
        in_specs=[pl.BlockSpec((tm, 1), lambda i: (i, 0)), row, row, row],
        out_specs=[tab, tab, tab],
        out_shape=[jax.ShapeDtypeStruct((m, HEAD_DIM), F32)] * 3,
        compiler_params=_params(("parallel",), 32),
        name="rope_tables",
    )(positions.reshape(m, 1), freq_row, lo_row, hi_row)


def _nmm_body(*refs, n_w, n_extra, epilogue):
    x_ref, g_ref = refs[0], refs[1]
    w_refs = refs[2:2 + n_w]
    extra = refs[2 + n_w:2 + n_w + n_extra]
    outs = refs[2 + n_w + n_extra:-1]
    xn_ref = refs[-1]
    j = pl.program_id(1)

    @pl.when(j == 0)
    def _():
        xn_ref[...] = _rms(x_ref[...], g_ref[...]).astype(BF16)

    xn = xn_ref[...]
    accs = [jnp.dot(xn, w[...], preferred_element_type=F32) for w in w_refs]
    epilogue(accs, j, extra, outs)


def norm_matmul(h, gain, w, w_maps, n_col_tiles, tn, extra, outs, epilogue, tm=512, vmem_mib=48):
    m, d = h.shape
    tm = min(tm, m)
    in_specs = [pl.BlockSpec((tm, d), lambda i, j: (i, 0)),
                pl.BlockSpec((1, d), lambda i, j: (0, 0))]
    in_specs += [pl.BlockSpec((d, tn), wm) for wm in w_maps]
    in_specs += [spec for _, spec in extra]
    return pl.pallas_call(
        functools.partial(_nmm_body, n_w=len(w_maps), n_extra=len(extra), epilogue=epilogue),
        grid=(m // tm, n_col_tiles),
        in_specs=in_specs,
        out_specs=[spec for _, spec in outs],
        out_shape=[shape for shape, _ in outs],
        scratch_shapes=[pltpu.VMEM((tm, d), BF16)],
        compiler_params=_params(("parallel", "arbitrary"), vmem_mib),
        name="norm_matmul",
    )(h, gain.reshape(1, d), *([w] * len(w_maps)), *[a for a, _ in extra])


def _head_norm_rope_store(acc, gain_row, tables, out_plain, out_rot):
    cos, slo, shi = tables
    for hh in range(acc.shape[1] // HEAD_DIM):
        sl = slice(hh * HEAD_DIM, (hh + 1) * HEAD_DIM)
        xh = _rms(acc[:, sl], gain_row[:, sl])
        if out_plain is not None:
            out_plain[:, sl] = xh.astype(BF16)
        if out_rot is not None:
            out_rot[:, sl] = _rope(xh, cos, slo, shi).astype(BF16)


def _omm_body(*refs, n_in, prologue):
    ins = refs[:n_in]
    w_ref, res_ref, o_ref = refs[n_in:n_in + 3]
    a = prologue(ins)
    o_ref[...] = res_ref[...] + jnp.dot(a, w_ref[...], preferred_element_type=F32)


def out_matmul(ins, prologue, w, res, tm=512, vmem_mib=48):
    m, d = res.shape
    tm = min(tm, m)
    k = w.shape[0]
    return pl.pallas_call(
        functools.partial(_omm_body, n_in=len(ins), prologue=prologue),
        grid=(m // tm,),
        in_specs=[spec for _, spec in ins] + [
            _resident((k, d), lambda i: (0, 0)),
            pl.BlockSpec((tm, d), lambda i: (i, 0))],
        out_specs=pl.BlockSpec((tm, d), lambda i: (i, 0)),
        out_shape=jax.ShapeDtypeStruct((m, d), F32),
        compiler_params=_params(("parallel",), vmem_mib),
        name="out_matmul",
    )(*[a for a, _ in ins], w, res)


def _mlp_body(x_ref, g_ref, w1_ref, w2_ref, o_ref, xn_ref, acc_ref):
    f = pl.program_id(1)

    @pl.when(f == 0)
    def _():
        xn_ref[...] = _rms(x_ref[...], g_ref[...]).astype(BF16)
        acc_ref[...] = jnp.zeros_like(acc_ref)

    a = jnp.dot(xn_ref[...], w1_ref[...], preferred_element_type=F32)
    a = jnp.square(jnp.maximum(a, 0.0)).astype(BF16)
    acc_ref[...] += jnp.dot(a, w2_ref[...], preferred_element_type=F32)

    @pl.when(f == pl.num_programs(1) - 1)
    def _():
        o_ref[...] = x_ref[...] + acc_ref[...]


def mlp_layer(h, gain, w1, w2, layer, tm=512, tf=512):
    m, d = h.shape
    dff = w1.shape[2]
    tm = min(tm, m)
    return pl.pallas_call(
        _mlp_body,
        grid=(m // tm, dff // tf),
        in_specs=[pl.BlockSpec((tm, d), lambda i, f: (i, 0)),
                  pl.BlockSpec((1, d), lambda i, f: (0, 0)),
                  pl.BlockSpec((None, d, tf), lambda i, f: (layer, 0, f)),
                  pl.BlockSpec((None, tf, d), lambda i, f: (layer, f, 0))],
        out_specs=pl.BlockSpec((tm, d), lambda i, f: (i, 0)),
        out_shape=jax.ShapeDtypeStruct((m, d), F32),
        scratch_shapes=[pltpu.VMEM((tm, d), BF16), pltpu.VMEM((tm, d), F32)],
        compiler_params=_params(("parallel", "arbitrary"), 48),
        name="mlp",
    )(h, gain.reshape(1, d), w1, w2)


def _ple_body(x_ref, g_ref, wg_ref, p_ref, wp_ref, o_ref):
    x = x_ref[...]
    xn = _rms(x, g_ref[...]).astype(BF16)
    gate = jax.nn.sigmoid(jnp.dot(xn, wg_ref[...], preferred_element_type=F32))
    emb = jnp.dot(p_ref[...].astype(BF16), wp_ref[...], preferred_element_type=F32)
    o_ref[...] = x + gate * emb


def ple_layer(h, gain, wg, p, wp, layer, tm=512):
    m, d = h.shape
    pd = p.shape[2]
    tm = min(tm, m)
    return pl.pallas_call(
        _ple_body,
        grid=(m // tm,),
        in_specs=[pl.BlockSpec((tm, d), lambda i: (i, 0)),
                  pl.BlockSpec((1, d), lambda i: (0, 0)),
                  _resident((None, d, d), lambda i: (layer, 0, 0)),
                  pl.BlockSpec((None, tm, pd), lambda i: (layer, i, 0)),
                  _resident((None, pd, d), lambda i: (layer, 0, 0))],
        out_specs=pl.BlockSpec((tm, d), lambda i: (i, 0)),
        out_shape=jax.ShapeDtypeStruct((m, d), F32),
        compiler_params=_params(("parallel",), 48),
        name="ple",
    )(h, gain.reshape(1, d), wg, p, wp)


def _flash_buffers(chains, keys, queries):
    return [pltpu.VMEM((chains, keys, queries), F32), pltpu.VMEM((chains, keys, queries), BF16),
            pltpu.VMEM((chains, HEAD_DIM, queries), F32)]


def _flash_logits(buf, k, q_t, bias):
    s = jnp.dot(k, q_t, preferred_element_type=F32) + bias
    buf[0][...] = s
    return jnp.max(s, axis=0, keepdims=True)


def _flash_start(buf, k, q_t, bias):
    s_ref, p_ref, acc_ref = buf
    p_ref[...] = jnp.zeros_like(p_ref)
    acc_ref[...] = jnp.zeros_like(acc_ref)
    queries = s_ref.shape[1]
    return (_flash_logits(buf, k, q_t, bias), jnp.ones((1, queries), F32),
            jnp.full((1, queries), MASKED_LOGIT, F32), jnp.zeros((1, queries), F32))


def _flash_stage(bufs, states, v_prev, following):
    products = [jnp.dot(v, buf[1][...], preferred_element_type=F32) for v, buf in zip(v_prev, bufs)]
    if following is not None:
        upcoming = [jnp.dot(k, q_t, preferred_element_type=F32) for k, q_t, _ in following]
    out = []
    for c, ((s_ref, p_ref, acc_ref), (s_max, alpha_prev, m, l), pv) in enumerate(zip(bufs, states, products)):
        m_new = jnp.maximum(m, s_max)
        p = jnp.exp2((s_ref[...] - m_new) * SOFTMAX_EXP2_SCALE)
        alpha = jnp.exp2((m - m_new) * SOFTMAX_EXP2_SCALE)
        l = alpha * l + jnp.sum(p, axis=0, keepdims=True)
        p_ref[...] = p.astype(BF16)
        acc_ref[...] = alpha_prev * acc_ref[...] + pv
        if following is not None:
            s_next = upcoming[c] + following[c][2]
            s_ref[...] = s_next
            s_max = jnp.max(s_next, axis=0, keepdims=True)
        out.append((s_max, alpha, m_new, l))
    return tuple(out)


def _flash_finish(buf, state, v_last):
    _, p_ref, acc_ref = buf
    _, alpha, _, l = state
    return (alpha * acc_ref[...] + jnp.dot(v_last, p_ref[...], preferred_element_type=F32)) / l


def _store_transposed(v_ref, vt_ref, chunk):
    for n in range(vt_ref.shape[0]):
        vt_ref[n] = v_ref[n * chunk:(n + 1) * chunk, :].astype(F32).T.astype(BF16)


def _top_rank(v, row):
    rank = jnp.zeros(v.shape, jnp.int32)
    for j in range(v.shape[0]):
        r = v[j:j + 1, :]
        rank = rank + jnp.where(row > j, jnp.where(r >= v, 1, 0), jnp.where(r > v, 1, 0))
    return rank


def _moba_body(q_ref, k_ref, v_ref, o_ref, kmean_ref, vt_ref, bias_ref, s_ref, p_ref, acc_ref,
               *, n_blocks, heads):
    blk = MOBA_BLOCK
    own = pl.program_id(2)
    cols = [slice(e * HEAD_DIM, (e + 1) * HEAD_DIM) for e in range(heads)]
    bufs = [(s_ref.at[e], p_ref.at[e], acc_ref.at[e]) for e in range(heads)]

    @pl.when(own == 0)
    def _():
        for e in range(heads):
            for n in range(n_blocks):
                kb = k_ref[n * blk:(n + 1) * blk, cols[e]].astype(F32)
                kmean_ref[e, n:n + 1, :] = jnp.mean(kb, axis=0, keepdims=True)
                vt_ref[e, n] = v_ref[n * blk:(n + 1) * blk, cols[e]].astype(F32).T.astype(BF16)

    q_ts = [q_ref[:, cols[e]].astype(F32).T.astype(BF16) for e in range(heads)]
    for e in range(heads):
        gate = jnp.dot(kmean_ref[e].astype(BF16), q_ts[e], preferred_element_type=F32)
        row = lax.broadcasted_iota(jnp.int32, gate.shape, 0)
        gate = jnp.where(row < own, gate, -jnp.inf)
        rank = _top_rank(gate, row)
        bias_ref[e] = jnp.where(row < own, jnp.where(rank < MOBA_TOPK, 0.0, MASKED_LOGIT), MASKED_LOGIT)

    def keys(j):
        start = pl.multiple_of(j * blk, blk)
        return [k_ref[pl.ds(start, blk), cols[e]] for e in range(heads)]

    def values(j):
        return [vt_ref[e, j] for e in range(heads)]

    def stage(j, states, next_biases):
        following = None if next_biases is None else list(zip(keys(j + 1), q_ts, next_biases))
        return _flash_stage(bufs, states, values(jnp.maximum(j - 1, 0)), following)

    def chosen_bias(j):
        return [bias_ref[e, pl.ds(j, 1), :] for e in range(heads)]

    causal = jnp.where(lax.broadcasted_iota(jnp.int32, (blk, blk), 0)
                       <= lax.broadcasted_iota(jnp.int32, (blk, blk), 1), 0.0, MASKED_LOGIT)
    first_bias = [jnp.where(own == 0, causal, b) for b in chosen_bias(0)]
    states = tuple(_flash_start(buf, k, q_t, b) for buf, k, q_t, b in zip(bufs, keys(0), q_ts, first_bias))
    states = lax.fori_loop(0, own - 1, lambda j, st: stage(j, st, chosen_bias(j + 1)), states)
    states = lax.cond(own > 0, lambda st: stage(own - 1, st, [causal] * heads), lambda st: st, states)
    states = stage(own, states, None)
    for e, v_last in enumerate(values(own)):
        o_ref[:, cols[e]] = _flash_finish(bufs[e], states[e], v_last).T.astype(BF16)


def moba_attention(qkv, batch, seq, n_heads, heads_per_step=4):
    blk = MOBA_BLOCK
    nq = seq // blk
    hp = heads_per_step
    width = hp * HEAD_DIM
    groups = n_heads // hp
    return pl.pallas_call(
        functools.partial(_moba_body, n_blocks=nq, heads=hp),
        grid=(batch, groups, nq),
        in_specs=[pl.BlockSpec((blk, width), lambda b, h, i: (b * nq + i, h)),
                  pl.BlockSpec((seq, width), lambda b, h, i: (b, groups + h)),
                  pl.BlockSpec((seq, width), lambda b, h, i: (b, 2 * groups + h))],
        out_specs=pl.BlockSpec((blk, width), lambda b, h, i: (b * nq + i, h)),
        out_shape=jax.ShapeDtypeStruct((batch * seq, n_heads * HEAD_DIM), BF16),
        scratch_shapes=[pltpu.VMEM((hp, nq, HEAD_DIM), F32),
                        pltpu.VMEM((hp, nq, HEAD_DIM, blk), BF16),
                        pltpu.VMEM((hp, nq, blk), F32)] + _flash_buffers(hp, blk, blk),
        compiler_params=_params(("parallel", "parallel", "arbitrary"), 32),
        name="moba_attention",
    )(qkv, qkv, qkv)


def moba_layer(h, gain, w_qkv, q_gain, k_gain, w_o, tables, batch, seq):
    m, d = h.shape
    n_heads = d // HEAD_DIM
    tn = 512
    n_qk_tiles = 2 * d // tn
    gain_row = jnp.concatenate([jnp.tile(q_gain, n_heads), jnp.tile(k_gain, n_heads),
                                jnp.ones((d,), F32)])[None]

    def epilogue(accs, j, extra, outs):
        g_ref, cos_ref, slo_ref, shi_ref = extra
        (o_ref,) = outs

        @pl.when(j < n_qk_tiles)
        def _():
            _head_norm_rope_store(accs[0], g_ref[...], (cos_ref[...], slo_ref[...], shi_ref[...]),
                                  None, o_ref)

        @pl.when(j >= n_qk_tiles)
        def _():
            o_ref[...] = accs[0].astype(BF16)

    tm = min(512, m)
    tab = pl.BlockSpec((tm, HEAD_DIM), lambda i, j: (i, 0))
    (qkv,) = norm_matmul(
        h, gain, w_qkv, [lambda i, j: (0, j)], 3 * d // tn, tn,
        extra=[(gain_row, pl.BlockSpec((1, tn), lambda i, j: (0, j)))] + [(t, tab) for t in tables],
        outs=[(jax.ShapeDtypeStruct((m, 3 * d), BF16), pl.BlockSpec((tm, tn), lambda i, j: (i, j)))],
        epilogue=epilogue, tm=tm)
    o = moba_attention(qkv, batch, seq, n_heads)
    tmo = min(512, m)
    return out_matmul([(o, pl.BlockSpec((tmo, d), lambda i: (i, 0)))],
                      lambda ins: ins[0][...], w_o, h, tm=tmo)


def _pool_body(x_ref, halo_ref, g_ref, w_ref, s_ref, o_ref, *, tiles_per_seq, halo):
    i = pl.program_id(0)
    tm, d = x_ref.shape
    group = d // len(POOL_WINDOWS)
    x = x_ref[...]
    gain = g_ref[...]
    xn = _rms(x, gain)
    prev = jnp.where(i % tiles_per_seq == 0, 0.0, _rms(halo_ref[...], gain))
    pos = (i % tiles_per_seq) * tm + lax.broadcasted_iota(jnp.int32, (tm, 1), 0)
    for g, win in enumerate(POOL_WINDOWS):
        sl = slice(g * group, (g + 1) * group)
        run = jnp.concatenate([prev[:, sl], xn[:, sl]], axis=0)
        span = 1
        while span < win:
            run = run + pltpu.roll(run, span, 0)
            span *= 2
        cnt = jnp.minimum(pos + 1, win).astype(F32)
        mean = run[halo:, :] / cnt
        mix = jnp.dot((mean - xn[:, sl]).astype(BF16), w_ref[g], preferred_element_type=F32)
        o_ref[:, sl] = x[:, sl] + mix * s_ref[:, sl]


def pool_layer(h, gain, w_groups, scale, seq, tm=512):
    m, d = h.shape
    halo = 16
    assert max(POOL_WINDOWS) <= halo
    tm = min(tm, seq)
    group = d // len(POOL_WINDOWS)
    return pl.pallas_call(
        functools.partial(_pool_body, tiles_per_seq=seq // tm, halo=halo),
        grid=(m // tm,),
        in_specs=[pl.BlockSpec((tm, d), lambda i: (i, 0)),
                  pl.BlockSpec((halo, d), lambda i: (jnp.maximum(i * (tm // halo) - 1, 0), 0)),
                  pl.BlockSpec((1, d), lambda i: (0, 0)),
                  _resident((len(POOL_WINDOWS), group, group), lambda i: (0, 0, 0)),
                  pl.BlockSpec((1, d), lambda i: (0, 0))],
        out_specs=pl.BlockSpec((tm, d), lambda i: (i, 0)),
        out_shape=jax.ShapeDtypeStruct((m, d), F32),
        compiler_params=_params(("parallel",), 48),
        name="pool_mixer",
    )(h, h, gain.reshape(1, d), w_groups, scale.reshape(1, d))


def conv_layer(h, gain, w_in, conv_w, conv_b, w_o, seq):
    m, d = h.shape
    tn = 512
    nj = d // tn

    def epilogue(accs, j, extra, outs):
        b_ref, u_ref = outs
        b_ref[...] = accs[0]
        u_ref[...] = accs[1] * accs[2]

    tm = min(512, m)
    col = pl.BlockSpec((tm, tn), lambda i, j: (i, j))
    act = jax.ShapeDtypeStruct((m, d), F32)
    b_gate, u = norm_matmul(
        h, gain, w_in, [lambda i, j: (0, j), lambda i, j: (0, j + nj), lambda i, j: (0, j + 2 * nj)],
        nj, tn, extra=[], outs=[(act, col), (act, col)], epilogue=epilogue, tm=tm)

    tmo = min(256, seq)
    tiles_per_seq = seq // tmo
    halo = 8

    def prologue(ins):
        b_ref, u_ref, halo_ref, cw_ref, cb_ref = ins
        i = pl.program_id(0)
        u0 = u_ref[...]
        prev = jnp.where(i % tiles_per_seq == 0, 0.0, halo_ref[...])
        row = lax.broadcasted_iota(jnp.int32, u0.shape, 0)
        u1 = jnp.where(row == 0, prev[halo - 1:halo, :], pltpu.roll(u0, 1, 0))
        u2 = jnp.where(row == 0, prev[halo - 2:halo - 1, :],
                       jnp.where(row == 1, prev[halo - 1:halo, :], pltpu.roll(u0, 2, 0)))
        conv = cw_ref[0:1, :] * u2 + cw_ref[1:2, :] * u1 + cw_ref[2:3, :] * u0 + cb_ref[...]
        return (b_ref[...] * conv).astype(BF16)

    full = pl.BlockSpec((tmo, d), lambda i: (i, 0))
    return out_matmul(
        [(b_gate, full), (u, full),
         (u, pl.BlockSpec((halo, d), lambda i: (jnp.maximum(i * (tmo // halo) - 1, 0), 0))),
         (conv_w, pl.BlockSpec((CONV_WIDTH, d), lambda i: (0, 0))),
         (conv_b.reshape(1, d), pl.BlockSpec((1, d), lambda i: (0, 0)))],
        prologue, w_o, h, tm=tmo)


def _compress_body(x_ref, pos_ref, w1_ref, w2_ref, g_ref, kc_ref, vc_ref, lo_ref, hi_ref, *, groups):
    step = pl.program_id(1)
    stride = NSA_CMP_STRIDE

    @pl.when(step == 0)
    def _():
        lo_ref[...] = jnp.zeros_like(lo_ref)
        hi_ref[...] = jnp.zeros_like(hi_ref)

    for kv in range(2):
        p_lo = pos_ref[kv, pl.ds(step, 1), :]
        p_hi = pos_ref[kv, pl.ds(stride + step, 1), :]
        w_lo = w1_ref[kv, step]
        w_hi = w1_ref[kv, stride + step]
        for g in range(groups):
            c = kv * groups + g
            t = x_ref[:, c * HEAD_DIM:(c + 1) * HEAD_DIM]
            lo_ref[c] += jnp.dot((t + p_lo).astype(BF16), w_lo, preferred_element_type=F32)
            hi_ref[c] += jnp.dot((t + p_hi).astype(BF16), w_hi, preferred_element_type=F32)

    @pl.when(step == stride - 1)
    def _():
        n_half = lo_ref.shape[1]
        for kv in range(2):
            for g in range(groups):
                c = kv * groups + g
                pre = lo_ref[c] + pltpu.roll(hi_ref[c], n_half - 1, 0)
                hid = jax.nn.gelu(pre)
                if kv == 0:
                    out = jnp.dot(hid.astype(BF16), w2_ref[0], preferred_element_type=F32)
                    kc_ref[g] = _rms(out, g_ref[...]).astype(BF16)
                else:
                    vc_ref[g] = jnp.dot(w2_ref[1], hid.T.astype(BF16),
                                        preferred_element_type=F32).astype(BF16)


def nsa_compress(raw, cmp_pos, cmp_w1, cmp_w2, k_gain0, batch, seq):
    assert NSA_CMP_LEN == 2 * NSA_CMP_STRIDE
    stride = NSA_CMP_STRIDE
    groups = NSA_KV_GROUPS
    width = raw.shape[1]
    n_half = seq // stride
    x = raw.reshape(batch * n_half, stride * width)
    w1 = cmp_w1.reshape(2, NSA_CMP_LEN, HEAD_DIM, HEAD_DIM)
    cmp_w2 = jnp.stack([cmp_w2[0], cmp_w2[1].T])
    k_out = jax.ShapeDtypeStruct((batch, groups, n_half, HEAD_DIM), BF16)
    v_out = jax.ShapeDtypeStruct((batch, groups, HEAD_DIM, n_half), BF16)
    k_spec = pl.BlockSpec((None, groups, n_half, HEAD_DIM), lambda b, s: (b, 0, 0, 0))
    v_spec = pl.BlockSpec((None, groups, HEAD_DIM, n_half), lambda b, s: (b, 0, 0, 0))
    return pl.pallas_call(
        functools.partial(_compress_body, groups=groups),
        grid=(batch, stride),
        in_specs=[pl.BlockSpec((n_half, width), lambda b, s: (b, s)),
                  pl.BlockSpec((2, NSA_CMP_LEN, HEAD_DIM), lambda b, s: (0, 0, 0)),
                  pl.BlockSpec((2, NSA_CMP_LEN, HEAD_DIM, HEAD_DIM), lambda b, s: (0, 0, 0, 0)),
                  pl.BlockSpec((2, HEAD_DIM, HEAD_DIM), lambda b, s: (0, 0, 0)),
                  pl.BlockSpec((1, HEAD_DIM), lambda b, s: (0, 0))],
        out_specs=[k_spec, v_spec],
        out_shape=[k_out, v_out],
        scratch_shapes=[pltpu.VMEM((2 * groups, n_half, HEAD_DIM), F32)] * 2,
        compiler_params=_params(("parallel", "arbitrary"), 32),
        name="nsa_compress",
    )(x, cmp_pos, w1, cmp_w2, k_gain0.reshape(1, HEAD_DIM))


def _cmp_select_body(q_ref, kc_ref, vct_ref, o_ref, sel_ref, *, n_cmp, n_top, q_per_kv):
    tq = q_ref.shape[0]
    n_pad = kc_ref.shape[0]
    n_slc = sel_ref.shape[0]
    scale = HEAD_DIM ** -0.5
    t = pl.program_id(2) * tq + lax.broadcasted_iota(jnp.int32, (1, tq), 1)
    n = lax.broadcasted_iota(jnp.int32, (n_pad, 1), 0)
    ok = (n < n_cmp) & (n * NSA_CMP_STRIDE + (NSA_CMP_LEN - 1) <= t)
    kc = kc_ref[...]
    vct = vct_ref[...]
    p_sum = jnp.zeros((n_pad, tq), F32)
    for r in range(q_per_kv):
        sl = slice(r * HEAD_DIM, (r + 1) * HEAD_DIM)
        s = jnp.where(ok, _nt_dot(kc, q_ref[:, sl]) * scale, -jnp.inf)
        mx = jnp.max(s, axis=0, keepdims=True)
        mx = jnp.where(mx > -jnp.inf, mx, 0.0)
        e = jnp.where(ok, jnp.exp(s - mx), 0.0)
        den = jnp.maximum(jnp.sum(e, axis=0, keepdims=True), jnp.finfo(F32).tiny)
        o_t = jnp.dot(vct, e.astype(BF16), preferred_element_type=F32) / den
        o_ref[:, sl] = o_t.T.astype(BF16)
        p_sum = p_sum + e / den

    jj = lax.broadcasted_iota(jnp.int32, (n_slc, n_pad), 0) * NSA_SLC_LEN
    nn = lax.broadcasted_iota(jnp.int32, (n_slc, n_pad), 1) * NSA_CMP_STRIDE
    overlap = jnp.where((nn < jj + NSA_SLC_LEN) & (jj < nn + NSA_CMP_LEN)
                        & (nn < n_cmp * NSA_CMP_STRIDE), 1.0, 0.0).astype(BF16)
    p_hi = p_sum.astype(BF16)
    p_lo = (p_sum - p_hi.astype(F32)).astype(BF16)
    imp = (jnp.dot(overlap, p_hi, preferred_element_type=F32)
           + jnp.dot(overlap, p_lo, preferred_element_type=F32))

    cur = t // NSA_SLC_LEN
    jb = lax.broadcasted_iota(jnp.int32, (n_slc, tq), 0)
    val = jnp.where(jb == cur, jnp.inf,
                    jnp.where(jb == 0, jnp.inf, jnp.where(jb < cur, imp, -jnp.inf)))
    rank = _top_rank(val, jb)
    sel_ref[...] = jnp.where(rank < n_top, jnp.where(val > -jnp.inf, 1.0, 0.0), 0.0)


def nsa_cmp_select(q_cmp, k_cmp, v_cmp_t, batch, seq, tq=256):
    m, d = q_cmp.shape
    groups = NSA_KV_GROUPS
    q_per_kv = d // HEAD_DIM // groups
    gw = q_per_kv * HEAD_DIM
    tq = min(tq, seq)
    nq = seq // tq
    n_pad = k_cmp.shape[2]
    n_cmp = (seq - NSA_CMP_LEN) // NSA_CMP_STRIDE + 1
    n_slc = seq // NSA_SLC_LEN
    return pl.pallas_call(
        functools.partial(_cmp_select_body, n_cmp=n_cmp, n_top=min(NSA_SLC_TOPK, n_slc),
                          q_per_kv=q_per_kv),
        grid=(batch, groups, nq),
        in_specs=[pl.BlockSpec((tq, gw), lambda b, g, i: (b * nq + i, g)),
                  pl.BlockSpec((None, None, n_pad, HEAD_DIM), lambda b, g, i: (b, g, 0, 0)),
                  pl.BlockSpec((None, None, HEAD_DIM, n_pad), lambda b, g, i: (b, g, 0, 0))],
        out_specs=[pl.BlockSpec((tq, gw), lambda b, g, i: (b * nq + i, g)),
                   pl.BlockSpec((None, None, n_slc, tq), lambda b, g, i: (b, g, 0, i))],
        out_shape=[jax.ShapeDtypeStruct((m, d), BF16),
                   jax.ShapeDtypeStruct((batch, groups, n_slc, seq), F32)],
        compiler_params=_params(("parallel", "parallel", "parallel"), 32),
        name="nsa_cmp_select",
    )(q_cmp, k_cmp, v_cmp_t)


def _nsa_attn_body(q_ref, ks_ref, vs_ref, kw_ref, vw_ref, sel_ref, os_ref, ow_ref,
                   vst_ref, vwt_ref, bias_ref, s_ref, p_ref, acc_ref, *, q_per_kv, per):
    tq = q_ref.shape[0]
    kc = KV_CHUNK
    sub = kc // NSA_SLC_LEN
    step = pl.program_id(2)
    q0 = step * tq

    @pl.when(step == 0)
    def _():
        _store_transposed(vs_ref, vst_ref, kc)
        _store_transposed(vw_ref, vwt_ref, kc)

    n_chain = q_per_kv // per
    q_ts = [jnp.concatenate([q_ref[:, r * HEAD_DIM:(r + 1) * HEAD_DIM]
                             for r in range(c * per, (c + 1) * per)], axis=0).astype(F32).T.astype(BF16)
            for c in range(n_chain)]
    t = q0 + lax.broadcasted_iota(jnp.int32, (1, tq), 1)
    bias_ref[...] = jnp.where(sel_ref[...] > 0.5, 0.0, MASKED_LOGIT)

    def chain_bias(bias):
        return jnp.concatenate([bias] * per, axis=1)

    def selected_bias(c, kpos):
        picked = jnp.concatenate(
            [jnp.broadcast_to(bias_ref[pl.ds(c * sub + a, 1), :], (NSA_SLC_LEN, tq)) for a in range(sub)],
            axis=0)
        return chain_bias(jnp.where(kpos <= t, picked, MASKED_LOGIT))

    def window_bias(kpos):
        return chain_bias(jnp.where(kpos <= t, jnp.where(t - kpos < NSA_WINDOW, 0.0, MASKED_LOGIT),
                                    MASKED_LOGIT))

    def selected_logits(c):
        start = pl.multiple_of(c * kc, kc)
        kpos = start + lax.broadcasted_iota(jnp.int32, (kc, 1), 0)
        k, bias = ks_ref[pl.ds(start, kc), :], selected_bias(c, kpos)
        return [(k, q_t, bias) for q_t in q_ts]

    def window_logits(c):
        start = pl.multiple_of(c * kc, kc)
        kpos = start + lax.broadcasted_iota(jnp.int32, (kc, 1), 0)
        k, bias = kw_ref[pl.ds(start, kc), :], window_bias(kpos)
        return [(k, q_t, bias) for q_t in q_ts]

    bufs = [(s_ref.at[c], p_ref.at[c], acc_ref.at[c]) for c in range(2 * n_chain)]

    def selected_only(c, states):
        return _flash_stage(bufs[:n_chain], states, [vst_ref[jnp.maximum(c - 1, 0)]] * n_chain,
                            selected_logits(c + 1))

    def selected_and_window(c, states, is_last=False):
        prev = jnp.maximum(c - 1, 0)
        following = None if is_last else selected_logits(c + 1) + window_logits(c + 1)
        return _flash_stage(bufs, states, [vst_ref[prev]] * n_chain + [vwt_ref[prev]] * n_chain, following)

    first = jnp.maximum(q0 - (NSA_WINDOW - 1), 0) // kc
    last = (q0 + tq - 1) // kc
    states = tuple(_flash_start(buf, *f) for buf, f in zip(bufs[:n_chain], selected_logits(0)))
    states = lax.fori_loop(0, first, selected_only, states)
    states = states + tuple(_flash_start(buf, *f) for buf, f in zip(bufs[n_chain:], window_logits(first)))
    states = lax.fori_loop(first, last, selected_and_window, states)
    states = selected_and_window(last, states, is_last=True)

    for c, state in enumerate(states):
        out_ref, v_last = (os_ref, vst_ref[last]) if c < n_chain else (ow_ref, vwt_ref[last])
        out = _flash_finish(bufs[c], state, v_last)
        for i in range(per):
            r = (c % n_chain) * per + i
            out_ref[:, r * HEAD_DIM:(r + 1) * HEAD_DIM] = out[:, i * tq:(i + 1) * tq].T.astype(BF16)


def nsa_attention(q_rot, kv, sel, batch, seq, tq=256):
    m, d = q_rot.shape
    groups = NSA_KV_GROUPS
    q_per_kv = d // HEAD_DIM // groups
    gw = q_per_kv * HEAD_DIM
    tq = min(tq, seq)
    nq = seq // tq
    n_slc = sel.shape[2]
    assert seq % KV_CHUNK == 0 and KV_CHUNK % NSA_SLC_LEN == 0

    def kv_spec(part):
        return pl.BlockSpec((seq, HEAD_DIM), lambda b, g, i: (b, part * groups + g))

    q_spec = pl.BlockSpec((tq, gw), lambda b, g, i: (b * nq + i, g))
    out = jax.ShapeDtypeStruct((m, d), BF16)
    v_t = pltpu.VMEM((seq // KV_CHUNK, HEAD_DIM, KV_CHUNK), BF16)
    per = 1
    return pl.pallas_call(
        functools.partial(_nsa_attn_body, q_per_kv=q_per_kv, per=per),
        grid=(batch, groups, nq),
        in_specs=[q_spec, kv_spec(0), kv_spec(1), kv_spec(2), kv_spec(3),
                  pl.BlockSpec((None, None, n_slc, tq), lambda b, g, i: (b, g, 0, i))],
        out_specs=[q_spec, q_spec],
        out_shape=[out, out],
        scratch_shapes=[v_t, v_t, pltpu.VMEM((n_slc, tq), F32)]
        + _flash_buffers(2 * q_per_kv // per, KV_CHUNK, per * tq),
        compiler_params=_params(("parallel", "parallel", "arbitrary"), 32),
        name="nsa_attention",
    )(q_rot, kv, kv, kv, kv, sel)


def nsa_layer(h, gain, w_q, w_kv_raw, w_kv_rot, q_gain, k_gain, cmp_pos, cmp_w1, cmp_w2, w_gate, w_o,
              tables, batch, seq):
    m, d = h.shape
    n_heads = d // HEAD_DIM
    groups = NSA_KV_GROUPS
    gwk = groups * HEAD_DIM
    tn = 512
    tm = min(512, m)
    tab = pl.BlockSpec((tm, HEAD_DIM), lambda i, j: (i, 0))
    col = pl.BlockSpec((tm, tn), lambda i, j: (i, j))
    one_w = [lambda i, j: (0, j)]

    def q_epilogue(accs, j, extra, outs):
        g_ref, cos_ref, slo_ref, shi_ref = extra
        _head_norm_rope_store(accs[0], g_ref[...], (cos_ref[...], slo_ref[...], shi_ref[...]),
                              outs[0], outs[1])

    q_shape = jax.ShapeDtypeStruct((m, d), BF16)
    q_cmp, q_rot = norm_matmul(
        h, gain, w_q, one_w, d // tn, tn,
        extra=[(jnp.tile(q_gain, n_heads)[None], pl.BlockSpec((1, tn), lambda i, j: (0, j)))]
        + [(t, tab) for t in tables],
        outs=[(q_shape, col), (q_shape, col)], epilogue=q_epilogue, tm=tm)

    def raw_epilogue(accs, j, extra, outs):
        outs[0][...] = accs[0]

    (raw,) = norm_matmul(
        h, gain, w_kv_raw, one_w, 2 * gwk // tn, tn, extra=[],
        outs=[(jax.ShapeDtypeStruct((m, 2 * gwk), F32), col)], epilogue=raw_epilogue, tm=tm)

    assert tn == gwk
    kv_gain_row = jnp.concatenate([jnp.tile(k_gain[1], groups), jnp.ones((gwk,), F32),
                                   jnp.tile(k_gain[2], groups), jnp.ones((gwk,), F32)])[None]

    def kv_epilogue(accs, j, extra, outs):
        g_ref, cos_ref, slo_ref, shi_ref = extra

        @pl.when(j % 2 == 0)
        def _():
            _head_norm_rope_store(accs[0], g_ref[...], (cos_ref[...], slo_ref[...], shi_ref[...]),
                                  None, outs[0])

        @pl.when(j % 2 == 1)
        def _():
            outs[0][...] = accs[0].astype(BF16)

    (kv,) = norm_matmul(
        h, gain, w_kv_rot, one_w, 4, tn,
        extra=[(kv_gain_row, pl.BlockSpec((1, tn), lambda i, j: (0, j)))] + [(t, tab) for t in tables],
        outs=[(jax.ShapeDtypeStruct((m, 4 * gwk), BF16), col)], epilogue=kv_epilogue, tm=tm)

    def gate_epilogue(accs, j, extra, outs):
        outs[0][...] = jax.nn.sigmoid(accs[0])

    gate_w = w_gate.shape[1]
    (gates,) = norm_matmul(
        h, gain, w_gate, one_w, 1, gate_w, extra=[],
        outs=[(jax.ShapeDtypeStruct((m, gate_w), F32), pl.BlockSpec((tm, gate_w), lambda i, j: (i, j)))],
        epilogue=gate_epilogue, tm=tm)

    k_cmp, v_cmp = nsa_compress(raw, cmp_pos, cmp_w1, cmp_w2, k_gain[0], batch, seq)
    o_cmp, sel = nsa_cmp_select(q_cmp, k_cmp, v_cmp, batch, seq)
    o_slc, o_win = nsa_attention(q_rot, kv, sel, batch, seq)

    def prologue(ins):
        g_ref, oc_ref, os_ref, ow_ref = ins
        parts = []
        for hh in range(n_heads):
            sl = slice(hh * HEAD_DIM, (hh + 1) * HEAD_DIM)
            mixed = (g_ref[:, 3 * hh:3 * hh + 1] * oc_ref[:, sl].astype(F32)
                     + g_ref[:, 3 * hh + 1:3 * hh + 2] * os_ref[:, sl].astype(F32)
                     + g_ref[:, 3 * hh + 2:3 * hh + 3] * ow_ref[:, sl].astype(F32))
            parts.append(mixed.astype(BF16))
        return jnp.concatenate(parts, axis=1)

    tmo = min(512, m)
    full = pl.BlockSpec((tmo, d), lambda i: (i, 0))
    return out_matmul([(gates, pl.BlockSpec((tmo, gate_w), lambda i: (i, 0))),
                       (o_cmp, full), (o_slc, full), (o_win, full)], prologue, w_o, h, tm=tmo)


def kernel(x, p, positions, mixer_norm, mlp_norm, mlp_w1, mlp_w2, ple_norm, ple_gate, ple_proj,
           moba_w_qkv, moba_q_gain, moba_k_gain, moba_w_o, pool_w, pool_scale,
           nsa_w_q, nsa_w_kv, nsa_q_gain, nsa_k_gain, nsa_cmp_pos, nsa_cmp_w1, nsa_cmp_w2,
           nsa_w_gate, nsa_w_o, conv_w_in, conv_w, conv_b, conv_w_o):
    batch, seq, d = x.shape
    depth = p.shape[0]
    m = batch * seq
    n_heads = d // HEAD_DIM
    gwk = NSA_KV_GROUPS * HEAD_DIM
    tables = rope_tables(positions)
    bf = lambda w: w.astype(BF16)
    gate_pad = (-3 * n_heads) % HEAD_DIM
    mlp_w1_bf, mlp_w2_bf = bf(mlp_w1), bf(mlp_w2)
    ple_gate_bf, ple_proj_bf = bf(ple_gate), bf(ple_proj)
    p_rows = p.reshape(depth, m, -1)

    h = x.reshape(m, d)
    for i in range(depth):
        kind, j = i % 4, i // 4
        if kind == 0:
            h = moba_layer(h, mixer_norm[i], bf(moba_w_qkv[j]), moba_q_gain[j], moba_k_gain[j],
                           bf(moba_w_o[j]), tables, batch, seq)
        elif kind == 1:
            h = pool_layer(h, mixer_norm[i], bf(pool_w[j]), pool_scale[j], seq)
        elif kind == 2:
            w_kv = nsa_w_kv[j]
            w_gate = jnp.pad(nsa_w_gate[j], ((0, 0), (0, gate_pad)))
            h = nsa_layer(h, mixer_norm[i], bf(nsa_w_q[j]), bf(w_kv[:, :2 * gwk]), bf(w_kv[:, 2 * gwk:]),
                          nsa_q_gain[j], nsa_k_gain[j], nsa_cmp_pos[j], bf(nsa_cmp_w1[j]),
                          bf(nsa_cmp_w2[j]), bf(w_gate), bf(nsa_w_o[j]), tables, batch, seq)
        else:
            h = conv_layer(h, mixer_norm[i], bf(conv_w_in[j]), conv_w[j], conv_b[j], bf(conv_w_o[j]), seq)
        h = mlp_layer(h, mlp_norm[i], mlp_w1_bf, mlp_w2_bf, i)
        h = ple_layer(h, ple_norm[i], ple_gate_bf, p_rows, ple_proj_bf, i)
    return h.reshape(batch, seq, d)
```

```python
import functools

import jax
import jax.numpy as jnp
from jax import lax
from jax.experimental import pallas as pl
from jax.experimental.pallas import tpu as pltpu

F32 = jnp.float32
BF16 = jnp.bfloat16

HEAD_DIM = 128
ROT_DIM = HEAD_DIM // 4
ROPE_THETA = 500000.0
NORM_EPS = 1e-6
MOBA_BLOCK = 256
MOBA_TOPK = 3
POOL_WINDOWS = (2, 4, 8, 16)
NSA_KV_GROUPS = 4
NSA_CMP_LEN = 32
NSA_CMP_STRIDE = 16
NSA_SLC_LEN = 64
NSA_SLC_TOPK = 16
NSA_WINDOW = 512
CONV_WIDTH = 3

MASKED_LOGIT = -1e30
SOFTMAX_EXP2_SCALE = HEAD_DIM ** -0.5 * 1.4426950408889634
KV_CHUNK = 256
MIB = 1024 * 1024


def _params(semantics, vmem_mib):
    return pltpu.CompilerParams(dimension_semantics=semantics,
                                vmem_limit_bytes=vmem_mib * MIB)


def _resident(shape, index_map):
    return pl.BlockSpec(shape, index_map, pipeline_mode=pl.Buffered(1))


def _rms(x, gain):
    ms = jnp.mean(x * x, axis=-1, keepdims=True)
    return x * lax.rsqrt(ms + NORM_EPS) * gain


def _rope(x, cos, sin_lo, sin_hi):
    half = ROT_DIM // 2
    return (x * cos + pltpu.roll(x, HEAD_DIM - half, 1) * sin_lo
            + pltpu.roll(x, half, 1) * sin_hi)


def _nt_dot(a, b):
    return lax.dot_general(a, b, (((1,), (1,)), ((), ())), preferred_element_type=F32)


def _rope_table_body(pos_ref, freq_ref, lo_ref, hi_ref, cos_ref, slo_ref, shi_ref):
    ang = pos_ref[...].astype(F32) * freq_ref[...]
    s = jnp.sin(ang)
    cos_ref[...] = jnp.cos(ang)
    slo_ref[...] = s * lo_ref[...]
    shi_ref[...] = s * hi_ref[...]


def rope_tables(positions):
    m = positions.size
    half = ROT_DIM // 2
    freqs = jnp.float32(ROPE_THETA) ** (-jnp.arange(half, dtype=F32) * 2.0 / ROT_DIM)
    zeros = jnp.zeros((HEAD_DIM - ROT_DIM,), F32)
    freq_row = jnp.concatenate([freqs, freqs, zeros])[None]
    lo_row = jnp.concatenate([-jnp.ones((half,), F32), jnp.zeros((half,), F32), zeros])[None]
    hi_row = jnp.concatenate([jnp.zeros((half,), F32), jnp.ones((half,), F32), zeros])[None]
    tm = min(m, 1024)
    row = pl.BlockSpec((1, HEAD_DIM), lambda i: (0, 0))
    tab = pl.BlockSpec((tm, HEAD_DIM), lambda i: (i, 0))
    return pl.pallas_call(
        _rope_table_body,
        grid=(m // tm,),
        in_specs=[pl.BlockSpec((tm, 1), lambda i: (i, 0)), row, row, row],
        out_specs=[tab, tab, tab],
        out_shape=[jax.ShapeDtypeStruct((m, HEAD_DIM), F32)] * 3,
        compiler_params=_params(("parallel",), 32),
        name="rope_tables",
    )(positions.reshape(m, 1), freq_row, lo_row, hi_row)


def _nmm_body(*refs, n_w, n_extra, sub, variants):
    x_ref, g_ref = refs[0], refs[1]
    w_refs = refs[2:2 + n_w]
    extra = refs[2 + n_w:2 + n_w + n_extra]
    outs = refs[2 + n_w + n_extra:-1]
    xn_ref = refs[-1]
    j = pl.program_id(1)

    @pl.when(j == 0)
    def _():
        xn_ref[...] = _rms(x_ref[...], g_ref[...]).astype(BF16)

    width = w_refs[0].shape[1] // sub
    cols = [slice(s * width, (s + 1) * width) for s in range(sub)]

    def run(epilogue):
        xn = xn_ref[...]
        accs = [[jnp.dot(xn, w[:, c], preferred_element_type=F32) for w in w_refs] for c in cols]
        for c, acc in zip(cols, accs):
            epilogue(acc, c, extra, outs)

    if len(variants) == 1:
        run(variants[0][1])
    else:
        for applies, epilogue in variants:
            pl.when(applies(j))(functools.partial(run, epilogue))


def norm_matmul(h, gain, w, w_maps, n_col_tiles, tn, extra, outs, variants, sub=1, tm=1024, vmem_mib=56):
    m, d = h.shape
    tm = min(tm, m)
    in_specs = [pl.BlockSpec((tm, d), lambda i, j: (i, 0)),
                pl.BlockSpec((1, d), lambda i, j: (0, 0))]
    in_specs += [pl.BlockSpec((d, tn), wm) for wm in w_maps]
    in_specs += [spec for _, spec in extra]
    return pl.pallas_call(
        functools.partial(_nmm_body, n_w=len(w_maps), n_extra=len(extra), sub=sub, variants=variants),
        grid=(m // tm, n_col_tiles),
        in_specs=in_specs,
        out_specs=[spec for _, spec in outs],
        out_shape=[shape for shape, _ in outs],
        scratch_shapes=[pltpu.VMEM((tm, d), BF16)],
        compiler_params=_params(("parallel", "arbitrary"), vmem_mib),
        name="norm_matmul",
    )(h, gain.reshape(1, d), *([w] * len(w_maps)), *[a for a, _ in extra])


def _head_norm_rope_store(acc, cols, gain_ref, table_refs, out_plain, out_rot):
    cos, slo, shi = (t[...] for t in table_refs)
    for hh in range(acc.shape[1] // HEAD_DIM):
        sl = slice(hh * HEAD_DIM, (hh + 1) * HEAD_DIM)
        dst = slice(cols.start + hh * HEAD_DIM, cols.start + (hh + 1) * HEAD_DIM)
        xh = _rms(acc[:, sl], gain_ref[:, dst])
        if out_plain is not None:
            out_plain[:, dst] = xh.astype(BF16)
        if out_rot is not None:
            out_rot[:, dst] = _rope(xh, cos, slo, shi).astype(BF16)


def _omm_body(*refs, n_in, prologue):
    ins = refs[:n_in]
    w_ref, res_ref, o_ref = refs[n_in:n_in + 3]
    a = prologue(ins)
    o_ref[...] = res_ref[...] + jnp.dot(a, w_ref[...], preferred_element_type=F32)


def out_matmul(ins, prologue, w, res, tm=512, vmem_mib=48):
    m, d = res.shape
    tm = min(tm, m)
    k = w.shape[0]
    return pl.pallas_call(
        functools.partial(_omm_body, n_in=len(ins), prologue=prologue),
        grid=(m // tm,),
        in_specs=[spec for _, spec in ins] + [
            _resident((k, d), lambda i: (0, 0)),
            pl.BlockSpec((tm, d), lambda i: (i, 0))],
        out_specs=pl.BlockSpec((tm, d), lambda i: (i, 0)),
        out_shape=jax.ShapeDtypeStruct((m, d), F32),
        compiler_params=_params(("parallel",), vmem_mib),
        name="out_matmul",
    )(*[a for a, _ in ins], w, res)


def _mlp_body(x_ref, g_ref, w1_ref, w2_ref, o_ref, xn_ref):
    @pl.when(pl.program_id(1) == 0)
    def _():
        x = x_ref[...]
        xn_ref[...] = _rms(x, g_ref[...]).astype(BF16)
        o_ref[...] = x

    a = jnp.dot(xn_ref[...], w1_ref[...], preferred_element_type=F32)
    a = jnp.square(jnp.maximum(a, 0.0)).astype(BF16)
    o_ref[...] += jnp.dot(a, w2_ref[...], preferred_element_type=F32)


def mlp_layer(h, gain, w1, w2, layer, tm=1024, tf=512):
    m, d = h.shape
    dff = w1.shape[2]
    tm = min(tm, m)
    return pl.pallas_call(
        _mlp_body,
        grid=(m // tm, dff // tf),
        in_specs=[pl.BlockSpec((tm, d), lambda i, f: (i, 0)),
                  pl.BlockSpec((1, d), lambda i, f: (0, 0)),
                  pl.BlockSpec((None, d, tf), lambda i, f: (layer, 0, f)),
                  pl.BlockSpec((None, tf, d), lambda i, f: (layer, f, 0))],
        out_specs=pl.BlockSpec((tm, d), lambda i, f: (i, 0)),
        out_shape=jax.ShapeDtypeStruct((m, d), F32),
        scratch_shapes=[pltpu.VMEM((tm, d), BF16)],
        compiler_params=_params(("parallel", "arbitrary"), 56),
        name="mlp",
    )(h, gain.reshape(1, d), w1, w2)


def _ple_body(x_ref, g_ref, wg_ref, p_ref, wp_ref, o_ref):
    x = x_ref[...]
    xn = _rms(x, g_ref[...]).astype(BF16)
    gate = jax.nn.sigmoid(jnp.dot(xn, wg_ref[...], preferred_element_type=F32))
    emb = jnp.dot(p_ref[...].astype(BF16), wp_ref[...], preferred_element_type=F32)
    o_ref[...] = x + gate * emb


def ple_layer(h, gain, wg, p, wp, layer, tm=512):
    m, d = h.shape
    pd = p.shape[2]
    tm = min(tm, m)
    return pl.pallas_call(
        _ple_body,
        grid=(m // tm,),
        in_specs=[pl.BlockSpec((tm, d), lambda i: (i, 0)),
                  pl.BlockSpec((1, d), lambda i: (0, 0)),
                  _resident((None, d, d), lambda i: (layer, 0, 0)),
                  pl.BlockSpec((None, tm, pd), lambda i: (layer, i, 0)),
                  _resident((None, pd, d), lambda i: (layer, 0, 0))],
        out_specs=pl.BlockSpec((tm, d), lambda i: (i, 0)),
        out_shape=jax.ShapeDtypeStruct((m, d), F32),
        compiler_params=_params(("parallel",), 48),
        name="ple",
    )(h, gain.reshape(1, d), wg, p, wp)


def _flash_buffers(chains, keys, queries):
    return [pltpu.VMEM((chains, keys, queries), F32), pltpu.VMEM((chains, keys, queries), BF16),
            pltpu.VMEM((chains, HEAD_DIM, queries), F32)]


def _flash_logits(buf, k, q_t, bias):
    s = jnp.dot(k, q_t, preferred_element_type=F32) + bias
    buf[0][...] = s
    return jnp.max(s, axis=0, keepdims=True)


def _flash_start(buf, k, q_t, bias):
    s_ref, p_ref, acc_ref = buf
    p_ref[...] = jnp.zeros_like(p_ref)
    acc_ref[...] = jnp.zeros_like(acc_ref)
    queries = s_ref.shape[1]
    return (_flash_logits(buf, k, q_t, bias), jnp.ones((1, queries), F32),
            jnp.full((1, queries), MASKED_LOGIT, F32), jnp.zeros((1, queries), F32))


def _flash_stage(bufs, states, v_prev, following):
    products = [jnp.dot(v, buf[1][...], preferred_element_type=F32) for v, buf in zip(v_prev, bufs)]
    if following is not None:
        upcoming = [jnp.dot(k, q_t, preferred_element_type=F32) for k, q_t, _ in following]
    out = []
    for c, ((s_ref, p_ref, acc_ref), (s_max, alpha_prev, m, l), pv) in enumerate(zip(bufs, states, products)):
        m_new = jnp.maximum(m, s_max)
        p = jnp.exp2((s_ref[...] - m_new) * SOFTMAX_EXP2_SCALE)
        alpha = jnp.exp2((m - m_new) * SOFTMAX_EXP2_SCALE)
        l = alpha * l + jnp.sum(p, axis=0, keepdims=True)
        p_ref[...] = p.astype(BF16)
        acc_ref[...] = alpha_prev * acc_ref[...] + pv
        if following is not None:
            s_next = upcoming[c] + following[c][2]
            s_ref[...] = s_next
            s_max = jnp.max(s_next, axis=0, keepdims=True)
        out.append((s_max, alpha, m_new, l))
    return tuple(out)


def _flash_finish(buf, state, v_last):
    _, p_ref, acc_ref = buf
    _, alpha, _, l = state
    return (alpha * acc_ref[...] + jnp.dot(v_last, p_ref[...], preferred_element_type=F32)) / l


def _store_transposed(v_ref, vt_ref, chunk):
    for n in range(vt_ref.shape[0]):
        vt_ref[n] = v_ref[n * chunk:(n + 1) * chunk, :].astype(F32).T.astype(BF16)


def _top_rank(v, row):
    rank = jnp.zeros(v.shape, jnp.int32)
    for j in range(v.shape[0]):
        r = v[j:j + 1, :]
        rank = rank + jnp.where(row > j, jnp.where(r >= v, 1, 0), jnp.where(r > v, 1, 0))
    return rank


def _moba_body(q_ref, k_ref, v_ref, o_ref, kmean_ref, vt_ref, bias_ref, s_ref, p_ref, acc_ref,
               *, n_blocks, heads):
    blk = MOBA_BLOCK
    own = pl.program_id(2)
    cols = [slice(e * HEAD_DIM, (e + 1) * HEAD_DIM) for e in range(heads)]
    bufs = [(s_ref.at[e], p_ref.at[e], acc_ref.at[e]) for e in range(heads)]

    @pl.when(own == 0)
    def _():
        for e in range(heads):
            for n in range(n_blocks):
                kb = k_ref[n * blk:(n + 1) * blk, cols[e]].astype(F32)
                kmean_ref[e, n:n + 1, :] = jnp.mean(kb, axis=0, keepdims=True)
                vt_ref[e, n] = v_ref[n * blk:(n + 1) * blk, cols[e]].astype(F32).T.astype(BF16)

    q_ts = [q_ref[:, cols[e]].astype(F32).T.astype(BF16) for e in range(heads)]
    for e in range(heads):
        gate = jnp.dot(kmean_ref[e].astype(BF16), q_ts[e], preferred_element_type=F32)
        row = lax.broadcasted_iota(jnp.int32, gate.shape, 0)
        gate = jnp.where(row < own, gate, -jnp.inf)
        rank = _top_rank(gate, row)
        bias_ref[e] = jnp.where(row < own, jnp.where(rank < MOBA_TOPK, 0.0, MASKED_LOGIT), MASKED_LOGIT)

    def keys(j):
        start = pl.multiple_of(j * blk, blk)
        return [k_ref[pl.ds(start, blk), cols[e]] for e in range(heads)]

    def values(j):
        return [vt_ref[e, j] for e in range(heads)]

    def stage(j, states, next_biases):
        following = None if next_biases is None else list(zip(keys(j + 1), q_ts, next_biases))
        return _flash_stage(bufs, states, values(jnp.maximum(j - 1, 0)), following)

    def chosen_bias(j):
        return [bias_ref[e, pl.ds(j, 1), :] for e in range(heads)]

    causal = jnp.where(lax.broadcasted_iota(jnp.int32, (blk, blk), 0)
                       <= lax.broadcasted_iota(jnp.int32, (blk, blk), 1), 0.0, MASKED_LOGIT)
    first_bias = [jnp.where(own == 0, causal, b) for b in chosen_bias(0)]
    states = tuple(_flash_start(buf, k, q_t, b) for buf, k, q_t, b in zip(bufs, keys(0), q_ts, first_bias))
    states = lax.fori_loop(0, own - 1, lambda j, st: stage(j, st, chosen_bias(j + 1)), states)
    states = lax.cond(own > 0, lambda st: stage(own - 1, st, [causal] * heads), lambda st: st, states)
    states = stage(own, states, None)
    for e, v_last in enumerate(values(own)):
        o_ref[:, cols[e]] = _flash_finish(bufs[e], states[e], v_last).T.astype(BF16)


def moba_attention(qkv, batch, seq, n_heads, heads_per_step=4):
    blk = MOBA_BLOCK
    nq = seq // blk
    hp = heads_per_step
    width = hp * HEAD_DIM
    groups = n_heads // hp
    return pl.pallas_call(
        functools.partial(_moba_body, n_blocks=nq, heads=hp),
        grid=(batch, groups, nq),
        in_specs=[pl.BlockSpec((blk, width), lambda b, h, i: (b * nq + i, h)),
                  pl.BlockSpec((seq, width), lambda b, h, i: (b, groups + h)),
                  pl.BlockSpec((seq, width), lambda b, h, i: (b, 2 * groups + h))],
        out_specs=pl.BlockSpec((blk, width), lambda b, h, i: (b * nq + i, h)),
        out_shape=jax.ShapeDtypeStruct((batch * seq, n_heads * HEAD_DIM), BF16),
        scratch_shapes=[pltpu.VMEM((hp, nq, HEAD_DIM), F32),
                        pltpu.VMEM((hp, nq, HEAD_DIM, blk), BF16),
                        pltpu.VMEM((hp, nq, blk), F32)] + _flash_buffers(hp, blk, blk),
        compiler_params=_params(("parallel", "parallel", "arbitrary"), 32),
        name="moba_attention",
    )(qkv, qkv, qkv)


def moba_layer(h, gain, w_qkv, q_gain, k_gain, w_o, tables, batch, seq):
    m, d = h.shape
    n_heads = d // HEAD_DIM
    tn = 1024
    n_qk_tiles = 2 * d // tn
    gain_row = jnp.concatenate([jnp.tile(q_gain, n_heads), jnp.tile(k_gain, n_heads),
                                jnp.ones((d,), F32)])[None]

    def qk_epilogue(accs, cols, extra, outs):
        _head_norm_rope_store(accs[0], cols, extra[0], extra[1:], None, outs[0])

    def v_epilogue(accs, cols, extra, outs):
        outs[0][:, cols] = accs[0].astype(BF16)

    tm = min(1024, m)
    tab = pl.BlockSpec((tm, HEAD_DIM), lambda i, j: (i, 0))
    (qkv,) = norm_matmul(
        h, gain, w_qkv, [lambda i, j: (0, j)], 3 * d // tn, tn,
        extra=[(gain_row, pl.BlockSpec((1, tn), lambda i, j: (0, j)))] + [(t, tab) for t in tables],
        outs=[(jax.ShapeDtypeStruct((m, 3 * d), BF16), pl.BlockSpec((tm, tn), lambda i, j: (i, j)))],
        variants=[(lambda j: j < n_qk_tiles, qk_epilogue), (lambda j: j >= n_qk_tiles, v_epilogue)],
        sub=4, tm=tm)
    o = moba_attention(qkv, batch, seq, n_heads)
    tmo = min(512, m)
    return out_matmul([(o, pl.BlockSpec((tmo, d), lambda i: (i, 0)))],
                      lambda ins: ins[0][...], w_o, h, tm=tmo)


def _pool_body(x_ref, halo_ref, g_ref, w_ref, s_ref, o_ref, *, tiles_per_seq, halo):
    i = pl.program_id(0)
    tm, d = x_ref.shape
    group = d // len(POOL_WINDOWS)
    x = x_ref[...]
    gain = g_ref[...]
    xn = _rms(x, gain)
    prev = jnp.where(i % tiles_per_seq == 0, 0.0, _rms(halo_ref[...], gain))
    pos = (i % tiles_per_seq) * tm + lax.broadcasted_iota(jnp.int32, (tm, 1), 0)
    for g, win in enumerate(POOL_WINDOWS):
        sl = slice(g * group, (g + 1) * group)
        run = jnp.concatenate([prev[:, sl], xn[:, sl]], axis=0)
        span = 1
        while span < win:
            run = run + pltpu.roll(run, span, 0)
            span *= 2
        cnt = jnp.minimum(pos + 1, win).astype(F32)
        mean = run[halo:, :] / cnt
        mix = jnp.dot((mean - xn[:, sl]).astype(BF16), w_ref[g], preferred_element_type=F32)
        o_ref[:, sl] = x[:, sl] + mix * s_ref[:, sl]


def pool_layer(h, gain, w_groups, scale, seq, tm=512):
    m, d = h.shape
    halo = 16
    assert max(POOL_WINDOWS) <= halo
    tm = min(tm, seq)
    group = d // len(POOL_WINDOWS)
    return pl.pallas_call(
        functools.partial(_pool_body, tiles_per_seq=seq // tm, halo=halo),
        grid=(m // tm,),
        in_specs=[pl.BlockSpec((tm, d), lambda i: (i, 0)),
                  pl.BlockSpec((halo, d), lambda i: (jnp.maximum(i * (tm // halo) - 1, 0), 0)),
                  pl.BlockSpec((1, d), lambda i: (0, 0)),
                  _resident((len(POOL_WINDOWS), group, group), lambda i: (0, 0, 0)),
                  pl.BlockSpec((1, d), lambda i: (0, 0))],
        out_specs=pl.BlockSpec((tm, d), lambda i: (i, 0)),
        out_shape=jax.ShapeDtypeStruct((m, d), F32),
        compiler_params=_params(("parallel",), 48),
        name="pool_mixer",
    )(h, h, gain.reshape(1, d), w_groups, scale.reshape(1, d))


def conv_layer(h, gain, w_in, conv_w, conv_b, w_o, seq):
    m, d = h.shape
    tn = 512
    nj = d // tn

    def epilogue(accs, cols, extra, outs):
        b_ref, u_ref = outs
        b_ref[:, cols] = accs[0]
        u_ref[:, cols] = accs[1] * accs[2]

    tm = min(1024, m)
    col = pl.BlockSpec((tm, tn), lambda i, j: (i, j))
    act = jax.ShapeDtypeStruct((m, d), F32)
    b_gate, u = norm_matmul(
        h, gain, w_in, [lambda i, j: (0, j), lambda i, j: (0, j + nj), lambda i, j: (0, j + 2 * nj)],
        nj, tn, extra=[], outs=[(act, col), (act, col)], variants=[(None, epilogue)], sub=2, tm=tm)

    tmo = min(256, seq)
    tiles_per_seq = seq // tmo
    halo = 8

    def prologue(ins):
        b_ref, u_ref, halo_ref, cw_ref, cb_ref = ins
        i = pl.program_id(0)
        u0 = u_ref[...]
        prev = jnp.where(i % tiles_per_seq == 0, 0.0, halo_ref[...])
        row = lax.broadcasted_iota(jnp.int32, u0.shape, 0)
        u1 = jnp.where(row == 0, prev[halo - 1:halo, :], pltpu.roll(u0, 1, 0))
        u2 = jnp.where(row == 0, prev[halo - 2:halo - 1, :],
                       jnp.where(row == 1, prev[halo - 1:halo, :], pltpu.roll(u0, 2, 0)))
        conv = cw_ref[0:1, :] * u2 + cw_ref[1:2, :] * u1 + cw_ref[2:3, :] * u0 + cb_ref[...]
        return (b_ref[...] * conv).astype(BF16)

    full = pl.BlockSpec((tmo, d), lambda i: (i, 0))
    return out_matmul(
        [(b_gate, full), (u, full),
         (u, pl.BlockSpec((halo, d), lambda i: (jnp.maximum(i * (tmo // halo) - 1, 0), 0))),
         (conv_w, pl.BlockSpec((CONV_WIDTH, d), lambda i: (0, 0))),
         (conv_b.reshape(1, d), pl.BlockSpec((1, d), lambda i: (0, 0)))],
        prologue, w_o, h, tm=tmo)


def _compress_body(x_ref, pos_ref, w1_ref, w2_ref, g_ref, kc_ref, vc_ref, lo_ref, hi_ref, *, groups):
    step = pl.program_id(1)
    stride = NSA_CMP_STRIDE

    @pl.when(step == 0)
    def _():
        lo_ref[...] = jnp.zeros_like(lo_ref)
        hi_ref[...] = jnp.zeros_like(hi_ref)

    for kv in range(2):
        p_lo = pos_ref[kv, pl.ds(step, 1), :]
        p_hi = pos_ref[kv, pl.ds(stride + step, 1), :]
        w_lo = w1_ref[kv, step]
        w_hi = w1_ref[kv, stride + step]
        for g in range(groups):
            c = kv * groups + g
            t = x_ref[:, c * HEAD_DIM:(c + 1) * HEAD_DIM]
            lo_ref[c] += jnp.dot((t + p_lo).astype(BF16), w_lo, preferred_element_type=F32)
            hi_ref[c] += jnp.dot((t + p_hi).astype(BF16), w_hi, preferred_element_type=F32)

    @pl.when(step == stride - 1)
    def _():
        n_half = lo_ref.shape[1]
        for kv in range(2):
            for g in range(groups):
                c = kv * groups + g
                pre = lo_ref[c] + pltpu.roll(hi_ref[c], n_half - 1, 0)
                hid = jax.nn.gelu(pre)
                if kv == 0:
                    out = jnp.dot(hid.astype(BF16), w2_ref[0], preferred_element_type=F32)
                    kc_ref[g] = _rms(out, g_ref[...]).astype(BF16)
                else:
                    vc_ref[g] = jnp.dot(w2_ref[1], hid.T.astype(BF16),
                                        preferred_element_type=F32).astype(BF16)


def nsa_compress(raw, cmp_pos, cmp_w1, cmp_w2, k_gain0, batch, seq):
    assert NSA_CMP_LEN == 2 * NSA_CMP_STRIDE
    stride = NSA_CMP_STRIDE
    groups = NSA_KV_GROUPS
    width = raw.shape[1]
    n_half = seq // stride
    x = raw.reshape(batch * n_half, stride * width)
    w1 = cmp_w1.reshape(2, NSA_CMP_LEN, HEAD_DIM, HEAD_DIM)
    cmp_w2 = jnp.stack([cmp_w2[0], cmp_w2[1].T])
    k_out = jax.ShapeDtypeStruct((batch, groups, n_half, HEAD_DIM), BF16)
    v_out = jax.ShapeDtypeStruct((batch, groups, HEAD_DIM, n_half), BF16)
    k_spec = pl.BlockSpec((None, groups, n_half, HEAD_DIM), lambda b, s: (b, 0, 0, 0))
    v_spec = pl.BlockSpec((None, groups, HEAD_DIM, n_half), lambda b, s: (b, 0, 0, 0))
    return pl.pallas_call(
        functools.partial(_compress_body, groups=groups),
        grid=(batch, stride),
        in_specs=[pl.BlockSpec((n_half, width), lambda b, s: (b, s)),
                  pl.BlockSpec((2, NSA_CMP_LEN, HEAD_DIM), lambda b, s: (0, 0, 0)),
                  pl.BlockSpec((2, NSA_CMP_LEN, HEAD_DIM, HEAD_DIM), lambda b, s: (0, 0, 0, 0)),
                  pl.BlockSpec((2, HEAD_DIM, HEAD_DIM), lambda b, s: (0, 0, 0)),
                  pl.BlockSpec((1, HEAD_DIM), lambda b, s: (0, 0))],
        out_specs=[k_spec, v_spec],
        out_shape=[k_out, v_out],
        scratch_shapes=[pltpu.VMEM((2 * groups, n_half, HEAD_DIM), F32)] * 2,
        compiler_params=_params(("parallel", "arbitrary"), 32),
        name="nsa_compress",
    )(x, cmp_pos, w1, cmp_w2, k_gain0.reshape(1, HEAD_DIM))


def _cmp_select_body(q_ref, kc_ref, vct_ref, o_ref, sel_ref, *, n_cmp, n_top, q_per_kv):
    tq = q_ref.shape[0]
    n_pad = kc_ref.shape[0]
    n_slc = sel_ref.shape[0]
    scale = HEAD_DIM ** -0.5
    t = pl.program_id(2) * tq + lax.broadcasted_iota(jnp.int32, (1, tq), 1)
    n = lax.broadcasted_iota(jnp.int32, (n_pad, 1), 0)
    ok = (n < n_cmp) & (n * NSA_CMP_STRIDE + (NSA_CMP_LEN - 1) <= t)
    kc = kc_ref[...]
    vct = vct_ref[...]
    p_sum = jnp.zeros((n_pad, tq), F32)
    for r in range(q_per_kv):
        sl = slice(r * HEAD_DIM, (r + 1) * HEAD_DIM)
        s = jnp.where(ok, _nt_dot(kc, q_ref[:, sl]) * scale, -jnp.inf)
        mx = jnp.max(s, axis=0, keepdims=True)
        mx = jnp.where(mx > -jnp.inf, mx, 0.0)
        e = jnp.where(ok, jnp.exp(s - mx), 0.0)
        den = jnp.maximum(jnp.sum(e, axis=0, keepdims=True), jnp.finfo(F32).tiny)
        o_t = jnp.dot(vct, e.astype(BF16), preferred_element_type=F32) / den
        o_ref[:, sl] = o_t.T.astype(BF16)
        p_sum = p_sum + e / den

    jj = lax.broadcasted_iota(jnp.int32, (n_slc, n_pad), 0) * NSA_SLC_LEN
    nn = lax.broadcasted_iota(jnp.int32, (n_slc, n_pad), 1) * NSA_CMP_STRIDE
    overlap = jnp.where((nn < jj + NSA_SLC_LEN) & (jj < nn + NSA_CMP_LEN)
                        & (nn < n_cmp * NSA_CMP_STRIDE), 1.0, 0.0).astype(BF16)
    p_hi = p_sum.astype(BF16)
    p_lo = (p_sum - p_hi.astype(F32)).astype(BF16)
    imp = (jnp.dot(overlap, p_hi, preferred_element_type=F32)
           + jnp.dot(overlap, p_lo, preferred_element_type=F32))

    cur = t // NSA_SLC_LEN
    jb = lax.broadcasted_iota(jnp.int32, (n_slc, tq), 0)
    val = jnp.where(jb == cur, jnp.inf,
                    jnp.where(jb == 0, jnp.inf, jnp.where(jb < cur, imp, -jnp.inf)))
    rank = _top_rank(val, jb)
    sel_ref[...] = jnp.where(rank < n_top, jnp.where(val > -jnp.inf, 1.0, 0.0), 0.0)


def nsa_cmp_select(q_cmp, k_cmp, v_cmp_t, batch, seq, tq=256):
    m, d = q_cmp.shape
    groups = NSA_KV_GROUPS
    q_per_kv = d // HEAD_DIM // groups
    gw = q_per_kv * HEAD_DIM
    tq = min(tq, seq)
    nq = seq // tq
    n_pad = k_cmp.shape[2]
    n_cmp = (seq - NSA_CMP_LEN) // NSA_CMP_STRIDE + 1
    n_slc = seq // NSA_SLC_LEN
    return pl.pallas_call(
        functools.partial(_cmp_select_body, n_cmp=n_cmp, n_top=min(NSA_SLC_TOPK, n_slc),
                          q_per_kv=q_per_kv),
        grid=(batch, groups, nq),
        in_specs=[pl.BlockSpec((tq, gw), lambda b, g, i: (b * nq + i, g)),
                  pl.BlockSpec((None, None, n_pad, HEAD_DIM), lambda b, g, i: (b, g, 0, 0)),
                  pl.BlockSpec((None, None, HEAD_DIM, n_pad), lambda b, g, i: (b, g, 0, 0))],
        out_specs=[pl.BlockSpec((tq, gw), lambda b, g, i: (b * nq + i, g)),
                   pl.BlockSpec((None, None, n_slc, tq), lambda b, g, i: (b, g, 0, i))],
        out_shape=[jax.ShapeDtypeStruct((m, d), BF16),
                   jax.ShapeDtypeStruct((batch, groups, n_slc, seq), F32)],
        compiler_params=_params(("parallel", "parallel", "parallel"), 32),
        name="nsa_cmp_select",
    )(q_cmp, k_cmp, v_cmp_t)


def _nsa_attn_body(q_ref, ks_ref, vs_ref, kw_ref, vw_ref, sel_ref, os_ref, ow_ref,
                   vst_ref, vwt_ref, bias_ref, s_ref, p_ref, acc_ref, *, q_per_kv, per):
    tq = q_ref.shape[0]
    kc = KV_CHUNK
    sub = kc // NSA_SLC_LEN
    step = pl.program_id(2)
    q0 = step * tq

    @pl.when(step == 0)
    def _():
        _store_transposed(vs_ref, vst_ref, kc)
        _store_transposed(vw_ref, vwt_ref, kc)

    n_chain = q_per_kv // per
    q_ts = [jnp.concatenate([q_ref[:, r * HEAD_DIM:(r + 1) * HEAD_DIM]
                             for r in range(c * per, (c + 1) * per)], axis=0).astype(F32).T.astype(BF16)
            for c in range(n_chain)]
    t = q0 + lax.broadcasted_iota(jnp.int32, (1, tq), 1)
    bias_ref[...] = jnp.where(sel_ref[...] > 0.5, 0.0, MASKED_LOGIT)

    def chain_bias(bias):
        return jnp.concatenate([bias] * per, axis=1)

    def selected_bias(c, kpos):
        picked = jnp.concatenate(
            [jnp.broadcast_to(bias_ref[pl.ds(c * sub + a, 1), :], (NSA_SLC_LEN, tq)) for a in range(sub)],
            axis=0)
        return chain_bias(jnp.where(kpos <= t, picked, MASKED_LOGIT))

    def window_bias(kpos):
        return chain_bias(jnp.where(kpos <= t, jnp.where(t - kpos < NSA_WINDOW, 0.0, MASKED_LOGIT),
                                    MASKED_LOGIT))

    def selected_logits(c):
        start = pl.multiple_of(c * kc, kc)
        kpos = start + lax.broadcasted_iota(jnp.int32, (kc, 1), 0)
        k, bias = ks_ref[pl.ds(start, kc), :], selected_bias(c, kpos)
        return [(k, q_t, bias) for q_t in q_ts]

    def window_logits(c):
        start = pl.multiple_of(c * kc, kc)
        kpos = start + lax.broadcasted_iota(jnp.int32, (kc, 1), 0)
        k, bias = kw_ref[pl.ds(start, kc), :], window_bias(kpos)
        return [(k, q_t, bias) for q_t in q_ts]

    bufs = [(s_ref.at[c], p_ref.at[c], acc_ref.at[c]) for c in range(2 * n_chain)]

    def selected_only(c, states):
        return _flash_stage(bufs[:n_chain], states, [vst_ref[jnp.maximum(c - 1, 0)]] * n_chain,
                            selected_logits(c + 1))

    def selected_and_window(c, states, is_last=False):
        prev = jnp.maximum(c - 1, 0)
        following = None if is_last else selected_logits(c + 1) + window_logits(c + 1)
        return _flash_stage(bufs, states, [vst_ref[prev]] * n_chain + [vwt_ref[prev]] * n_chain, following)

    first = jnp.maximum(q0 - (NSA_WINDOW - 1), 0) // kc
    last = (q0 + tq - 1) // kc
    states = tuple(_flash_start(buf, *f) for buf, f in zip(bufs[:n_chain], selected_logits(0)))
    states = lax.fori_loop(0, first, selected_only, states)
    states = states + tuple(_flash_start(buf, *f) for buf, f in zip(bufs[n_chain:], window_logits(first)))
    states = lax.fori_loop(first, last, selected_and_window, states)
    states = selected_and_window(last, states, is_last=True)

    for c, state in enumerate(states):
        out_ref, v_last = (os_ref, vst_ref[last]) if c < n_chain else (ow_ref, vwt_ref[last])
        out = _flash_finish(bufs[c], state, v_last)
        for i in range(per):
            r = (c % n_chain) * per + i
            out_ref[:, r * HEAD_DIM:(r + 1) * HEAD_DIM] = out[:, i * tq:(i + 1) * tq].T.astype(BF16)


def nsa_attention(q_rot, kv, sel, batch, seq, tq=256):
    m, d = q_rot.shape
    groups = NSA_KV_GROUPS
    q_per_kv = d // HEAD_DIM // groups
    gw = q_per_kv * HEAD_DIM
    tq = min(tq, seq)
    nq = seq // tq
    n_slc = sel.shape[2]
    assert seq % KV_CHUNK == 0 and KV_CHUNK % NSA_SLC_LEN == 0

    def kv_spec(part):
        return pl.BlockSpec((seq, HEAD_DIM), lambda b, g, i: (b, part * groups + g))

    q_spec = pl.BlockSpec((tq, gw), lambda b, g, i: (b * nq + i, g))
    out = jax.ShapeDtypeStruct((m, d), BF16)
    v_t = pltpu.VMEM((seq // KV_CHUNK, HEAD_DIM, KV_CHUNK), BF16)
    per = 1
    return pl.pallas_call(
        functools.partial(_nsa_attn_body, q_per_kv=q_per_kv, per=per),
        grid=(batch, groups, nq),
        in_specs=[q_spec, kv_spec(0), kv_spec(1), kv_spec(2), kv_spec(3),
                  pl.BlockSpec((None, None, n_slc, tq), lambda b, g, i: (b, g, 0, i))],
        out_specs=[q_spec, q_spec],
        out_shape=[out, out],
        scratch_shapes=[v_t, v_t, pltpu.VMEM((n_slc, tq), F32)]
        + _flash_buffers(2 * q_per_kv // per, KV_CHUNK, per * tq),
        compiler_params=_params(("parallel", "parallel", "arbitrary"), 32),
        name="nsa_attention",
    )(q_rot, kv, kv, kv, kv, sel)


def nsa_layer(h, gain, w_q, w_kv_raw, w_kv_rot, q_gain, k_gain, cmp_pos, cmp_w1, cmp_w2, w_gate, w_o,
              tables, batch, seq):
    m, d = h.shape
    n_heads = d // HEAD_DIM
    groups = NSA_KV_GROUPS
    gwk = groups * HEAD_DIM
    tn = 512
    tm = min(1024, m)
    tab = pl.BlockSpec((tm, HEAD_DIM), lambda i, j: (i, 0))
    col = pl.BlockSpec((tm, tn), lambda i, j: (i, j))
    one_w = [lambda i, j: (0, j)]

    def q_epilogue(accs, cols, extra, outs):
        _head_norm_rope_store(accs[0], cols, extra[0], extra[1:], outs[0], outs[1])

    q_shape = jax.ShapeDtypeStruct((m, d), BF16)
    q_cmp, q_rot = norm_matmul(
        h, gain, w_q, one_w, d // tn, tn,
        extra=[(jnp.tile(q_gain, n_heads)[None], pl.BlockSpec((1, tn), lambda i, j: (0, j)))]
        + [(t, tab) for t in tables],
        outs=[(q_shape, col), (q_shape, col)], variants=[(None, q_epilogue)], sub=2, tm=tm)

    def raw_epilogue(accs, cols, extra, outs):
        outs[0][:, cols] = accs[0]

    (raw,) = norm_matmul(
        h, gain, w_kv_raw, one_w, 2 * gwk // tn, tn, extra=[],
        outs=[(jax.ShapeDtypeStruct((m, 2 * gwk), F32), col)], variants=[(None, raw_epilogue)], tm=tm)

    assert tn == gwk
    kv_gain_row = jnp.concatenate([jnp.tile(k_gain[1], groups), jnp.ones((gwk,), F32),
                                   jnp.tile(k_gain[2], groups), jnp.ones((gwk,), F32)])[None]

    def k_epilogue(accs, cols, extra, outs):
        _head_norm_rope_store(accs[0], cols, extra[0], extra[1:], None, outs[0])

    def v_epilogue(accs, cols, extra, outs):
        outs[0][:, cols] = accs[0].astype(BF16)

    (kv,) = norm_matmul(
        h, gain, w_kv_rot, one_w, 4, tn,
        extra=[(kv_gain_row, pl.BlockSpec((1, tn), lambda i, j: (0, j)))] + [(t, tab) for t in tables],
        outs=[(jax.ShapeDtypeStruct((m, 4 * gwk), BF16), col)],
        variants=[(lambda j: j % 2 == 0, k_epilogue), (lambda j: j % 2 == 1, v_epilogue)], sub=2, tm=tm)

    def gate_epilogue(accs, cols, extra, outs):
        outs[0][:, cols] = jax.nn.sigmoid(accs[0])

    gate_w = w_gate.shape[1]
    (gates,) = norm_matmul(
        h, gain, w_gate, one_w, 1, gate_w, extra=[],
        outs=[(jax.ShapeDtypeStruct((m, gate_w), F32), pl.BlockSpec((tm, gate_w), lambda i, j: (i, j)))],
        variants=[(None, gate_epilogue)], tm=tm)

    k_cmp, v_cmp = nsa_compress(raw, cmp_pos, cmp_w1, cmp_w2, k_gain[0], batch, seq)
    o_cmp, sel = nsa_cmp_select(q_cmp, k_cmp, v_cmp, batch, seq)
    o_slc, o_win = nsa_attention(q_rot, kv, sel, batch, seq)

    def prologue(ins):
        g_ref, oc_ref, os_ref, ow_ref = ins
        parts = []
        for hh in range(n_heads):
            sl = slice(hh * HEAD_DIM, (hh + 1) * HEAD_DIM)
            mixed = (g_ref[:, 3 * hh:3 * hh + 1] * oc_ref[:, sl].astype(F32)
                     + g_ref[:, 3 * hh + 1:3 * hh + 2] * os_ref[:, sl].astype(F32)
                     + g_ref[:, 3 * hh + 2:3 * hh + 3] * ow_ref[:, sl].astype(F32))
            parts.append(mixed.astype(BF16))
        return jnp.concatenate(parts, axis=1)

    tmo = min(512, m)
    full = pl.BlockSpec((tmo, d), lambda i: (i, 0))
    return out_matmul([(gates, pl.BlockSpec((tmo, gate_w), lambda i: (i, 0))),
                       (o_cmp, full), (o_slc, full), (o_win, full)], prologue, w_o, h, tm=tmo)


def kernel(x, p, positions, mixer_norm, mlp_norm, mlp_w1, mlp_w2, ple_norm, ple_gate, ple_proj,
           moba_w_qkv, moba_q_gain, moba_k_gain, moba_w_o, pool_w, pool_scale,
           nsa_w_q, nsa_w_kv, nsa_q_gain, nsa_k_gain, nsa_cmp_pos, nsa_cmp_w1, nsa_cmp_w2,
           nsa_w_gate, nsa_w_o, conv_w_in, conv_w, conv_b, conv_w_o):
    batch, seq, d = x.shape
    depth = p.shape[0]
    m = batch * seq
    n_heads = d // HEAD_DIM
    gwk = NSA_KV_GROUPS * HEAD_DIM
    tables = rope_tables(positions)
    bf = lambda w: w.astype(BF16)
    gate_pad = (-3 * n_heads) % HEAD_DIM
    mlp_w1_bf, mlp_w2_bf = bf(mlp_w1), bf(mlp_w2)
    ple_gate_bf, ple_proj_bf = bf(ple_gate), bf(ple_proj)
    p_rows = p.reshape(depth, m, -1)

    h = x.reshape(m, d)
    for i in range(depth):
        kind, j = i % 4, i // 4
        if kind == 0:
            h = moba_layer(h, mixer_norm[i], bf(moba_w_qkv[j]), moba_q_gain[j], moba_k_gain[j],
                           bf(moba_w_o[j]), tables, batch, seq)
        elif kind == 1:
            h = pool_layer(h, mixer_norm[i], bf(pool_w[j]), pool_scale[j], seq)
        elif kind == 2:
            w_kv = nsa_w_kv[j]
            w_gate = jnp.pad(nsa_w_gate[j], ((0, 0), (0, gate_pad)))
            h = nsa_layer(h, mixer_norm[i], bf(nsa_w_q[j]), bf(w_kv[:, :2 * gwk]), bf(w_kv[:, 2 * gwk:]),
                          nsa_q_gain[j], nsa_k_gain[j], nsa_cmp_pos[j], bf(nsa_cmp_w1[j]),
                          bf(nsa_cmp_w2[j]), bf(w_gate), bf(nsa_w_o[j]), tables, batch, seq)
        else:
            h = conv_layer(h, mixer_norm[i], bf(conv_w_in[j]), conv_w[j], conv_b[j], bf(conv_w_o[j]), seq)
        h = mlp_layer(h, mlp_norm[i], mlp_w1_bf, mlp_w2_bf, i)
        h = ple_layer(h, ple_norm[i], ple_gate_bf, p_rows, ple_proj_bf, i)
    return h.reshape(batch, seq, d)
```

```python
import functools

import jax
import jax.numpy as jnp
from jax import lax
from jax.experimental import pallas as pl
from jax.experimental.pallas import tpu as pltpu

F32 = jnp.float32
BF16 = jnp.bfloat16

HEAD_DIM = 128
ROT_DIM = HEAD_DIM // 4
ROPE_THETA = 500000.0
NORM_EPS = 1e-6
MOBA_BLOCK = 256
MOBA_TOPK = 3
POOL_WINDOWS = (2, 4, 8, 16)
NSA_KV_GROUPS = 4
NSA_CMP_LEN = 32
NSA_CMP_STRIDE = 16
NSA_SLC_LEN = 64
NSA_SLC_TOPK = 16
NSA_WINDOW = 512
CONV_WIDTH = 3

MASKED_LOGIT = -1e30
SOFTMAX_EXP2_SCALE = HEAD_DIM ** -0.5 * 1.4426950408889634
KV_CHUNK = 256
MIB = 1024 * 1024


def _params(semantics, vmem_mib):
    return pltpu.CompilerParams(dimension_semantics=semantics,
                                vmem_limit_bytes=vmem_mib * MIB)


def _resident(shape, index_map):
    return pl.BlockSpec(shape, index_map, pipeline_mode=pl.Buffered(1))


def _rms(x, gain):
    ms = jnp.mean(x * x, axis=-1, keepdims=True)
    return x * lax.rsqrt(ms + NORM_EPS) * gain


def _rope(x, cos, sin_lo, sin_hi):
    half = ROT_DIM // 2
    return (x * cos + pltpu.roll(x, HEAD_DIM - half, 1) * sin_lo
            + pltpu.roll(x, half, 1) * sin_hi)


def _nt_dot(a, b):
    return lax.dot_general(a, b, (((1,), (1,)), ((), ())), preferred_element_type=F32)


def _rope_table_body(pos_ref, freq_ref, lo_ref, hi_ref, cos_ref, slo_ref, shi_ref):
    ang = pos_ref[...].astype(F32) * freq_ref[...]
    s = jnp.sin(ang)
    cos_ref[...] = jnp.cos(ang)
    slo_ref[...] = s * lo_ref[...]
    shi_ref[...] = s * hi_ref[...]


def rope_tables(positions):
    m = positions.size
    half = ROT_DIM // 2
    freqs = jnp.float32(ROPE_THETA) ** (-jnp.arange(half, dtype=F32) * 2.0 / ROT_DIM)
    zeros = jnp.zeros((HEAD_DIM - ROT_DIM,), F32)
    freq_row = jnp.concatenate([freqs, freqs, zeros])[None]
    lo_row = jnp.concatenate([-jnp.ones((half,), F32), jnp.zeros((half,), F32), zeros])[None]
    hi_row = jnp.concatenate([jnp.zeros((half,), F32), jnp.ones((half,), F32), zeros])[None]
    tm = min(m, 1024)
    row = pl.BlockSpec((1, HEAD_DIM), lambda i: (0, 0))
    tab = pl.BlockSpec((tm, HEAD_DIM), lambda i: (i, 0))
    return pl.pallas_call(
        _rope_table_body,
        grid=(m // tm,),
        in_specs=[pl.BlockSpec((tm, 1), lambda i: (i, 0)), row, row, row],
        out_specs=[tab, tab, tab],
        out_shape=[jax.ShapeDtypeStruct((m, HEAD_DIM), F32)] * 3,
        compiler_params=_params(("parallel",), 32),
        name="rope_tables",
    )(positions.reshape(m, 1), freq_row, lo_row, hi_row)


def _nmm_body(*refs, n_w, n_extra, n_out, sub, variants, on_row_start):
    x_ref, g_ref = refs[0], refs[1]
    w_refs = refs[2:2 + n_w]
    extra = refs[2 + n_w:2 + n_w + n_extra]
    outs = refs[2 + n_w + n_extra:2 + n_w + n_extra + n_out]
    xn_ref = refs[2 + n_w + n_extra + n_out]
    extra = extra + refs[3 + n_w + n_extra + n_out:]
    j = pl.program_id(1)

    @pl.when(j == 0)
    def _():
        xn_ref[...] = _rms(x_ref[...], g_ref[...]).astype(BF16)
        if on_row_start is not None:
            on_row_start(extra)

    width = w_refs[0].shape[1] // sub
    cols = [slice(s * width, (s + 1) * width) for s in range(sub)]

    def run(epilogue):
        xn = xn_ref[...]
        accs = [[jnp.dot(xn, w[:, c], preferred_element_type=F32) for w in w_refs] for c in cols]
        for c, acc in zip(cols, accs):
            epilogue(acc, c, extra, outs)

    if len(variants) == 1:
        run(variants[0][1])
    else:
        for applies, epilogue in variants:
            pl.when(applies(j))(functools.partial(run, epilogue))


def norm_matmul(h, gain, w, w_maps, n_col_tiles, tn, extra, outs, variants, sub=1, tm=1024, vmem_mib=56,
                scratch=(), on_row_start=None):
    m, d = h.shape
    tm = min(tm, m)
    in_specs = [pl.BlockSpec((tm, d), lambda i, j: (i, 0)),
                pl.BlockSpec((1, d), lambda i, j: (0, 0))]
    in_specs += [pl.BlockSpec((d, tn), wm) for wm in w_maps]
    in_specs += [spec for _, spec in extra]
    return pl.pallas_call(
        functools.partial(_nmm_body, n_w=len(w_maps), n_extra=len(extra), n_out=len(outs), sub=sub,
                          variants=variants, on_row_start=on_row_start),
        grid=(m // tm, n_col_tiles),
        in_specs=in_specs,
        out_specs=[spec for _, spec in outs],
        out_shape=[shape for shape, _ in outs],
        scratch_shapes=[pltpu.VMEM((tm, d), BF16)] + list(scratch),
        compiler_params=_params(("arbitrary" if scratch else "parallel", "arbitrary"), vmem_mib),
        name="norm_matmul",
    )(h, gain.reshape(1, d), *([w] * len(w_maps)), *[a for a, _ in extra])


def _head_norm_rope_store(acc, cols, gain_ref, table_refs, out_plain, out_rot):
    cos, slo, shi = (t[...] for t in table_refs)
    width = acc.shape[1]
    same_head = (lax.broadcasted_iota(jnp.int32, (width, width), 0) // HEAD_DIM
                 == lax.broadcasted_iota(jnp.int32, (width, width), 1) // HEAD_DIM)
    ones = jnp.where(same_head, 1.0, 0.0).astype(BF16)
    sq = acc * acc
    sq_hi = sq.astype(BF16)
    sq_lo = (sq - sq_hi.astype(F32)).astype(BF16)
    ssq = (jnp.dot(sq_hi, ones, preferred_element_type=F32)
           + jnp.dot(sq_lo, ones, preferred_element_type=F32))
    normed = acc * lax.rsqrt(ssq * (1.0 / HEAD_DIM) + NORM_EPS) * gain_ref[:, cols]
    for hh in range(width // HEAD_DIM):
        sl = slice(hh * HEAD_DIM, (hh + 1) * HEAD_DIM)
        dst = slice(cols.start + hh * HEAD_DIM, cols.start + (hh + 1) * HEAD_DIM)
        xh = normed[:, sl]
        if out_plain is not None:
            out_plain[:, dst] = xh.astype(BF16)
        if out_rot is not None:
            out_rot[:, dst] = _rope(xh, cos, slo, shi).astype(BF16)


def _omm_body(*refs, n_in, prologue):
    ins = refs[:n_in]
    w_ref, res_ref, o_ref = refs[n_in:n_in + 3]
    a = prologue(ins)
    o_ref[...] = res_ref[...] + jnp.dot(a, w_ref[...], preferred_element_type=F32)


def out_matmul(ins, prologue, w, res, tm=512, vmem_mib=48):
    m, d = res.shape
    tm = min(tm, m)
    k = w.shape[0]
    return pl.pallas_call(
        functools.partial(_omm_body, n_in=len(ins), prologue=prologue),
        grid=(m // tm,),
        in_specs=[spec for _, spec in ins] + [
            _resident((k, d), lambda i: (0, 0)),
            pl.BlockSpec((tm, d), lambda i: (i, 0))],
        out_specs=pl.BlockSpec((tm, d), lambda i: (i, 0)),
        out_shape=jax.ShapeDtypeStruct((m, d), F32),
        compiler_params=_params(("parallel",), vmem_mib),
        name="out_matmul",
    )(*[a for a, _ in ins], w, res)


def _mlp_body(x_ref, g_ref, w1_ref, w2_ref, o_ref, xn_ref):
    @pl.when(pl.program_id(1) == 0)
    def _():
        x = x_ref[...]
        xn_ref[...] = _rms(x, g_ref[...]).astype(BF16)
        o_ref[...] = x

    a = jnp.dot(xn_ref[...], w1_ref[...], preferred_element_type=F32)
    a = jnp.square(jnp.maximum(a, 0.0)).astype(BF16)
    o_ref[...] += jnp.dot(a, w2_ref[...], preferred_element_type=F32)


def mlp_layer(h, gain, w1, w2, layer, tm=1024, tf=512):
    m, d = h.shape
    dff = w1.shape[2]
    tm = min(tm, m)
    return pl.pallas_call(
        _mlp_body,
        grid=(m // tm, dff // tf),
        in_specs=[pl.BlockSpec((tm, d), lambda i, f: (i, 0)),
                  pl.BlockSpec((1, d), lambda i, f: (0, 0)),
                  pl.BlockSpec((None, d, tf), lambda i, f: (layer, 0, f)),
                  pl.BlockSpec((None, tf, d), lambda i, f: (layer, f, 0))],
        out_specs=pl.BlockSpec((tm, d), lambda i, f: (i, 0)),
        out_shape=jax.ShapeDtypeStruct((m, d), F32),
        scratch_shapes=[pltpu.VMEM((tm, d), BF16)],
        compiler_params=_params(("parallel", "arbitrary"), 56),
        name="mlp",
    )(h, gain.reshape(1, d), w1, w2)


def _ple_body(x_ref, g_ref, wg_ref, p_ref, wp_ref, o_ref):
    x = x_ref[...]
    xn = _rms(x, g_ref[...]).astype(BF16)
    gate = jax.nn.sigmoid(jnp.dot(xn, wg_ref[...], preferred_element_type=F32))
    emb = jnp.dot(p_ref[...].astype(BF16), wp_ref[...], preferred_element_type=F32)
    o_ref[...] = x + gate * emb


def ple_layer(h, gain, wg, p, wp, layer, tm=512):
    m, d = h.shape
    pd = p.shape[2]
    tm = min(tm, m)
    return pl.pallas_call(
        _ple_body,
        grid=(m // tm,),
        in_specs=[pl.BlockSpec((tm, d), lambda i: (i, 0)),
                  pl.BlockSpec((1, d), lambda i: (0, 0)),
                  _resident((None, d, d), lambda i: (layer, 0, 0)),
                  pl.BlockSpec((None, tm, pd), lambda i: (layer, i, 0)),
                  _resident((None, pd, d), lambda i: (layer, 0, 0))],
        out_specs=pl.BlockSpec((tm, d), lambda i: (i, 0)),
        out_shape=jax.ShapeDtypeStruct((m, d), F32),
        compiler_params=_params(("parallel",), 48),
        name="ple",
    )(h, gain.reshape(1, d), wg, p, wp)


def _flash_buffers(chains, keys, queries):
    return [pltpu.VMEM((chains, keys, queries), F32), pltpu.VMEM((chains, keys, queries), BF16),
            pltpu.VMEM((chains, HEAD_DIM, queries), F32)]


def _flash_logits(buf, k, q_t, bias):
    s = jnp.dot(k, q_t, preferred_element_type=F32) + bias
    buf[0][...] = s
    return jnp.max(s, axis=0, keepdims=True)


def _flash_start(buf, k, q_t, bias):
    s_ref, p_ref, acc_ref = buf
    p_ref[...] = jnp.zeros_like(p_ref)
    acc_ref[...] = jnp.zeros_like(acc_ref)
    queries = s_ref.shape[1]
    return (_flash_logits(buf, k, q_t, bias), jnp.ones((1, queries), F32),
            jnp.full((1, queries), MASKED_LOGIT, F32), jnp.zeros((1, queries), F32))


def _flash_stage(bufs, states, v_prev, following):
    products = [jnp.dot(v, buf[1][...], preferred_element_type=F32) for v, buf in zip(v_prev, bufs)]
    if following is not None:
        upcoming = [jnp.dot(k, q_t, preferred_element_type=F32) for k, q_t, _ in following]
    out = []
    for c, ((s_ref, p_ref, acc_ref), (s_max, alpha_prev, m, l), pv) in enumerate(zip(bufs, states, products)):
        m_new = jnp.maximum(m, s_max)
        p = jnp.exp2((s_ref[...] - m_new) * SOFTMAX_EXP2_SCALE)
        alpha = jnp.exp2((m - m_new) * SOFTMAX_EXP2_SCALE)
        l = alpha * l + jnp.sum(p, axis=0, keepdims=True)
        p_ref[...] = p.astype(BF16)
        acc_ref[...] = alpha_prev * acc_ref[...] + pv
        if following is not None:
            s_next = upcoming[c] + following[c][2]
            s_ref[...] = s_next
            s_max = jnp.max(s_next, axis=0, keepdims=True)
        out.append((s_max, alpha, m_new, l))
    return tuple(out)


def _flash_finish(buf, state, v_last):
    _, p_ref, acc_ref = buf
    _, alpha, _, l = state
    return (alpha * acc_ref[...] + jnp.dot(v_last, p_ref[...], preferred_element_type=F32)) / l


def _store_transposed(v_ref, vt_ref, chunk):
    for n in range(vt_ref.shape[0]):
        vt_ref[n] = v_ref[n * chunk:(n + 1) * chunk, :].astype(F32).T.astype(BF16)


def _top_rank(v, row):
    n = v.shape[0]
    if n % 8:
        rank = jnp.zeros(v.shape, jnp.int32)
        for j in range(n):
            r = v[j:j + 1, :]
            rank = rank + jnp.where(row > j, jnp.where(r >= v, 1, 0), jnp.where(r > v, 1, 0))
        return rank
    starts = range(0, n, 8)
    groups = [v[lo:lo + 8, :] for lo in starts]
    ranks = [jnp.zeros((8, v.shape[1]), jnp.int32) for _ in starts]
    for j in range(n):
        r = jnp.broadcast_to(v[j:j + 1, :], (8, v.shape[1]))
        for g, lo in enumerate(starts):
            if lo > j:
                ranks[g] = ranks[g] + jnp.where(r >= groups[g], 1, 0)
            elif lo + 7 <= j:
                ranks[g] = ranks[g] + jnp.where(r > groups[g], 1, 0)
            else:
                below = lax.broadcasted_iota(jnp.int32, (8, v.shape[1]), 0) > j - lo
                ranks[g] = ranks[g] + jnp.where(below, jnp.where(r >= groups[g], 1, 0),
                                                jnp.where(r > groups[g], 1, 0))
    return jnp.concatenate(ranks, axis=0)


def _moba_body(q_ref, k_ref, v_ref, o_ref, kmean_ref, vt_ref, bias_ref, s_ref, p_ref, acc_ref,
               *, n_blocks, heads):
    blk = MOBA_BLOCK
    own = pl.program_id(2)
    cols = [slice(e * HEAD_DIM, (e + 1) * HEAD_DIM) for e in range(heads)]
    bufs = [(s_ref.at[e], p_ref.at[e], acc_ref.at[e]) for e in range(heads)]

    @pl.when(own == 0)
    def _():
        for e in range(heads):
            for n in range(n_blocks):
                kb = k_ref[n * blk:(n + 1) * blk, cols[e]].astype(F32)
                kmean_ref[e, n:n + 1, :] = jnp.mean(kb, axis=0, keepdims=True)
                vt_ref[e, n] = v_ref[n * blk:(n + 1) * blk, cols[e]].astype(F32).T.astype(BF16)

    q_ts = [q_ref[:, cols[e]].astype(F32).T.astype(BF16) for e in range(heads)]
    for e in range(heads):
        gate = jnp.dot(kmean_ref[e].astype(BF16), q_ts[e], preferred_element_type=F32)
        row = lax.broadcasted_iota(jnp.int32, gate.shape, 0)
        gate = jnp.where(row < own, gate, -jnp.inf)
        rank = _top_rank(gate, row)
        bias_ref[e] = jnp.where(row < own, jnp.where(rank < MOBA_TOPK, 0.0, MASKED_LOGIT), MASKED_LOGIT)

    def keys(j):
        start = pl.multiple_of(j * blk, blk)
        return [k_ref[pl.ds(start, blk), cols[e]] for e in range(heads)]

    def values(j):
        return [vt_ref[e, j] for e in range(heads)]

    def stage(j, states, next_biases):
        following = None if next_biases is None else list(zip(keys(j + 1), q_ts, next_biases))
        return _flash_stage(bufs, states, values(jnp.maximum(j - 1, 0)), following)

    def chosen_bias(j):
        return [bias_ref[e, pl.ds(j, 1), :] for e in range(heads)]

    causal = jnp.where(lax.broadcasted_iota(jnp.int32, (blk, blk), 0)
                       <= lax.broadcasted_iota(jnp.int32, (blk, blk), 1), 0.0, MASKED_LOGIT)
    first_bias = [jnp.where(own == 0, causal, b) for b in chosen_bias(0)]
    states = tuple(_flash_start(buf, k, q_t, b) for buf, k, q_t, b in zip(bufs, keys(0), q_ts, first_bias))
    states = lax.fori_loop(0, own - 1, lambda j, st: stage(j, st, chosen_bias(j + 1)), states)
    states = lax.cond(own > 0, lambda st: stage(own - 1, st, [causal] * heads), lambda st: st, states)
    states = stage(own, states, None)
    for e, v_last in enumerate(values(own)):
        o_ref[:, cols[e]] = _flash_finish(bufs[e], states[e], v_last).T.astype(BF16)


def moba_attention(qkv, batch, seq, n_heads, heads_per_step=4):
    blk = MOBA_BLOCK
    nq = seq // blk
    hp = heads_per_step
    width = hp * HEAD_DIM
    groups = n_heads // hp
    return pl.pallas_call(
        functools.partial(_moba_body, n_blocks=nq, heads=hp),
        grid=(batch, groups, nq),
        in_specs=[pl.BlockSpec((blk, width), lambda b, h, i: (b * nq + i, h)),
                  pl.BlockSpec((seq, width), lambda b, h, i: (b, groups + h)),
                  pl.BlockSpec((seq, width), lambda b, h, i: (b, 2 * groups + h))],
        out_specs=pl.BlockSpec((blk, width), lambda b, h, i: (b * nq + i, h)),
        out_shape=jax.ShapeDtypeStruct((batch * seq, n_heads * HEAD_DIM), BF16),
        scratch_shapes=[pltpu.VMEM((hp, nq, HEAD_DIM), F32),
                        pltpu.VMEM((hp, nq, HEAD_DIM, blk), BF16),
                        pltpu.VMEM((hp, nq, blk), F32)] + _flash_buffers(hp, blk, blk),
        compiler_params=_params(("parallel", "parallel", "arbitrary"), 32),
        name="moba_attention",
    )(qkv, qkv, qkv)


def moba_layer(h, gain, w_qkv, q_gain, k_gain, w_o, tables, batch, seq):
    m, d = h.shape
    n_heads = d // HEAD_DIM
    tn = 1024
    n_qk_tiles = 2 * d // tn
    gain_row = jnp.concatenate([jnp.tile(q_gain, n_heads), jnp.tile(k_gain, n_heads),
                                jnp.ones((d,), F32)])[None]

    def qk_epilogue(accs, cols, extra, outs):
        _head_norm_rope_store(accs[0], cols, extra[0], extra[1:], None, outs[0])

    def v_epilogue(accs, cols, extra, outs):
        outs[0][:, cols] = accs[0].astype(BF16)

    tm = min(1024, m)
    tab = pl.BlockSpec((tm, HEAD_DIM), lambda i, j: (i, 0))
    (qkv,) = norm_matmul(
        h, gain, w_qkv, [lambda i, j: (0, j)], 3 * d // tn, tn,
        extra=[(gain_row, pl.BlockSpec((1, tn), lambda i, j: (0, j)))] + [(t, tab) for t in tables],
        outs=[(jax.ShapeDtypeStruct((m, 3 * d), BF16), pl.BlockSpec((tm, tn), lambda i, j: (i, j)))],
        variants=[(lambda j: j < n_qk_tiles, qk_epilogue), (lambda j: j >= n_qk_tiles, v_epilogue)],
        sub=4, tm=tm)
    o = moba_attention(qkv, batch, seq, n_heads)
    tmo = min(512, m)
    return out_matmul([(o, pl.BlockSpec((tmo, d), lambda i: (i, 0)))],
                      lambda ins: ins[0][...], w_o, h, tm=tmo)


def _pool_body(x_ref, halo_ref, g_ref, w_ref, s_ref, o_ref, *, tiles_per_seq, halo):
    i = pl.program_id(0)
    tm, d = x_ref.shape
    group = d // len(POOL_WINDOWS)
    x = x_ref[...]
    gain = g_ref[...]
    xn = _rms(x, gain)
    prev = jnp.where(i % tiles_per_seq == 0, 0.0, _rms(halo_ref[...], gain))
    pos = (i % tiles_per_seq) * tm + lax.broadcasted_iota(jnp.int32, (tm, 1), 0)
    for g, win in enumerate(POOL_WINDOWS):
        sl = slice(g * group, (g + 1) * group)
        run = jnp.concatenate([prev[:, sl], xn[:, sl]], axis=0)
        span = 1
        while span < win:
            run = run + pltpu.roll(run, span, 0)
            span *= 2
        cnt = jnp.minimum(pos + 1, win).astype(F32)
        mean = run[halo:, :] / cnt
        mix = jnp.dot((mean - xn[:, sl]).astype(BF16), w_ref[g], preferred_element_type=F32)
        o_ref[:, sl] = x[:, sl] + mix * s_ref[:, sl]


def pool_layer(h, gain, w_groups, scale, seq, tm=512):
    m, d = h.shape
    halo = 16
    assert max(POOL_WINDOWS) <= halo
    tm = min(tm, seq)
    group = d // len(POOL_WINDOWS)
    return pl.pallas_call(
        functools.partial(_pool_body, tiles_per_seq=seq // tm, halo=halo),
        grid=(m // tm,),
        in_specs=[pl.BlockSpec((tm, d), lambda i: (i, 0)),
                  pl.BlockSpec((halo, d), lambda i: (jnp.maximum(i * (tm // halo) - 1, 0), 0)),
                  pl.BlockSpec((1, d), lambda i: (0, 0)),
                  _resident((len(POOL_WINDOWS), group, group), lambda i: (0, 0, 0)),
                  pl.BlockSpec((1, d), lambda i: (0, 0))],
        out_specs=pl.BlockSpec((tm, d), lambda i: (i, 0)),
        out_shape=jax.ShapeDtypeStruct((m, d), F32),
        compiler_params=_params(("parallel",), 48),
        name="pool_mixer",
    )(h, h, gain.reshape(1, d), w_groups, scale.reshape(1, d))


def conv_layer(h, gain, w_in, conv_w, conv_b, w_o, seq):
    m, d = h.shape
    tn = 512
    nj = d // tn

    tm = min(1024, seq)
    tiles_per_seq = seq // tm
    halo = 8

    def on_row_start(extra):
        @pl.when(pl.program_id(0) % tiles_per_seq == 0)
        def _():
            extra[2][...] = jnp.zeros_like(extra[2])

    def epilogue(accs, cols, extra, outs):
        cw_ref, cb_ref, carry_ref = extra
        j = pl.program_id(1)
        u0 = accs[1] * accs[2]
        prev = carry_ref[j, :, cols]
        row = lax.broadcasted_iota(jnp.int32, u0.shape, 0)
        u1 = jnp.where(row == 0, prev[halo - 1:halo, :], pltpu.roll(u0, 1, 0))
        u2 = jnp.where(row == 0, prev[halo - 2:halo - 1, :],
                       jnp.where(row == 1, prev[halo - 1:halo, :], pltpu.roll(u0, 2, 0)))
        conv = (cw_ref[0:1, cols] * u2 + cw_ref[1:2, cols] * u1 + cw_ref[2:3, cols] * u0
                + cb_ref[:, cols])
        outs[0][:, cols] = (accs[0] * conv).astype(BF16)
        carry_ref[j, :, cols] = u0[tm - halo:, :]

    (y,) = norm_matmul(
        h, gain, w_in, [lambda i, j: (0, j), lambda i, j: (0, j + nj), lambda i, j: (0, j + 2 * nj)],
        nj, tn,
        extra=[(conv_w, pl.BlockSpec((CONV_WIDTH, tn), lambda i, j: (0, j))),
               (conv_b.reshape(1, d), pl.BlockSpec((1, tn), lambda i, j: (0, j)))],
        outs=[(jax.ShapeDtypeStruct((m, d), BF16), pl.BlockSpec((tm, tn), lambda i, j: (i, j)))],
        variants=[(None, epilogue)], sub=2, tm=tm,
        scratch=[pltpu.VMEM((nj, halo, tn), F32)], on_row_start=on_row_start)

    tmo = min(512, m)
    return out_matmul([(y, pl.BlockSpec((tmo, d), lambda i: (i, 0)))],
                      lambda ins: ins[0][...], w_o, h, tm=tmo)


def _compress_body(x_ref, pos_ref, w1_ref, w2_ref, g_ref, kc_ref, vc_ref, lo_ref, hi_ref, *, groups):
    step = pl.program_id(1)
    stride = NSA_CMP_STRIDE

    @pl.when(step == 0)
    def _():
        lo_ref[...] = jnp.zeros_like(lo_ref)
        hi_ref[...] = jnp.zeros_like(hi_ref)

    for kv in range(2):
        p_lo = pos_ref[kv, pl.ds(step, 1), :]
        p_hi = pos_ref[kv, pl.ds(stride + step, 1), :]
        w_lo = w1_ref[kv, step]
        w_hi = w1_ref[kv, stride + step]
        for g in range(groups):
            c = kv * groups + g
            t = x_ref[:, c * HEAD_DIM:(c + 1) * HEAD_DIM]
            lo_ref[c] += jnp.dot((t + p_lo).astype(BF16), w_lo, preferred_element_type=F32)
            hi_ref[c] += jnp.dot((t + p_hi).astype(BF16), w_hi, preferred_element_type=F32)

    @pl.when(step == stride - 1)
    def _():
        n_half = lo_ref.shape[1]
        for kv in range(2):
            for g in range(groups):
                c = kv * groups + g
                pre = lo_ref[c] + pltpu.roll(hi_ref[c], n_half - 1, 0)
                hid = jax.nn.gelu(pre)
                if kv == 0:
                    out = jnp.dot(hid.astype(BF16), w2_ref[0], preferred_element_type=F32)
                    kc_ref[g] = _rms(out, g_ref[...]).astype(BF16)
                else:
                    vc_ref[g] = jnp.dot(w2_ref[1], hid.T.astype(BF16),
                                        preferred_element_type=F32).astype(BF16)


def nsa_compress(raw, cmp_pos, cmp_w1, cmp_w2, k_gain0, batch, seq):
    assert NSA_CMP_LEN == 2 * NSA_CMP_STRIDE
    stride = NSA_CMP_STRIDE
    groups = NSA_KV_GROUPS
    width = raw.shape[1]
    n_half = seq // stride
    x = raw.reshape(batch * n_half, stride * width)
    w1 = cmp_w1.reshape(2, NSA_CMP_LEN, HEAD_DIM, HEAD_DIM)
    cmp_w2 = jnp.stack([cmp_w2[0], cmp_w2[1].T])
    k_out = jax.ShapeDtypeStruct((batch, groups, n_half, HEAD_DIM), BF16)
    v_out = jax.ShapeDtypeStruct((batch, groups, HEAD_DIM, n_half), BF16)
    k_spec = pl.BlockSpec((None, groups, n_half, HEAD_DIM), lambda b, s: (b, 0, 0, 0))
    v_spec = pl.BlockSpec((None, groups, HEAD_DIM, n_half), lambda b, s: (b, 0, 0, 0))
    return pl.pallas_call(
        functools.partial(_compress_body, groups=groups),
        grid=(batch, stride),
        in_specs=[pl.BlockSpec((n_half, width), lambda b, s: (b, s)),
                  pl.BlockSpec((2, NSA_CMP_LEN, HEAD_DIM), lambda b, s: (0, 0, 0)),
                  pl.BlockSpec((2, NSA_CMP_LEN, HEAD_DIM, HEAD_DIM), lambda b, s: (0, 0, 0, 0)),
                  pl.BlockSpec((2, HEAD_DIM, HEAD_DIM), lambda b, s: (0, 0, 0)),
                  pl.BlockSpec((1, HEAD_DIM), lambda b, s: (0, 0))],
        out_specs=[k_spec, v_spec],
        out_shape=[k_out, v_out],
        scratch_shapes=[pltpu.VMEM((2 * groups, n_half, HEAD_DIM), F32)] * 2,
        compiler_params=_params(("parallel", "arbitrary"), 32),
        name="nsa_compress",
    )(x, cmp_pos, w1, cmp_w2, k_gain0.reshape(1, HEAD_DIM))


def _cmp_select_body(q_ref, kc_ref, vct_ref, o_ref, sel_ref, *, n_cmp, n_top, q_per_kv):
    tq = q_ref.shape[0]
    n_pad = kc_ref.shape[0]
    n_slc = sel_ref.shape[0]
    scale = HEAD_DIM ** -0.5
    t = pl.program_id(2) * tq + lax.broadcasted_iota(jnp.int32, (1, tq), 1)
    n = lax.broadcasted_iota(jnp.int32, (n_pad, 1), 0)
    ok = (n < n_cmp) & (n * NSA_CMP_STRIDE + (NSA_CMP_LEN - 1) <= t)
    kc = kc_ref[...]
    vct = vct_ref[...]
    p_sum = jnp.zeros((n_pad, tq), F32)
    for r in range(q_per_kv):
        sl = slice(r * HEAD_DIM, (r + 1) * HEAD_DIM)
        s = jnp.where(ok, _nt_dot(kc, q_ref[:, sl]) * scale, -jnp.inf)
        mx = jnp.max(s, axis=0, keepdims=True)
        mx = jnp.where(mx > -jnp.inf, mx, 0.0)
        e = jnp.where(ok, jnp.exp(s - mx), 0.0)
        den = jnp.maximum(jnp.sum(e, axis=0, keepdims=True), jnp.finfo(F32).tiny)
        o_t = jnp.dot(vct, e.astype(BF16), preferred_element_type=F32) / den
        o_ref[:, sl] = o_t.T.astype(BF16)
        p_sum = p_sum + e / den

    jj = lax.broadcasted_iota(jnp.int32, (n_slc, n_pad), 0) * NSA_SLC_LEN
    nn = lax.broadcasted_iota(jnp.int32, (n_slc, n_pad), 1) * NSA_CMP_STRIDE
    overlap = jnp.where((nn < jj + NSA_SLC_LEN) & (jj < nn + NSA_CMP_LEN)
                        & (nn < n_cmp * NSA_CMP_STRIDE), 1.0, 0.0).astype(BF16)
    p_hi = p_sum.astype(BF16)
    p_lo = (p_sum - p_hi.astype(F32)).astype(BF16)
    imp = (jnp.dot(overlap, p_hi, preferred_element_type=F32)
           + jnp.dot(overlap, p_lo, preferred_element_type=F32))

    cur = t // NSA_SLC_LEN
    jb = lax.broadcasted_iota(jnp.int32, (n_slc, tq), 0)
    val = jnp.where(jb == cur, jnp.inf,
                    jnp.where(jb == 0, jnp.inf, jnp.where(jb < cur, imp, -jnp.inf)))
    rank = _top_rank(val, jb)
    sel_ref[...] = jnp.where(rank < n_top, jnp.where(val > -jnp.inf, 1.0, 0.0), 0.0)


def nsa_cmp_select(q_cmp, k_cmp, v_cmp_t, batch, seq, tq=256):
    m, d = q_cmp.shape
    groups = NSA_KV_GROUPS
    q_per_kv = d // HEAD_DIM // groups
    gw = q_per_kv * HEAD_DIM
    tq = min(tq, seq)
    nq = seq // tq
    n_pad = k_cmp.shape[2]
    n_cmp = (seq - NSA_CMP_LEN) // NSA_CMP_STRIDE + 1
    n_slc = seq // NSA_SLC_LEN
    return pl.pallas_call(
        functools.partial(_cmp_select_body, n_cmp=n_cmp, n_top=min(NSA_SLC_TOPK, n_slc),
                          q_per_kv=q_per_kv),
        grid=(batch, groups, nq),
        in_specs=[pl.BlockSpec((tq, gw), lambda b, g, i: (b * nq + i, g)),
                  pl.BlockSpec((None, None, n_pad, HEAD_DIM), lambda b, g, i: (b, g, 0, 0)),
                  pl.BlockSpec((None, None, HEAD_DIM, n_pad), lambda b, g, i: (b, g, 0, 0))],
        out_specs=[pl.BlockSpec((tq, gw), lambda b, g, i: (b * nq + i, g)),
                   pl.BlockSpec((None, None, n_slc, tq), lambda b, g, i: (b, g, 0, i))],
        out_shape=[jax.ShapeDtypeStruct((m, d), BF16),
                   jax.ShapeDtypeStruct((batch, groups, n_slc, seq), F32)],
        compiler_params=_params(("parallel", "parallel", "parallel"), 32),
        name="nsa_cmp_select",
    )(q_cmp, k_cmp, v_cmp_t)


def _nsa_attn_body(q_ref, ks_ref, vs_ref, kw_ref, vw_ref, sel_ref, os_ref, ow_ref,
                   vst_ref, vwt_ref, bias_ref, s_ref, p_ref, acc_ref, *, q_per_kv, per):
    tq = q_ref.shape[0]
    kc = KV_CHUNK
    sub = kc // NSA_SLC_LEN
    step = pl.program_id(2)
    q0 = step * tq

    @pl.when(step == 0)
    def _():
        _store_transposed(vs_ref, vst_ref, kc)
        _store_transposed(vw_ref, vwt_ref, kc)

    n_chain = q_per_kv // per
    q_ts = [jnp.concatenate([q_ref[:, r * HEAD_DIM:(r + 1) * HEAD_DIM]
                             for r in range(c * per, (c + 1) * per)], axis=0).astype(F32).T.astype(BF16)
            for c in range(n_chain)]
    t = q0 + lax.broadcasted_iota(jnp.int32, (1, tq), 1)
    bias_ref[...] = jnp.where(sel_ref[...] > 0.5, 0.0, MASKED_LOGIT)

    def chain_bias(bias):
        return jnp.concatenate([bias] * per, axis=1)

    def selected_bias(c, kpos):
        picked = jnp.concatenate(
            [jnp.broadcast_to(bias_ref[pl.ds(c * sub + a, 1), :], (NSA_SLC_LEN, tq)) for a in range(sub)],
            axis=0)
        return chain_bias(jnp.where(kpos <= t, picked, MASKED_LOGIT))

    def window_bias(kpos):
        return chain_bias(jnp.where(kpos <= t, jnp.where(t - kpos < NSA_WINDOW, 0.0, MASKED_LOGIT),
                                    MASKED_LOGIT))

    def selected_logits(c):
        start = pl.multiple_of(c * kc, kc)
        kpos = start + lax.broadcasted_iota(jnp.int32, (kc, 1), 0)
        k, bias = ks_ref[pl.ds(start, kc), :], selected_bias(c, kpos)
        return [(k, q_t, bias) for q_t in q_ts]

    def window_logits(c):
        start = pl.multiple_of(c * kc, kc)
        kpos = start + lax.broadcasted_iota(jnp.int32, (kc, 1), 0)
        k, bias = kw_ref[pl.ds(start, kc), :], window_bias(kpos)
        return [(k, q_t, bias) for q_t in q_ts]

    bufs = [(s_ref.at[c], p_ref.at[c], acc_ref.at[c]) for c in range(2 * n_chain)]

    def selected_only(c, states):
        return _flash_stage(bufs[:n_chain], states, [vst_ref[jnp.maximum(c - 1, 0)]] * n_chain,
                            selected_logits(c + 1))

    def selected_and_window(c, states, is_last=False):
        prev = jnp.maximum(c - 1, 0)
        following = None if is_last else selected_logits(c + 1) + window_logits(c + 1)
        return _flash_stage(bufs, states, [vst_ref[prev]] * n_chain + [vwt_ref[prev]] * n_chain, following)

    first = jnp.maximum(q0 - (NSA_WINDOW - 1), 0) // kc
    last = (q0 + tq - 1) // kc
    states = tuple(_flash_start(buf, *f) for buf, f in zip(bufs[:n_chain], selected_logits(0)))
    states = lax.fori_loop(0, first, selected_only, states)
    states = states + tuple(_flash_start(buf, *f) for buf, f in zip(bufs[n_chain:], window_logits(first)))
    states = lax.fori_loop(first, last, selected_and_window, states)
    states = selected_and_window(last, states, is_last=True)

    for c, state in enumerate(states):
        out_ref, v_last = (os_ref, vst_ref[last]) if c < n_chain else (ow_ref, vwt_ref[last])
        out = _flash_finish(bufs[c], state, v_last)
        for i in range(per):
            r = (c % n_chain) * per + i
            out_ref[:, r * HEAD_DIM:(r + 1) * HEAD_DIM] = out[:, i * tq:(i + 1) * tq].T.astype(BF16)


def nsa_attention(q_rot, kv, sel, batch, seq, tq=256):
    m, d = q_rot.shape
    groups = NSA_KV_GROUPS
    q_per_kv = d // HEAD_DIM // groups
    gw = q_per_kv * HEAD_DIM
    tq = min(tq, seq)
    nq = seq // tq
    n_slc = sel.shape[2]
    assert seq % KV_CHUNK == 0 and KV_CHUNK % NSA_SLC_LEN == 0

    def kv_spec(part):
        return pl.BlockSpec((seq, HEAD_DIM), lambda b, g, i: (b, part * groups + g))

    q_spec = pl.BlockSpec((tq, gw), lambda b, g, i: (b * nq + i, g))
    out = jax.ShapeDtypeStruct((m, d), BF16)
    v_t = pltpu.VMEM((seq // KV_CHUNK, HEAD_DIM, KV_CHUNK), BF16)
    per = 1
    return pl.pallas_call(
        functools.partial(_nsa_attn_body, q_per_kv=q_per_kv, per=per),
        grid=(batch, groups, nq),
        in_specs=[q_spec, kv_spec(0), kv_spec(1), kv_spec(2), kv_spec(3),
                  pl.BlockSpec((None, None, n_slc, tq), lambda b, g, i: (b, g, 0, i))],
        out_specs=[q_spec, q_spec],
        out_shape=[out, out],
        scratch_shapes=[v_t, v_t, pltpu.VMEM((n_slc, tq), F32)]
        + _flash_buffers(2 * q_per_kv // per, KV_CHUNK, per * tq),
        compiler_params=_params(("parallel", "parallel", "arbitrary"), 32),
        name="nsa_attention",
    )(q_rot, kv, kv, kv, kv, sel)


def nsa_layer(h, gain, w_q, w_kv_raw, w_kv_rot, q_gain, k_gain, cmp_pos, cmp_w1, cmp_w2, w_gate, w_o,
              tables, batch, seq):
    m, d = h.shape
    n_heads = d // HEAD_DIM
    groups = NSA_KV_GROUPS
    gwk = groups * HEAD_DIM
    tn = 512
    tm = min(1024, m)
    tab = pl.BlockSpec((tm, HEAD_DIM), lambda i, j: (i, 0))
    col = pl.BlockSpec((tm, tn), lambda i, j: (i, j))
    one_w = [lambda i, j: (0, j)]

    def q_epilogue(accs, cols, extra, outs):
        _head_norm_rope_store(accs[0], cols, extra[0], extra[1:], outs[0], outs[1])

    q_shape = jax.ShapeDtypeStruct((m, d), BF16)
    q_cmp, q_rot = norm_matmul(
        h, gain, w_q, one_w, d // tn, tn,
        extra=[(jnp.tile(q_gain, n_heads)[None], pl.BlockSpec((1, tn), lambda i, j: (0, j)))]
        + [(t, tab) for t in tables],
        outs=[(q_shape, col), (q_shape, col)], variants=[(None, q_epilogue)], sub=2, tm=tm)

    def raw_epilogue(accs, cols, extra, outs):
        outs[0][:, cols] = accs[0]

    (raw,) = norm_matmul(
        h, gain, w_kv_raw, one_w, 2 * gwk // tn, tn, extra=[],
        outs=[(jax.ShapeDtypeStruct((m, 2 * gwk), F32), col)], variants=[(None, raw_epilogue)], tm=tm)

    assert tn == gwk
    kv_gain_row = jnp.concatenate([jnp.tile(k_gain[1], groups), jnp.ones((gwk,), F32),
                                   jnp.tile(k_gain[2], groups), jnp.ones((gwk,), F32)])[None]

    def k_epilogue(accs, cols, extra, outs):
        _head_norm_rope_store(accs[0], cols, extra[0], extra[1:], None, outs[0])

    def v_epilogue(accs, cols, extra, outs):
        outs[0][:, cols] = accs[0].astype(BF16)

    (kv,) = norm_matmul(
        h, gain, w_kv_rot, one_w, 4, tn,
        extra=[(kv_gain_row, pl.BlockSpec((1, tn), lambda i, j: (0, j)))] + [(t, tab) for t in tables],
        outs=[(jax.ShapeDtypeStruct((m, 4 * gwk), BF16), col)],
        variants=[(lambda j: j % 2 == 0, k_epilogue), (lambda j: j % 2 == 1, v_epilogue)], sub=2, tm=tm)

    def gate_epilogue(accs, cols, extra, outs):
        outs[0][:, cols] = jax.nn.sigmoid(accs[0])

    gate_w = w_gate.shape[1]
    (gates,) = norm_matmul(
        h, gain, w_gate, one_w, 1, gate_w, extra=[],
        outs=[(jax.ShapeDtypeStruct((m, gate_w), F32), pl.BlockSpec((tm, gate_w), lambda i, j: (i, j)))],
        variants=[(None, gate_epilogue)], tm=tm)

    k_cmp, v_cmp = nsa_compress(raw, cmp_pos, cmp_w1, cmp_w2, k_gain[0], batch, seq)
    o_cmp, sel = nsa_cmp_select(q_cmp, k_cmp, v_cmp, batch, seq)
    o_slc, o_win = nsa_attention(q_rot, kv, sel, batch, seq)

    def prologue(ins):
        g_ref, oc_ref, os_ref, ow_ref = ins
        parts = []
        for hh in range(n_heads):
            sl = slice(hh * HEAD_DIM, (hh + 1) * HEAD_DIM)
            mixed = (g_ref[:, 3 * hh:3 * hh + 1] * oc_ref[:, sl].astype(F32)
                     + g_ref[:, 3 * hh + 1:3 * hh + 2] * os_ref[:, sl].astype(F32)
                     + g_ref[:, 3 * hh + 2:3 * hh + 3] * ow_ref[:, sl].astype(F32))
            parts.append(mixed.astype(BF16))
        return jnp.concatenate(parts, axis=1)

    tmo = min(512, m)
    full = pl.BlockSpec((tmo, d), lambda i: (i, 0))
    return out_matmul([(gates, pl.BlockSpec((tmo, gate_w), lambda i: (i, 0))),
                       (o_cmp, full), (o_slc, full), (o_win, full)], prologue, w_o, h, tm=tmo)


def kernel(x, p, positions, mixer_norm, mlp_norm, mlp_w1, mlp_w2, ple_norm, ple_gate, ple_proj,
           moba_w_qkv, moba_q_gain, moba_k_gain, moba_w_o, pool_w, pool_scale,
           nsa_w_q, nsa_w_kv, nsa_q_gain, nsa_k_gain, nsa_cmp_pos, nsa_cmp_w1, nsa_cmp_w2,
           nsa_w_gate, nsa_w_o, conv_w_in, conv_w, conv_b, conv_w_o):
    batch, seq, d = x.shape
    depth = p.shape[0]
    m = batch * seq
    n_heads = d // HEAD_DIM
    gwk = NSA_KV_GROUPS * HEAD_DIM
    tables = rope_tables(positions)
    bf = lambda w: w.astype(BF16)
    gate_pad = (-3 * n_heads) % HEAD_DIM
    mlp_w1_bf, mlp_w2_bf = bf(mlp_w1), bf(mlp_w2)
    ple_gate_bf, ple_proj_bf = bf(ple_gate), bf(ple_proj)
    p_rows = p.reshape(depth, m, -1)

    h = x.reshape(m, d)
    for i in range(depth):
        kind, j = i % 4, i // 4
        if kind == 0:
            h = moba_layer(h, mixer_norm[i], bf(moba_w_qkv[j]), moba_q_gain[j], moba_k_gain[j],
                           bf(moba_w_o[j]), tables, batch, seq)
        elif kind == 1:
            h = pool_layer(h, mixer_norm[i], bf(pool_w[j]), pool_scale[j], seq)
        elif kind == 2:
            w_kv = nsa_w_kv[j]
            w_gate = jnp.pad(nsa_w_gate[j], ((0, 0), (0, gate_pad)))
            h = nsa_layer(h, mixer_norm[i], bf(nsa_w_q[j]), bf(w_kv[:, :2 * gwk]), bf(w_kv[:, 2 * gwk:]),
                          nsa_q_gain[j], nsa_k_gain[j], nsa_cmp_pos[j], bf(nsa_cmp_w1[j]),
                          bf(nsa_cmp_w2[j]), bf(w_gate), bf(nsa_w_o[j]), tables, batch, seq)
        else:
            h = conv_layer(h, mixer_norm[i], bf(conv_w_in[j]), conv_w[j], conv_b[j], bf(conv_w_o[j]), seq)
        h = mlp_layer(h, mlp_norm[i], mlp_w1_bf, mlp_w2_bf, i)
        h = ple_layer(h, ple_norm[i], ple_gate_bf, p_rows, ple_proj_bf, i)
    return h.reshape(batch, seq, d)
```

```python
import functools

import jax
import jax.numpy as jnp
from jax import lax
from jax.experimental import pallas as pl
from jax.experimental.pallas import tpu as pltpu

F32 = jnp.float32
BF16 = jnp.bfloat16

HEAD_DIM = 128
ROT_DIM = HEAD_DIM // 4
ROPE_THETA = 500000.0
NORM_EPS = 1e-6
MOBA_BLOCK = 256
MOBA_TOPK = 3
POOL_WINDOWS = (2, 4, 8, 16)
NSA_KV_GROUPS = 4
NSA_CMP_LEN = 32
NSA_CMP_STRIDE = 16
NSA_SLC_LEN = 64
NSA_SLC_TOPK = 16
NSA_WINDOW = 512
CONV_WIDTH = 3

MASKED_LOGIT = -1e30
SOFTMAX_EXP2_SCALE = HEAD_DIM ** -0.5 * 1.4426950408889634
KV_CHUNK = 256
MIB = 1024 * 1024


def _params(semantics, vmem_mib):
    return pltpu.CompilerParams(dimension_semantics=semantics,
                                vmem_limit_bytes=vmem_mib * MIB)


def _resident(shape, index_map):
    return pl.BlockSpec(shape, index_map, pipeline_mode=pl.Buffered(1))


def _rms(x, gain):
    ms = jnp.mean(x * x, axis=-1, keepdims=True)
    return x * lax.rsqrt(ms + NORM_EPS) * gain


def _split_bf16(x):
    hi = x.astype(BF16)
    return hi, (x - hi.astype(F32)).astype(BF16)


def _rope(x, cos, sin):
    half = ROT_DIM // 2
    lane = lax.broadcasted_iota(jnp.int32, x.shape, 1)
    partner = jnp.where(lane < half, -pltpu.roll(x, HEAD_DIM - half, 1), pltpu.roll(x, half, 1))
    return x * cos + partner * sin


def _nt_dot(a, b):
    return lax.dot_general(a, b, (((1,), (1,)), ((), ())), preferred_element_type=F32)


def _rope_table_body(pos_ref, freq_ref, cos_ref, sin_ref):
    ang = pos_ref[...].astype(F32) * freq_ref[...]
    cos_ref[...] = jnp.cos(ang)
    sin_ref[...] = jnp.sin(ang)


def rope_tables(positions):
    m = positions.size
    half = ROT_DIM // 2
    freqs = jnp.float32(ROPE_THETA) ** (-jnp.arange(half, dtype=F32) * 2.0 / ROT_DIM)
    freq_row = jnp.concatenate([freqs, freqs, jnp.zeros((HEAD_DIM - ROT_DIM,), F32)])[None]
    tm = min(m, 1024)
    tab = pl.BlockSpec((tm, HEAD_DIM), lambda i: (i, 0))
    return pl.pallas_call(
        _rope_table_body,
        grid=(m // tm,),
        in_specs=[pl.BlockSpec((tm, 1), lambda i: (i, 0)), pl.BlockSpec((1, HEAD_DIM), lambda i: (0, 0))],
        out_specs=[tab, tab],
        out_shape=[jax.ShapeDtypeStruct((m, HEAD_DIM), F32)] * 2,
        compiler_params=_params(("parallel",), 32),
        name="rope_tables",
    )(positions.reshape(m, 1), freq_row)


def _nmm_body(*refs, n_w, n_extra, n_out, sub, variants, on_row_start):
    x_ref, g_ref = refs[0], refs[1]
    w_refs = refs[2:2 + n_w]
    extra = refs[2 + n_w:2 + n_w + n_extra]
    outs = refs[2 + n_w + n_extra:2 + n_w + n_extra + n_out]
    xn_ref = refs[2 + n_w + n_extra + n_out]
    extra = extra + refs[3 + n_w + n_extra + n_out:]
    j = pl.program_id(1)

    @pl.when(j == 0)
    def _():
        xn_ref[...] = _rms(x_ref[...], g_ref[...]).astype(BF16)
        if on_row_start is not None:
            on_row_start(extra)

    width = w_refs[0].shape[1] // sub
    cols = [slice(s * width, (s + 1) * width) for s in range(sub)]

    def run(epilogue):
        xn = xn_ref[...]
        accs = [[jnp.dot(xn, w[:, c], preferred_element_type=F32) for w in w_refs] for c in cols]
        for c, acc in zip(cols, accs):
            epilogue(acc, c, extra, outs)

    if len(variants) == 1:
        run(variants[0][1])
    else:
        for applies, epilogue in variants:
            pl.when(applies(j))(functools.partial(run, epilogue))


def norm_matmul(h, gain, w, w_maps, n_col_tiles, tn, extra, outs, variants, sub=1, tm=1024, vmem_mib=56,
                scratch=(), on_row_start=None):
    m, d = h.shape
    tm = min(tm, m)
    in_specs = [pl.BlockSpec((tm, d), lambda i, j: (i, 0)),
                pl.BlockSpec((1, d), lambda i, j: (0, 0))]
    in_specs += [pl.BlockSpec((d, tn), wm) for wm in w_maps]
    in_specs += [spec for _, spec in extra]
    return pl.pallas_call(
        functools.partial(_nmm_body, n_w=len(w_maps), n_extra=len(extra), n_out=len(outs), sub=sub,
                          variants=variants, on_row_start=on_row_start),
        grid=(m // tm, n_col_tiles),
        in_specs=in_specs,
        out_specs=[spec for _, spec in outs],
        out_shape=[shape for shape, _ in outs],
        scratch_shapes=[pltpu.VMEM((tm, d), BF16)] + list(scratch),
        compiler_params=_params(("arbitrary" if scratch else "parallel", "arbitrary"), vmem_mib),
        name="norm_matmul",
    )(h, gain.reshape(1, d), *([w] * len(w_maps)), *[a for a, _ in extra])


def _head_norm_rope_store(acc, cols, gain_ref, table_refs, out_plain, out_rot):
    width = acc.shape[1]
    same_head = (lax.broadcasted_iota(jnp.int32, (width, width), 0) // HEAD_DIM
                 == lax.broadcasted_iota(jnp.int32, (width, width), 1) // HEAD_DIM)
    ones = jnp.where(same_head, 1.0, 0.0).astype(BF16)
    sq_hi, sq_lo = _split_bf16(acc * acc)
    ssq = (jnp.dot(sq_hi, ones, preferred_element_type=F32)
           + jnp.dot(sq_lo, ones, preferred_element_type=F32))
    normed = acc * lax.rsqrt(ssq * (1.0 / HEAD_DIM) + NORM_EPS) * gain_ref[:, cols]
    if out_plain is not None:
        out_plain[:, cols] = normed.astype(BF16)
    if out_rot is not None:
        cos, sin = table_refs[0][...], table_refs[1][...]
        for hh in range(width // HEAD_DIM):
            sl = slice(hh * HEAD_DIM, (hh + 1) * HEAD_DIM)
            dst = slice(cols.start + hh * HEAD_DIM, cols.start + (hh + 1) * HEAD_DIM)
            out_rot[:, dst] = _rope(normed[:, sl], cos, sin).astype(BF16)


def _omm_body(*refs, n_in, prologue):
    ins = refs[:n_in]
    w_ref, res_ref, o_ref = refs[n_in:n_in + 3]
    a = prologue(ins)
    o_ref[...] = res_ref[...] + jnp.dot(a, w_ref[...], preferred_element_type=F32)


def out_matmul(ins, prologue, w, res, tm=512, vmem_mib=48):
    m, d = res.shape
    tm = min(tm, m)
    k = w.shape[0]
    return pl.pallas_call(
        functools.partial(_omm_body, n_in=len(ins), prologue=prologue),
        grid=(m // tm,),
        in_specs=[spec for _, spec in ins] + [
            _resident((k, d), lambda i: (0, 0)),
            pl.BlockSpec((tm, d), lambda i: (i, 0))],
        out_specs=pl.BlockSpec((tm, d), lambda i: (i, 0)),
        out_shape=jax.ShapeDtypeStruct((m, d), F32),
        compiler_params=_params(("parallel",), vmem_mib),
        name="out_matmul",
    )(*[a for a, _ in ins], w, res)


def _mlp_body(x_ref, g_ref, w1_ref, w2_ref, o_ref, xn_ref):
    @pl.when(pl.program_id(1) == 0)
    def _():
        x = x_ref[...]
        xn_ref[...] = _rms(x, g_ref[...]).astype(BF16)
        o_ref[...] = x

    a = jnp.dot(xn_ref[...], w1_ref[...], preferred_element_type=F32)
    a = jnp.square(jnp.maximum(a, 0.0)).astype(BF16)
    o_ref[...] += jnp.dot(a, w2_ref[...], preferred_element_type=F32)


def mlp_layer(h, gain, w1, w2, layer, tm=1024, tf=512):
    m, d = h.shape
    dff = w1.shape[2]
    tm = min(tm, m)
    return pl.pallas_call(
        _mlp_body,
        grid=(m // tm, dff // tf),
        in_specs=[pl.BlockSpec((tm, d), lambda i, f: (i, 0)),
                  pl.BlockSpec((1, d), lambda i, f: (0, 0)),
                  pl.BlockSpec((None, d, tf), lambda i, f: (layer, 0, f)),
                  pl.BlockSpec((None, tf, d), lambda i, f: (layer, f, 0))],
        out_specs=pl.BlockSpec((tm, d), lambda i, f: (i, 0)),
        out_shape=jax.ShapeDtypeStruct((m, d), F32),
        scratch_shapes=[pltpu.VMEM((tm, d), BF16)],
        compiler_params=_params(("parallel", "arbitrary"), 56),
        name="mlp",
    )(h, gain.reshape(1, d), w1, w2)


def _ple_body(x_ref, g_ref, wg_ref, p_ref, wp_ref, o_ref):
    x = x_ref[...]
    xn = _rms(x, g_ref[...]).astype(BF16)
    gate = jax.nn.sigmoid(jnp.dot(xn, wg_ref[...], preferred_element_type=F32))
    emb = jnp.dot(p_ref[...].astype(BF16), wp_ref[...], preferred_element_type=F32)
    o_ref[...] = x + gate * emb


def ple_layer(h, gain, wg, p, wp, layer, tm=512):
    m, d = h.shape
    pd = p.shape[2]
    tm = min(tm, m)
    return pl.pallas_call(
        _ple_body,
        grid=(m // tm,),
        in_specs=[pl.BlockSpec((tm, d), lambda i: (i, 0)),
                  pl.BlockSpec((1, d), lambda i: (0, 0)),
                  _resident((None, d, d), lambda i: (layer, 0, 0)),
                  pl.BlockSpec((None, tm, pd), lambda i: (layer, i, 0)),
                  _resident((None, pd, d), lambda i: (layer, 0, 0))],
        out_specs=pl.BlockSpec((tm, d), lambda i: (i, 0)),
        out_shape=jax.ShapeDtypeStruct((m, d), F32),
        compiler_params=_params(("parallel",), 48),
        name="ple",
    )(h, gain.reshape(1, d), wg, p, wp)


def _flash_buffers(chains, keys, queries):
    return [pltpu.VMEM((chains, keys, queries), F32), pltpu.VMEM((chains, keys, queries), BF16),
            pltpu.VMEM((chains, HEAD_DIM, queries), F32)]


def _flash_logits(buf, k, q_t, bias):
    s = jnp.dot(k, q_t, preferred_element_type=F32) + bias
    buf[0][...] = s
    return jnp.max(s, axis=0, keepdims=True)


def _flash_start(buf, k, q_t, bias):
    s_ref, p_ref, acc_ref = buf
    p_ref[...] = jnp.zeros_like(p_ref)
    acc_ref[...] = jnp.zeros_like(acc_ref)
    queries = s_ref.shape[1]
    return (_flash_logits(buf, k, q_t, bias), jnp.ones((1, queries), F32),
            jnp.full((1, queries), MASKED_LOGIT, F32), jnp.zeros((1, queries), F32))


def _flash_stage(bufs, states, v_prev, following):
    products = [jnp.dot(v, buf[1][...], preferred_element_type=F32) for v, buf in zip(v_prev, bufs)]
    if following is not None:
        upcoming = [jnp.dot(k, q_t, preferred_element_type=F32) for k, q_t, _ in following]
    out = []
    for c, ((s_ref, p_ref, acc_ref), (s_max, alpha_prev, m, l), pv) in enumerate(zip(bufs, states, products)):
        m_new = jnp.maximum(m, s_max)
        p = jnp.exp2((s_ref[...] - m_new) * SOFTMAX_EXP2_SCALE)
        alpha = jnp.exp2((m - m_new) * SOFTMAX_EXP2_SCALE)
        l = alpha * l + jnp.sum(p, axis=0, keepdims=True)
        p_ref[...] = p.astype(BF16)
        acc_ref[...] = alpha_prev * acc_ref[...] + pv
        if following is not None:
            s_next = upcoming[c] + following[c][2]
            s_ref[...] = s_next
            s_max = jnp.max(s_next, axis=0, keepdims=True)
        out.append((s_max, alpha, m_new, l))
    return tuple(out)


def _flash_finish(buf, state, v_last):
    _, p_ref, acc_ref = buf
    _, alpha, _, l = state
    return (alpha * acc_ref[...] + jnp.dot(v_last, p_ref[...], preferred_element_type=F32)) / l


def _store_transposed(v_ref, vt_ref, chunk):
    for n in range(vt_ref.shape[0]):
        vt_ref[n] = v_ref[n * chunk:(n + 1) * chunk, :].astype(F32).T.astype(BF16)


def _top_rank(v, row):
    n = v.shape[0]
    if n % 8:
        rank = jnp.zeros(v.shape, jnp.int32)
        for j in range(n):
            r = v[j:j + 1, :]
            rank = rank + jnp.where(row > j, jnp.where(r >= v, 1, 0), jnp.where(r > v, 1, 0))
        return rank
    starts = range(0, n, 8)
    groups = [v[lo:lo + 8, :] for lo in starts]
    ranks = [jnp.zeros((8, v.shape[1]), jnp.int32) for _ in starts]
    for j in range(n):
        r = jnp.broadcast_to(v[j:j + 1, :], (8, v.shape[1]))
        for g, lo in enumerate(starts):
            if lo > j:
                ranks[g] = ranks[g] + jnp.where(r >= groups[g], 1, 0)
            elif lo + 7 <= j:
                ranks[g] = ranks[g] + jnp.where(r > groups[g], 1, 0)
            else:
                below = lax.broadcasted_iota(jnp.int32, (8, v.shape[1]), 0) > j - lo
                ranks[g] = ranks[g] + jnp.where(below, jnp.where(r >= groups[g], 1, 0),
                                                jnp.where(r > groups[g], 1, 0))
    return jnp.concatenate(ranks, axis=0)


def _moba_body(q_ref, k_ref, v_ref, o_ref, kmean_ref, vt_ref, bias_ref, s_ref, p_ref, acc_ref,
               *, n_blocks, heads):
    blk = MOBA_BLOCK
    own = pl.program_id(2)
    cols = [slice(e * HEAD_DIM, (e + 1) * HEAD_DIM) for e in range(heads)]
    bufs = [(s_ref.at[e], p_ref.at[e], acc_ref.at[e]) for e in range(heads)]

    @pl.when(own == 0)
    def _():
        for e in range(heads):
            for n in range(n_blocks):
                kb = k_ref[n * blk:(n + 1) * blk, cols[e]].astype(F32)
                kmean_ref[e, n:n + 1, :] = jnp.mean(kb, axis=0, keepdims=True)
                vt_ref[e, n] = v_ref[n * blk:(n + 1) * blk, cols[e]].astype(F32).T.astype(BF16)

    q_ts = [q_ref[:, cols[e]].astype(F32).T.astype(BF16) for e in range(heads)]
    for e in range(heads):
        gate = jnp.dot(kmean_ref[e].astype(BF16), q_ts[e], preferred_element_type=F32)
        row = lax.broadcasted_iota(jnp.int32, gate.shape, 0)
        gate = jnp.where(row < own, gate, -jnp.inf)
        rank = _top_rank(gate, row)
        bias_ref[e] = jnp.where(row < own, jnp.where(rank < MOBA_TOPK, 0.0, MASKED_LOGIT), MASKED_LOGIT)

    def keys(j):
        start = pl.multiple_of(j * blk, blk)
        return [k_ref[pl.ds(start, blk), cols[e]] for e in range(heads)]

    def values(j):
        return [vt_ref[e, j] for e in range(heads)]

    def stage(j, states, next_biases):
        following = None if next_biases is None else list(zip(keys(j + 1), q_ts, next_biases))
        return _flash_stage(bufs, states, values(jnp.maximum(j - 1, 0)), following)

    def chosen_bias(j):
        return [bias_ref[e, pl.ds(j, 1), :] for e in range(heads)]

    causal = jnp.where(lax.broadcasted_iota(jnp.int32, (blk, blk), 0)
                       <= lax.broadcasted_iota(jnp.int32, (blk, blk), 1), 0.0, MASKED_LOGIT)
    first_bias = [jnp.where(own == 0, causal, b) for b in chosen_bias(0)]
    states = tuple(_flash_start(buf, k, q_t, b) for buf, k, q_t, b in zip(bufs, keys(0), q_ts, first_bias))
    states = lax.fori_loop(0, own - 1, lambda j, st: stage(j, st, chosen_bias(j + 1)), states)
    states = lax.cond(own > 0, lambda st: stage(own - 1, st, [causal] * heads), lambda st: st, states)
    states = stage(own, states, None)
    for e, v_last in enumerate(values(own)):
        o_ref[:, cols[e]] = _flash_finish(bufs[e], states[e], v_last).T.astype(BF16)


def moba_attention(qkv, batch, seq, n_heads, heads_per_step=4):
    blk = MOBA_BLOCK
    nq = seq // blk
    hp = heads_per_step
    width = hp * HEAD_DIM
    groups = n_heads // hp
    return pl.pallas_call(
        functools.partial(_moba_body, n_blocks=nq, heads=hp),
        grid=(batch, groups, nq),
        in_specs=[pl.BlockSpec((blk, width), lambda b, h, i: (b * nq + i, h)),
                  pl.BlockSpec((seq, width), lambda b, h, i: (b, groups + h)),
                  pl.BlockSpec((seq, width), lambda b, h, i: (b, 2 * groups + h))],
        out_specs=pl.BlockSpec((blk, width), lambda b, h, i: (b * nq + i, h)),
        out_shape=jax.ShapeDtypeStruct((batch * seq, n_heads * HEAD_DIM), BF16),
        scratch_shapes=[pltpu.VMEM((hp, nq, HEAD_DIM), F32),
                        pltpu.VMEM((hp, nq, HEAD_DIM, blk), BF16),
                        pltpu.VMEM((hp, nq, blk), F32)] + _flash_buffers(hp, blk, blk),
        compiler_params=_params(("parallel", "parallel", "arbitrary"), 32),
        name="moba_attention",
    )(qkv, qkv, qkv)


def moba_layer(h, gain, w_qkv, q_gain, k_gain, w_o, tables, batch, seq):
    m, d = h.shape
    n_heads = d // HEAD_DIM
    tn = 1024
    n_qk_tiles = 2 * d // tn
    gain_row = jnp.concatenate([jnp.tile(q_gain, n_heads), jnp.tile(k_gain, n_heads),
                                jnp.ones((d,), F32)])[None]

    def qk_epilogue(accs, cols, extra, outs):
        _head_norm_rope_store(accs[0], cols, extra[0], extra[1:], None, outs[0])

    def v_epilogue(accs, cols, extra, outs):
        outs[0][:, cols] = accs[0].astype(BF16)

    tm = min(1024, m)
    tab = pl.BlockSpec((tm, HEAD_DIM), lambda i, j: (i, 0))
    (qkv,) = norm_matmul(
        h, gain, w_qkv, [lambda i, j: (0, j)], 3 * d // tn, tn,
        extra=[(gain_row, pl.BlockSpec((1, tn), lambda i, j: (0, j)))] + [(t, tab) for t in tables],
        outs=[(jax.ShapeDtypeStruct((m, 3 * d), BF16), pl.BlockSpec((tm, tn), lambda i, j: (i, j)))],
        variants=[(lambda j: j < n_qk_tiles, qk_epilogue), (lambda j: j >= n_qk_tiles, v_epilogue)],
        sub=4, tm=tm)
    o = moba_attention(qkv, batch, seq, n_heads)
    tmo = min(512, m)
    return out_matmul([(o, pl.BlockSpec((tmo, d), lambda i: (i, 0)))],
                      lambda ins: ins[0][...], w_o, h, tm=tmo)


def _pool_body(x_ref, halo_ref, g_ref, w_ref, s_ref, o_ref, *, tiles_per_seq, halo):
    i = pl.program_id(0)
    tm, d = x_ref.shape
    group = d // len(POOL_WINDOWS)
    x = x_ref[...]
    gain = g_ref[...]
    xn = _rms(x, gain)
    prev = jnp.where(i % tiles_per_seq == 0, 0.0, _rms(halo_ref[...], gain))
    pos = (i % tiles_per_seq) * tm + lax.broadcasted_iota(jnp.int32, (tm, 1), 0)
    for g, win in enumerate(POOL_WINDOWS):
        sl = slice(g * group, (g + 1) * group)
        run = jnp.concatenate([prev[:, sl], xn[:, sl]], axis=0)
        span = 1
        while span < win:
            run = run + pltpu.roll(run, span, 0)
            span *= 2
        cnt = jnp.minimum(pos + 1, win).astype(F32)
        mean = run[halo:, :] / cnt
        mix = jnp.dot((mean - xn[:, sl]).astype(BF16), w_ref[g], preferred_element_type=F32)
        o_ref[:, sl] = x[:, sl] + mix * s_ref[:, sl]


def pool_layer(h, gain, w_groups, scale, seq, tm=512):
    m, d = h.shape
    halo = 16
    assert max(POOL_WINDOWS) <= halo
    tm = min(tm, seq)
    group = d // len(POOL_WINDOWS)
    return pl.pallas_call(
        functools.partial(_pool_body, tiles_per_seq=seq // tm, halo=halo),
        grid=(m // tm,),
        in_specs=[pl.BlockSpec((tm, d), lambda i: (i, 0)),
                  pl.BlockSpec((halo, d), lambda i: (jnp.maximum(i * (tm // halo) - 1, 0), 0)),
                  pl.BlockSpec((1, d), lambda i: (0, 0)),
                  _resident((len(POOL_WINDOWS), group, group), lambda i: (0, 0, 0)),
                  pl.BlockSpec((1, d), lambda i: (0, 0))],
        out_specs=pl.BlockSpec((tm, d), lambda i: (i, 0)),
        out_shape=jax.ShapeDtypeStruct((m, d), F32),
        compiler_params=_params(("parallel",), 48),
        name="pool_mixer",
    )(h, h, gain.reshape(1, d), w_groups, scale.reshape(1, d))


def conv_layer(h, gain, w_in, conv_w, conv_b, w_o, seq):
    m, d = h.shape
    tn = 512
    nj = d // tn

    tm = min(1024, seq)
    tiles_per_seq = seq // tm
    halo = 8

    def on_row_start(extra):
        @pl.when(pl.program_id(0) % tiles_per_seq == 0)
        def _():
            extra[2][...] = jnp.zeros_like(extra[2])

    def epilogue(accs, cols, extra, outs):
        cw_ref, cb_ref, carry_ref = extra
        j = pl.program_id(1)
        u0 = accs[1] * accs[2]
        prev = carry_ref[j, :, cols]
        row = lax.broadcasted_iota(jnp.int32, u0.shape, 0)
        u1 = jnp.where(row == 0, prev[halo - 1:halo, :], pltpu.roll(u0, 1, 0))
        u2 = jnp.where(row == 0, prev[halo - 2:halo - 1, :],
                       jnp.where(row == 1, prev[halo - 1:halo, :], pltpu.roll(u0, 2, 0)))
        conv = (cw_ref[0:1, cols] * u2 + cw_ref[1:2, cols] * u1 + cw_ref[2:3, cols] * u0
                + cb_ref[:, cols])
        outs[0][:, cols] = (accs[0] * conv).astype(BF16)
        carry_ref[j, :, cols] = u0[tm - halo:, :]

    (y,) = norm_matmul(
        h, gain, w_in, [lambda i, j: (0, j), lambda i, j: (0, j + nj), lambda i, j: (0, j + 2 * nj)],
        nj, tn,
        extra=[(conv_w, pl.BlockSpec((CONV_WIDTH, tn), lambda i, j: (0, j))),
               (conv_b.reshape(1, d), pl.BlockSpec((1, tn), lambda i, j: (0, j)))],
        outs=[(jax.ShapeDtypeStruct((m, d), BF16), pl.BlockSpec((tm, tn), lambda i, j: (i, j)))],
        variants=[(None, epilogue)], sub=2, tm=tm,
        scratch=[pltpu.VMEM((nj, halo, tn), F32)], on_row_start=on_row_start)

    tmo = min(512, m)
    return out_matmul([(y, pl.BlockSpec((tmo, d), lambda i: (i, 0)))],
                      lambda ins: ins[0][...], w_o, h, tm=tmo)


def _compress_body(x_ref, pos_ref, w1_ref, w2_ref, g_ref, kc_ref, vc_ref, lo_ref, hi_ref, *, groups):
    step = pl.program_id(1)
    stride = NSA_CMP_STRIDE

    @pl.when(step == 0)
    def _():
        lo_ref[...] = jnp.zeros_like(lo_ref)
        hi_ref[...] = jnp.zeros_like(hi_ref)

    for kv in range(2):
        p_lo = pos_ref[kv, pl.ds(step, 1), :]
        p_hi = pos_ref[kv, pl.ds(stride + step, 1), :]
        w_lo = w1_ref[kv, step]
        w_hi = w1_ref[kv, stride + step]
        for g in range(groups):
            c = kv * groups + g
            t = x_ref[:, c * HEAD_DIM:(c + 1) * HEAD_DIM]
            lo_ref[c] += jnp.dot((t + p_lo).astype(BF16), w_lo, preferred_element_type=F32)
            hi_ref[c] += jnp.dot((t + p_hi).astype(BF16), w_hi, preferred_element_type=F32)

    @pl.when(step == stride - 1)
    def _():
        n_half = lo_ref.shape[1]
        for kv in range(2):
            for g in range(groups):
                c = kv * groups + g
                pre = lo_ref[c] + pltpu.roll(hi_ref[c], n_half - 1, 0)
                hid = jax.nn.gelu(pre)
                if kv == 0:
                    out = jnp.dot(hid.astype(BF16), w2_ref[0], preferred_element_type=F32)
                    kc_ref[g] = _rms(out, g_ref[...]).astype(BF16)
                else:
                    vc_ref[g] = jnp.dot(w2_ref[1], hid.T.astype(BF16),
                                        preferred_element_type=F32).astype(BF16)


def nsa_compress(raw, cmp_pos, cmp_w1, cmp_w2, k_gain0, batch, seq):
    assert NSA_CMP_LEN == 2 * NSA_CMP_STRIDE
    stride = NSA_CMP_STRIDE
    groups = NSA_KV_GROUPS
    width = raw.shape[1]
    n_half = seq // stride
    x = raw.reshape(batch * n_half, stride * width)
    w1 = cmp_w1.reshape(2, NSA_CMP_LEN, HEAD_DIM, HEAD_DIM)
    cmp_w2 = jnp.stack([cmp_w2[0], cmp_w2[1].T])
    k_out = jax.ShapeDtypeStruct((batch, groups, n_half, HEAD_DIM), BF16)
    v_out = jax.ShapeDtypeStruct((batch, groups, HEAD_DIM, n_half), BF16)
    k_spec = pl.BlockSpec((None, groups, n_half, HEAD_DIM), lambda b, s: (b, 0, 0, 0))
    v_spec = pl.BlockSpec((None, groups, HEAD_DIM, n_half), lambda b, s: (b, 0, 0, 0))
    return pl.pallas_call(
        functools.partial(_compress_body, groups=groups),
        grid=(batch, stride),
        in_specs=[pl.BlockSpec((n_half, width), lambda b, s: (b, s)),
                  pl.BlockSpec((2, NSA_CMP_LEN, HEAD_DIM), lambda b, s: (0, 0, 0)),
                  pl.BlockSpec((2, NSA_CMP_LEN, HEAD_DIM, HEAD_DIM), lambda b, s: (0, 0, 0, 0)),
                  pl.BlockSpec((2, HEAD_DIM, HEAD_DIM), lambda b, s: (0, 0, 0)),
                  pl.BlockSpec((1, HEAD_DIM), lambda b, s: (0, 0))],
        out_specs=[k_spec, v_spec],
        out_shape=[k_out, v_out],
        scratch_shapes=[pltpu.VMEM((2 * groups, n_half, HEAD_DIM), F32)] * 2,
        compiler_params=_params(("parallel", "arbitrary"), 32),
        name="nsa_compress",
    )(x, cmp_pos, w1, cmp_w2, k_gain0.reshape(1, HEAD_DIM))


def _cmp_select_body(q_ref, kc_ref, vct_ref, gate_ref, o_ref, sel_ref, *, n_cmp, n_top, q_per_kv):
    tq = q_ref.shape[0]
    first_head = pl.program_id(1) * q_per_kv
    n_pad = kc_ref.shape[0]
    n_slc = sel_ref.shape[0]
    scale = HEAD_DIM ** -0.5
    t = pl.program_id(2) * tq + lax.broadcasted_iota(jnp.int32, (1, tq), 1)
    n = lax.broadcasted_iota(jnp.int32, (n_pad, 1), 0)
    ok = (n < n_cmp) & (n * NSA_CMP_STRIDE + (NSA_CMP_LEN - 1) <= t)
    kc = kc_ref[...]
    vct = vct_ref[...]
    p_sum = jnp.zeros((n_pad, tq), F32)
    for r in range(q_per_kv):
        sl = slice(r * HEAD_DIM, (r + 1) * HEAD_DIM)
        s = jnp.where(ok, _nt_dot(kc, q_ref[:, sl]) * scale, -jnp.inf)
        mx = jnp.max(s, axis=0, keepdims=True)
        mx = jnp.where(mx > -jnp.inf, mx, 0.0)
        e = jnp.where(ok, jnp.exp(s - mx), 0.0)
        den = jnp.maximum(jnp.sum(e, axis=0, keepdims=True), jnp.finfo(F32).tiny)
        o_t = jnp.dot(vct, e.astype(BF16), preferred_element_type=F32) / den
        gate = gate_ref[pl.ds(3 * (first_head + r), 1), :]
        o_ref[:, sl] = (o_t * gate).T.astype(BF16)
        p_sum = p_sum + e / den

    jj = lax.broadcasted_iota(jnp.int32, (n_slc, n_pad), 0) * NSA_SLC_LEN
    nn = lax.broadcasted_iota(jnp.int32, (n_slc, n_pad), 1) * NSA_CMP_STRIDE
    overlap = jnp.where((nn < jj + NSA_SLC_LEN) & (jj < nn + NSA_CMP_LEN)
                        & (nn < n_cmp * NSA_CMP_STRIDE), 1.0, 0.0).astype(BF16)
    p_hi = p_sum.astype(BF16)
    p_lo = (p_sum - p_hi.astype(F32)).astype(BF16)
    imp = (jnp.dot(overlap, p_hi, preferred_element_type=F32)
           + jnp.dot(overlap, p_lo, preferred_element_type=F32))

    cur = t // NSA_SLC_LEN
    jb = lax.broadcasted_iota(jnp.int32, (n_slc, tq), 0)
    val = jnp.where(jb == cur, jnp.inf,
                    jnp.where(jb == 0, jnp.inf, jnp.where(jb < cur, imp, -jnp.inf)))
    rank = _top_rank(val, jb)
    sel_ref[...] = jnp.where(rank < n_top, jnp.where(val > -jnp.inf, 1.0, 0.0), 0.0)


def nsa_cmp_select(q_cmp, k_cmp, v_cmp_t, gates_t, batch, seq, tq=256):
    m, d = q_cmp.shape
    groups = NSA_KV_GROUPS
    q_per_kv = d // HEAD_DIM // groups
    gw = q_per_kv * HEAD_DIM
    tq = min(tq, seq)
    nq = seq // tq
    n_pad = k_cmp.shape[2]
    n_cmp = (seq - NSA_CMP_LEN) // NSA_CMP_STRIDE + 1
    n_slc = seq // NSA_SLC_LEN
    return pl.pallas_call(
        functools.partial(_cmp_select_body, n_cmp=n_cmp, n_top=min(NSA_SLC_TOPK, n_slc),
                          q_per_kv=q_per_kv),
        grid=(batch, groups, nq),
        in_specs=[pl.BlockSpec((tq, gw), lambda b, g, i: (b * nq + i, g)),
                  pl.BlockSpec((None, None, n_pad, HEAD_DIM), lambda b, g, i: (b, g, 0, 0)),
                  pl.BlockSpec((None, None, HEAD_DIM, n_pad), lambda b, g, i: (b, g, 0, 0)),
                  pl.BlockSpec((gates_t.shape[0], tq), lambda b, g, i: (0, b * nq + i))],
        out_specs=[pl.BlockSpec((tq, gw), lambda b, g, i: (b * nq + i, g)),
                   pl.BlockSpec((None, None, n_slc, tq), lambda b, g, i: (b, g, 0, i))],
        out_shape=[jax.ShapeDtypeStruct((m, d), BF16),
                   jax.ShapeDtypeStruct((batch, groups, n_slc, seq), F32)],
        compiler_params=_params(("parallel", "parallel", "parallel"), 32),
        name="nsa_cmp_select",
    )(q_cmp, k_cmp, v_cmp_t, gates_t)


def _nsa_attn_body(q_ref, ks_ref, vs_ref, kw_ref, vw_ref, sel_ref, gate_ref, oc_ref, o_ref,
                   vst_ref, vwt_ref, bias_ref, s_ref, p_ref, acc_ref, *, q_per_kv, per):
    tq = q_ref.shape[0]
    kc = KV_CHUNK
    sub = kc // NSA_SLC_LEN
    step = pl.program_id(2)
    q0 = step * tq

    @pl.when(step == 0)
    def _():
        _store_transposed(vs_ref, vst_ref, kc)
        _store_transposed(vw_ref, vwt_ref, kc)

    n_chain = q_per_kv // per
    q_ts = [jnp.concatenate([q_ref[:, r * HEAD_DIM:(r + 1) * HEAD_DIM]
                             for r in range(c * per, (c + 1) * per)], axis=0).astype(F32).T.astype(BF16)
            for c in range(n_chain)]
    t = q0 + lax.broadcasted_iota(jnp.int32, (1, tq), 1)
    bias_ref[...] = jnp.where(sel_ref[...] > 0.5, 0.0, MASKED_LOGIT)

    def chain_bias(bias):
        return jnp.concatenate([bias] * per, axis=1)

    def selected_bias(c, kpos):
        picked = jnp.concatenate(
            [jnp.broadcast_to(bias_ref[pl.ds(c * sub + a, 1), :], (NSA_SLC_LEN, tq)) for a in range(sub)],
            axis=0)
        return chain_bias(jnp.where(kpos <= t, picked, MASKED_LOGIT))

    def window_bias(kpos):
        return chain_bias(jnp.where(kpos <= t, jnp.where(t - kpos < NSA_WINDOW, 0.0, MASKED_LOGIT),
                                    MASKED_LOGIT))

    def selected_logits(c):
        start = pl.multiple_of(c * kc, kc)
        kpos = start + lax.broadcasted_iota(jnp.int32, (kc, 1), 0)
        k, bias = ks_ref[pl.ds(start, kc), :], selected_bias(c, kpos)
        return [(k, q_t, bias) for q_t in q_ts]

    def window_logits(c):
        start = pl.multiple_of(c * kc, kc)
        kpos = start + lax.broadcasted_iota(jnp.int32, (kc, 1), 0)
        k, bias = kw_ref[pl.ds(start, kc), :], window_bias(kpos)
        return [(k, q_t, bias) for q_t in q_ts]

    bufs = [(s_ref.at[c], p_ref.at[c], acc_ref.at[c]) for c in range(2 * n_chain)]

    def selected_only(c, states):
        return _flash_stage(bufs[:n_chain], states, [vst_ref[jnp.maximum(c - 1, 0)]] * n_chain,
                            selected_logits(c + 1))

    def selected_and_window(c, states, is_last=False):
        prev = jnp.maximum(c - 1, 0)
        following = None if is_last else selected_logits(c + 1) + window_logits(c + 1)
        return _flash_stage(bufs, states, [vst_ref[prev]] * n_chain + [vwt_ref[prev]] * n_chain, following)

    first = jnp.maximum(q0 - (NSA_WINDOW - 1), 0) // kc
    last = (q0 + tq - 1) // kc
    states = tuple(_flash_start(buf, *f) for buf, f in zip(bufs[:n_chain], selected_logits(0)))
    states = lax.fori_loop(0, first, selected_only, states)
    states = states + tuple(_flash_start(buf, *f) for buf, f in zip(bufs[n_chain:], window_logits(first)))
    states = lax.fori_loop(first, last, selected_and_window, states)
    states = selected_and_window(last, states, is_last=True)

    first_head = pl.program_id(1) * q_per_kv
    for c in range(n_chain):
        selected = _flash_finish(bufs[c], states[c], vst_ref[last])
        window = _flash_finish(bufs[n_chain + c], states[n_chain + c], vwt_ref[last])
        for i in range(per):
            r = c * per + i
            lanes = slice(i * tq, (i + 1) * tq)
            cols = slice(r * HEAD_DIM, (r + 1) * HEAD_DIM)
            g_slc = gate_ref[pl.ds(3 * (first_head + r) + 1, 1), :]
            g_win = gate_ref[pl.ds(3 * (first_head + r) + 2, 1), :]
            mixed = (selected[:, lanes] * g_slc + window[:, lanes] * g_win).T
            o_ref[:, cols] = (mixed + oc_ref[:, cols].astype(F32)).astype(BF16)


def nsa_attention(q_rot, kv, sel, gates_t, o_cmp, batch, seq, tq=256):
    m, d = q_rot.shape
    groups = NSA_KV_GROUPS
    q_per_kv = d // HEAD_DIM // groups
    gw = q_per_kv * HEAD_DIM
    tq = min(tq, seq)
    nq = seq // tq
    n_slc = sel.shape[2]
    assert seq % KV_CHUNK == 0 and KV_CHUNK % NSA_SLC_LEN == 0

    def kv_spec(part):
        return pl.BlockSpec((seq, HEAD_DIM), lambda b, g, i: (b, part * groups + g))

    q_spec = pl.BlockSpec((tq, gw), lambda b, g, i: (b * nq + i, g))
    out = jax.ShapeDtypeStruct((m, d), BF16)
    v_t = pltpu.VMEM((seq // KV_CHUNK, HEAD_DIM, KV_CHUNK), BF16)
    per = 1
    return pl.pallas_call(
        functools.partial(_nsa_attn_body, q_per_kv=q_per_kv, per=per),
        grid=(batch, groups, nq),
        in_specs=[q_spec, kv_spec(0), kv_spec(1), kv_spec(2), kv_spec(3),
                  pl.BlockSpec((None, None, n_slc, tq), lambda b, g, i: (b, g, 0, i)),
                  pl.BlockSpec((gates_t.shape[0], tq), lambda b, g, i: (0, b * nq + i)),
                  q_spec],
        out_specs=q_spec,
        out_shape=out,
        scratch_shapes=[v_t, v_t, pltpu.VMEM((n_slc, tq), F32)]
        + _flash_buffers(2 * q_per_kv // per, KV_CHUNK, per * tq),
        compiler_params=_params(("parallel", "parallel", "arbitrary"), 32),
        name="nsa_attention",
    )(q_rot, kv, kv, kv, kv, sel, gates_t, o_cmp)


def nsa_layer(h, gain, w_q, w_kv_raw, w_kv_rot, q_gain, k_gain, cmp_pos, cmp_w1, cmp_w2, w_gate, w_o,
              tables, batch, seq):
    m, d = h.shape
    n_heads = d // HEAD_DIM
    groups = NSA_KV_GROUPS
    gwk = groups * HEAD_DIM
    tn = 512
    tm = min(1024, m)
    tab = pl.BlockSpec((tm, HEAD_DIM), lambda i, j: (i, 0))
    col = pl.BlockSpec((tm, tn), lambda i, j: (i, j))
    one_w = [lambda i, j: (0, j)]

    def q_epilogue(accs, cols, extra, outs):
        _head_norm_rope_store(accs[0], cols, extra[0], extra[1:], outs[0], outs[1])

    q_shape = jax.ShapeDtypeStruct((m, d), BF16)
    q_cmp, q_rot = norm_matmul(
        h, gain, w_q, one_w, d // tn, tn,
        extra=[(jnp.tile(q_gain, n_heads)[None], pl.BlockSpec((1, tn), lambda i, j: (0, j)))]
        + [(t, tab) for t in tables],
        outs=[(q_shape, col), (q_shape, col)], variants=[(None, q_epilogue)], sub=2, tm=tm)

    def raw_epilogue(accs, cols, extra, outs):
        outs[0][:, cols] = accs[0]

    (raw,) = norm_matmul(
        h, gain, w_kv_raw, one_w, 2 * gwk // tn, tn, extra=[],
        outs=[(jax.ShapeDtypeStruct((m, 2 * gwk), F32), col)], variants=[(None, raw_epilogue)], tm=tm)

    assert tn == gwk
    kv_gain_row = jnp.concatenate([jnp.tile(k_gain[1], groups), jnp.ones((gwk,), F32),
                                   jnp.tile(k_gain[2], groups), jnp.ones((gwk,), F32)])[None]

    def k_epilogue(accs, cols, extra, outs):
        _head_norm_rope_store(accs[0], cols, extra[0], extra[1:], None, outs[0])

    def v_epilogue(accs, cols, extra, outs):
        outs[0][:, cols] = accs[0].astype(BF16)

    (kv,) = norm_matmul(
        h, gain, w_kv_rot, one_w, 4, tn,
        extra=[(kv_gain_row, pl.BlockSpec((1, tn), lambda i, j: (0, j)))] + [(t, tab) for t in tables],
        outs=[(jax.ShapeDtypeStruct((m, 4 * gwk), BF16), col)],
        variants=[(lambda j: j % 2 == 0, k_epilogue), (lambda j: j % 2 == 1, v_epilogue)], sub=2, tm=tm)

    def gate_epilogue(accs, cols, extra, outs):
        outs[0][...] = jax.nn.sigmoid(accs[0]).T

    gate_w = w_gate.shape[1]
    (gates_t,) = norm_matmul(
        h, gain, w_gate, one_w, 1, gate_w, extra=[],
        outs=[(jax.ShapeDtypeStruct((gate_w, m), F32), pl.BlockSpec((gate_w, tm), lambda i, j: (0, i)))],
        variants=[(None, gate_epilogue)], tm=tm)

    k_cmp, v_cmp = nsa_compress(raw, cmp_pos, cmp_w1, cmp_w2, k_gain[0], batch, seq)
    o_cmp, sel = nsa_cmp_select(q_cmp, k_cmp, v_cmp, gates_t, batch, seq)
    o = nsa_attention(q_rot, kv, sel, gates_t, o_cmp, batch, seq)
    tmo = min(512, m)
    return out_matmul([(o, pl.BlockSpec((tmo, d), lambda i: (i, 0)))],
                      lambda ins: ins[0][...], w_o, h, tm=tmo)


def kernel(x, p, positions, mixer_norm, mlp_norm, mlp_w1, mlp_w2, ple_norm, ple_gate, ple_proj,
           moba_w_qkv, moba_q_gain, moba_k_gain, moba_w_o, pool_w, pool_scale,
           nsa_w_q, nsa_w_kv, nsa_q_gain, nsa_k_gain, nsa_cmp_pos, nsa_cmp_w1, nsa_cmp_w2,
           nsa_w_gate, nsa_w_o, conv_w_in, conv_w, conv_b, conv_w_o):
    batch, seq, d = x.shape
    depth = p.shape[0]
    m = batch * seq
    n_heads = d // HEAD_DIM
    gwk = NSA_KV_GROUPS * HEAD_DIM
    tables = rope_tables(positions)
    bf = lambda w: w.astype(BF16)
    gate_pad = (-3 * n_heads) % HEAD_DIM
    mlp_w1_bf, mlp_w2_bf = bf(mlp_w1), bf(mlp_w2)
    ple_gate_bf, ple_proj_bf = bf(ple_gate), bf(ple_proj)
    p_rows = p.reshape(depth, m, -1)

    h = x.reshape(m, d)
    for i in range(depth):
        kind, j = i % 4, i // 4
        if kind == 0:
            h = moba_layer(h, mixer_norm[i], bf(moba_w_qkv[j]), moba_q_gain[j], moba_k_gain[j],
                           bf(moba_w_o[j]), tables, batch, seq)
        elif kind == 1:
            h = pool_layer(h, mixer_norm[i], bf(pool_w[j]), pool_scale[j], seq)
        elif kind == 2:
            w_kv = nsa_w_kv[j]
            w_gate = jnp.pad(nsa_w_gate[j], ((0, 0), (0, gate_pad)))
            h = nsa_layer(h, mixer_norm[i], bf(nsa_w_q[j]), bf(w_kv[:, :2 * gwk]), bf(w_kv[:, 2 * gwk:]),
                          nsa_q_gain[j], nsa_k_gain[j], nsa_cmp_pos[j], bf(nsa_cmp_w1[j]),
                          bf(nsa_cmp_w2[j]), bf(w_gate), bf(nsa_w_o[j]), tables, batch, seq)
        else:
            h = conv_layer(h, mixer_norm[i], bf(conv_w_in[j]), conv_w[j], conv_b[j], bf(conv_w_o[j]), seq)
        h = mlp_layer(h, mlp_norm[i], mlp_w1_bf, mlp_w2_bf, i)
        h = ple_layer(h, ple_norm[i], ple_gate_bf, p_rows, ple_proj_bf, i)
    return h.reshape(batch, seq, d)
```

```python
import functools

import jax
import jax.numpy as jnp
from jax import lax
from jax.experimental import pallas as pl
from jax.experimental.pallas import tpu as pltpu

F32 = jnp.float32
BF16 = jnp.bfloat16

HEAD_DIM = 128
ROT_DIM = HEAD_DIM // 4
ROPE_THETA = 500000.0
NORM_EPS = 1e-6
MOBA_BLOCK = 256
MOBA_TOPK = 3
POOL_WINDOWS = (2, 4, 8, 16)
NSA_KV_GROUPS = 4
NSA_CMP_LEN = 32
NSA_CMP_STRIDE = 16
NSA_SLC_LEN = 64
NSA_SLC_TOPK = 16
NSA_WINDOW = 512
CONV_WIDTH = 3

MASKED_LOGIT = -1e30
SOFTMAX_EXP2_SCALE = HEAD_DIM ** -0.5 * 1.4426950408889634
KV_CHUNK = 256
MIB = 1024 * 1024


def _params(semantics, vmem_mib):
    return pltpu.CompilerParams(dimension_semantics=semantics,
                                vmem_limit_bytes=vmem_mib * MIB)


def _resident(shape, index_map):
    return pl.BlockSpec(shape, index_map, pipeline_mode=pl.Buffered(1))


def _rms(x, gain):
    ms = jnp.mean(x * x, axis=-1, keepdims=True)
    return x * lax.rsqrt(ms + NORM_EPS) * gain


def _split_bf16(x):
    hi = x.astype(BF16)
    return hi, (x - hi.astype(F32)).astype(BF16)


def _rope(x, cos, sin):
    half = ROT_DIM // 2
    lane = lax.broadcasted_iota(jnp.int32, x.shape, 1)
    partner = jnp.where(lane < half, -pltpu.roll(x, HEAD_DIM - half, 1), pltpu.roll(x, half, 1))
    return x * cos + partner * sin


def _nt_dot(a, b):
    return lax.dot_general(a, b, (((1,), (1,)), ((), ())), preferred_element_type=F32)


def _rope_table_body(pos_ref, freq_ref, cos_ref, sin_ref):
    ang = pos_ref[...].astype(F32) * freq_ref[...]
    cos_ref[...] = jnp.cos(ang)
    sin_ref[...] = jnp.sin(ang)


def rope_tables(positions):
    m = positions.size
    half = ROT_DIM // 2
    freqs = jnp.float32(ROPE_THETA) ** (-jnp.arange(half, dtype=F32) * 2.0 / ROT_DIM)
    freq_row = jnp.concatenate([freqs, freqs, jnp.zeros((HEAD_DIM - ROT_DIM,), F32)])[None]
    tm = min(m, 1024)
    tab = pl.BlockSpec((tm, HEAD_DIM), lambda i: (i, 0))
    return pl.pallas_call(
        _rope_table_body,
        grid=(m // tm,),
        in_specs=[pl.BlockSpec((tm, 1), lambda i: (i, 0)), pl.BlockSpec((1, HEAD_DIM), lambda i: (0, 0))],
        out_specs=[tab, tab],
        out_shape=[jax.ShapeDtypeStruct((m, HEAD_DIM), F32)] * 2,
        compiler_params=_params(("parallel",), 32),
        name="rope_tables",
    )(positions.reshape(m, 1), freq_row)


def _nmm_body(*refs, n_w, n_extra, n_out, sub, variants, on_row_start):
    x_ref, g_ref = refs[0], refs[1]
    w_refs = refs[2:2 + n_w]
    extra = refs[2 + n_w:2 + n_w + n_extra]
    outs = refs[2 + n_w + n_extra:2 + n_w + n_extra + n_out]
    xn_ref = refs[2 + n_w + n_extra + n_out]
    extra = extra + refs[3 + n_w + n_extra + n_out:]
    j = pl.program_id(1)

    @pl.when(j == 0)
    def _():
        xn_ref[...] = _rms(x_ref[...], g_ref[...]).astype(BF16)
        if on_row_start is not None:
            on_row_start(extra)

    width = w_refs[0].shape[1] // sub
    cols = [slice(s * width, (s + 1) * width) for s in range(sub)]

    def run(epilogue):
        xn = xn_ref[...]
        accs = [[jnp.dot(xn, w[:, c], preferred_element_type=F32) for w in w_refs] for c in cols]
        for c, acc in zip(cols, accs):
            epilogue(acc, c, extra, outs)

    if len(variants) == 1:
        run(variants[0][1])
    else:
        for applies, epilogue in variants:
            pl.when(applies(j))(functools.partial(run, epilogue))


def norm_matmul(h, gain, w, w_maps, n_col_tiles, tn, extra, outs, variants, sub=1, tm=1024, vmem_mib=56,
                scratch=(), on_row_start=None):
    m, d = h.shape
    tm = min(tm, m)
    in_specs = [pl.BlockSpec((tm, d), lambda i, j: (i, 0)),
                pl.BlockSpec((1, d), lambda i, j: (0, 0))]
    in_specs += [pl.BlockSpec((d, tn), wm) for wm in w_maps]
    in_specs += [spec for _, spec in extra]
    return pl.pallas_call(
        functools.partial(_nmm_body, n_w=len(w_maps), n_extra=len(extra), n_out=len(outs), sub=sub,
                          variants=variants, on_row_start=on_row_start),
        grid=(m // tm, n_col_tiles),
        in_specs=in_specs,
        out_specs=[spec for _, spec in outs],
        out_shape=[shape for shape, _ in outs],
        scratch_shapes=[pltpu.VMEM((tm, d), BF16)] + list(scratch),
        compiler_params=_params(("arbitrary" if scratch else "parallel", "arbitrary"), vmem_mib),
        name="norm_matmul",
    )(h, gain.reshape(1, d), *([w] * len(w_maps)), *[a for a, _ in extra])


def _head_norm_rope_store(acc, cols, gain_ref, table_refs, out_plain, out_rot):
    width = acc.shape[1]
    same_head = (lax.broadcasted_iota(jnp.int32, (width, width), 0) // HEAD_DIM
                 == lax.broadcasted_iota(jnp.int32, (width, width), 1) // HEAD_DIM)
    ones = jnp.where(same_head, 1.0, 0.0).astype(BF16)
    sq_hi, sq_lo = _split_bf16(acc * acc)
    ssq = (jnp.dot(sq_hi, ones, preferred_element_type=F32)
           + jnp.dot(sq_lo, ones, preferred_element_type=F32))
    normed = acc * lax.rsqrt(ssq * (1.0 / HEAD_DIM) + NORM_EPS) * gain_ref[:, cols]
    if out_plain is not None:
        out_plain[:, cols] = normed.astype(BF16)
    if out_rot is not None:
        cos, sin = table_refs[0][...], table_refs[1][...]
        for hh in range(width // HEAD_DIM):
            sl = slice(hh * HEAD_DIM, (hh + 1) * HEAD_DIM)
            dst = slice(cols.start + hh * HEAD_DIM, cols.start + (hh + 1) * HEAD_DIM)
            out_rot[:, dst] = _rope(normed[:, sl], cos, sin).astype(BF16)


def _omm_body(*refs, n_in, prologue):
    ins = refs[:n_in]
    w_ref, res_ref, o_ref = refs[n_in:n_in + 3]
    a = prologue(ins)
    o_ref[...] = res_ref[...] + jnp.dot(a, w_ref[...], preferred_element_type=F32)


def out_matmul(ins, prologue, w, res, tm=512, vmem_mib=48):
    m, d = res.shape
    tm = min(tm, m)
    k = w.shape[0]
    return pl.pallas_call(
        functools.partial(_omm_body, n_in=len(ins), prologue=prologue),
        grid=(m // tm,),
        in_specs=[spec for _, spec in ins] + [
            _resident((k, d), lambda i: (0, 0)),
            pl.BlockSpec((tm, d), lambda i: (i, 0))],
        out_specs=pl.BlockSpec((tm, d), lambda i: (i, 0)),
        out_shape=jax.ShapeDtypeStruct((m, d), F32),
        compiler_params=_params(("parallel",), vmem_mib),
        name="out_matmul",
    )(*[a for a, _ in ins], w, res)


def _mlp_body(x_ref, g_ref, w1_ref, w2_ref, o_ref, xn_ref):
    @pl.when(pl.program_id(1) == 0)
    def _():
        x = x_ref[...]
        xn_ref[...] = _rms(x, g_ref[...]).astype(BF16)
        o_ref[...] = x

    a = jnp.dot(xn_ref[...], w1_ref[...].astype(BF16), preferred_element_type=F32)
    a = jnp.square(jnp.maximum(a, 0.0)).astype(BF16)
    o_ref[...] += jnp.dot(a, w2_ref[...].astype(BF16), preferred_element_type=F32)


def mlp_layer(h, gain, w1, w2, layer, tm=1024, tf=512):
    m, d = h.shape
    dff = w1.shape[2]
    tm = min(tm, m)
    return pl.pallas_call(
        _mlp_body,
        grid=(m // tm, dff // tf),
        in_specs=[_resident((tm, d), lambda i, f: (i, 0)),
                  pl.BlockSpec((1, d), lambda i, f: (0, 0)),
                  pl.BlockSpec((None, d, tf), lambda i, f: (layer, 0, f)),
                  pl.BlockSpec((None, tf, d), lambda i, f: (layer, f, 0))],
        out_specs=pl.BlockSpec((tm, d), lambda i, f: (i, 0)),
        out_shape=jax.ShapeDtypeStruct((m, d), F32),
        scratch_shapes=[pltpu.VMEM((tm, d), BF16)],
        compiler_params=_params(("parallel", "arbitrary"), 56),
        name="mlp",
    )(h, gain.reshape(1, d), w1, w2)


def _ple_body(x_ref, g_ref, wg_ref, p_ref, wp_ref, o_ref):
    x = x_ref[...]
    xn = _rms(x, g_ref[...]).astype(BF16)
    gate = jax.nn.sigmoid(jnp.dot(xn, wg_ref[...], preferred_element_type=F32))
    emb = jnp.dot(p_ref[...].astype(BF16), wp_ref[...], preferred_element_type=F32)
    o_ref[...] = x + gate * emb


def ple_layer(h, gain, wg, p, wp, layer, tm=512):
    m, d = h.shape
    pd = p.shape[2]
    tm = min(tm, m)
    return pl.pallas_call(
        _ple_body,
        grid=(m // tm,),
        in_specs=[pl.BlockSpec((tm, d), lambda i: (i, 0)),
                  pl.BlockSpec((1, d), lambda i: (0, 0)),
                  _resident((None, d, d), lambda i: (layer, 0, 0)),
                  pl.BlockSpec((None, tm, pd), lambda i: (layer, i, 0)),
                  _resident((None, pd, d), lambda i: (layer, 0, 0))],
        out_specs=pl.BlockSpec((tm, d), lambda i: (i, 0)),
        out_shape=jax.ShapeDtypeStruct((m, d), F32),
        compiler_params=_params(("parallel",), 48),
        name="ple",
    )(h, gain.reshape(1, d), wg, p, wp)


def _flash_buffers(chains, keys, queries):
    return [pltpu.VMEM((chains, keys, queries), F32), pltpu.VMEM((chains, keys, queries), BF16),
            pltpu.VMEM((chains, HEAD_DIM, queries), F32)]


def _flash_logits(buf, k, q_t, bias):
    s = jnp.dot(k, q_t, preferred_element_type=F32) + bias
    buf[0][...] = s
    return jnp.max(s, axis=0, keepdims=True)


def _flash_start(buf, k, q_t, bias):
    s_ref, p_ref, acc_ref = buf
    p_ref[...] = jnp.zeros_like(p_ref)
    acc_ref[...] = jnp.zeros_like(acc_ref)
    queries = s_ref.shape[1]
    return (_flash_logits(buf, k, q_t, bias), jnp.ones((1, queries), F32),
            jnp.full((1, queries), MASKED_LOGIT, F32), jnp.zeros((1, queries), F32))


def _flash_stage(bufs, states, v_prev, following):
    products = [jnp.dot(v, buf[1][...], preferred_element_type=F32) for v, buf in zip(v_prev, bufs)]
    if following is not None:
        upcoming = [jnp.dot(k, q_t, preferred_element_type=F32) for k, q_t, _ in following]
    out = []
    for c, ((s_ref, p_ref, acc_ref), (s_max, alpha_prev, m, l), pv) in enumerate(zip(bufs, states, products)):
        m_new = jnp.maximum(m, s_max)
        p = jnp.exp2((s_ref[...] - m_new) * SOFTMAX_EXP2_SCALE)
        alpha = jnp.exp2((m - m_new) * SOFTMAX_EXP2_SCALE)
        l = alpha * l + jnp.sum(p, axis=0, keepdims=True)
        p_ref[...] = p.astype(BF16)
        acc_ref[...] = alpha_prev * acc_ref[...] + pv
        if following is not None:
            s_next = upcoming[c] + following[c][2]
            s_ref[...] = s_next
            s_max = jnp.max(s_next, axis=0, keepdims=True)
        out.append((s_max, alpha, m_new, l))
    return tuple(out)


def _flash_finish(buf, state, v_last):
    _, p_ref, acc_ref = buf
    _, alpha, _, l = state
    return (alpha * acc_ref[...] + jnp.dot(v_last, p_ref[...], preferred_element_type=F32)) / l


def _store_transposed(v_ref, vt_ref, chunk):
    for n in range(vt_ref.shape[0]):
        vt_ref[n] = v_ref[n * chunk:(n + 1) * chunk, :].astype(F32).T.astype(BF16)


def _top_rank(v, row):
    n = v.shape[0]
    if n % 8:
        rank = jnp.zeros(v.shape, jnp.int32)
        for j in range(n):
            r = v[j:j + 1, :]
            rank = rank + jnp.where(row > j, jnp.where(r >= v, 1, 0), jnp.where(r > v, 1, 0))
        return rank
    starts = range(0, n, 8)
    groups = [v[lo:lo + 8, :] for lo in starts]
    ranks = [jnp.zeros((8, v.shape[1]), jnp.int32) for _ in starts]
    for j in range(n):
        r = jnp.broadcast_to(v[j:j + 1, :], (8, v.shape[1]))
        for g, lo in enumerate(starts):
            if lo > j:
                ranks[g] = ranks[g] + jnp.where(r >= groups[g], 1, 0)
            elif lo + 7 <= j:
                ranks[g] = ranks[g] + jnp.where(r > groups[g], 1, 0)
            else:
                below = lax.broadcasted_iota(jnp.int32, (8, v.shape[1]), 0) > j - lo
                ranks[g] = ranks[g] + jnp.where(below, jnp.where(r >= groups[g], 1, 0),
                                                jnp.where(r > groups[g], 1, 0))
    return jnp.concatenate(ranks, axis=0)


def _moba_body(q_ref, k_ref, v_ref, o_ref, kmean_ref, vt_ref, bias_ref, s_ref, p_ref, acc_ref,
               *, n_blocks, heads):
    blk = MOBA_BLOCK
    own = pl.program_id(2)
    cols = [slice(e * HEAD_DIM, (e + 1) * HEAD_DIM) for e in range(heads)]
    bufs = [(s_ref.at[e], p_ref.at[e], acc_ref.at[e]) for e in range(heads)]

    @pl.when(own == 0)
    def _():
        for e in range(heads):
            for n in range(n_blocks):
                kb = k_ref[n * blk:(n + 1) * blk, cols[e]].astype(F32)
                kmean_ref[e, n:n + 1, :] = jnp.mean(kb, axis=0, keepdims=True)
                vt_ref[e, n] = v_ref[n * blk:(n + 1) * blk, cols[e]].astype(F32).T.astype(BF16)

    q_ts = [q_ref[:, cols[e]].astype(F32).T.astype(BF16) for e in range(heads)]
    for e in range(heads):
        gate = jnp.dot(kmean_ref[e].astype(BF16), q_ts[e], preferred_element_type=F32)
        row = lax.broadcasted_iota(jnp.int32, gate.shape, 0)
        gate = jnp.where(row < own, gate, -jnp.inf)
        rank = _top_rank(gate, row)
        bias_ref[e] = jnp.where(row < own, jnp.where(rank < MOBA_TOPK, 0.0, MASKED_LOGIT), MASKED_LOGIT)

    def keys(j):
        start = pl.multiple_of(j * blk, blk)
        return [k_ref[pl.ds(start, blk), cols[e]] for e in range(heads)]

    def values(j):
        return [vt_ref[e, j] for e in range(heads)]

    def stage(j, states, next_biases):
        following = None if next_biases is None else list(zip(keys(j + 1), q_ts, next_biases))
        return _flash_stage(bufs, states, values(jnp.maximum(j - 1, 0)), following)

    def chosen_bias(j):
        return [bias_ref[e, pl.ds(j, 1), :] for e in range(heads)]

    causal = jnp.where(lax.broadcasted_iota(jnp.int32, (blk, blk), 0)
                       <= lax.broadcasted_iota(jnp.int32, (blk, blk), 1), 0.0, MASKED_LOGIT)
    first_bias = [jnp.where(own == 0, causal, b) for b in chosen_bias(0)]
    states = tuple(_flash_start(buf, k, q_t, b) for buf, k, q_t, b in zip(bufs, keys(0), q_ts, first_bias))
    states = lax.fori_loop(0, own - 1, lambda j, st: stage(j, st, chosen_bias(j + 1)), states)
    states = lax.cond(own > 0, lambda st: stage(own - 1, st, [causal] * heads), lambda st: st, states)
    states = stage(own, states, None)
    for e, v_last in enumerate(values(own)):
        o_ref[:, cols[e]] = _flash_finish(bufs[e], states[e], v_last).T.astype(BF16)


def moba_attention(qkv, batch, seq, n_heads, heads_per_step=4):
    blk = MOBA_BLOCK
    nq = seq // blk
    hp = heads_per_step
    width = hp * HEAD_DIM
    groups = n_heads // hp
    return pl.pallas_call(
        functools.partial(_moba_body, n_blocks=nq, heads=hp),
        grid=(batch, groups, nq),
        in_specs=[pl.BlockSpec((blk, width), lambda b, h, i: (b * nq + i, h)),
                  pl.BlockSpec((seq, width), lambda b, h, i: (b, groups + h)),
                  pl.BlockSpec((seq, width), lambda b, h, i: (b, 2 * groups + h))],
        out_specs=pl.BlockSpec((blk, width), lambda b, h, i: (b * nq + i, h)),
        out_shape=jax.ShapeDtypeStruct((batch * seq, n_heads * HEAD_DIM), BF16),
        scratch_shapes=[pltpu.VMEM((hp, nq, HEAD_DIM), F32),
                        pltpu.VMEM((hp, nq, HEAD_DIM, blk), BF16),
                        pltpu.VMEM((hp, nq, blk), F32)] + _flash_buffers(hp, blk, blk),
        compiler_params=_params(("parallel", "parallel", "arbitrary"), 32),
        name="moba_attention",
    )(qkv, qkv, qkv)


def moba_layer(h, gain, w_qkv, q_gain, k_gain, w_o, tables, batch, seq):
    m, d = h.shape
    n_heads = d // HEAD_DIM
    tn = 1024
    n_qk_tiles = 2 * d // tn
    gain_row = jnp.concatenate([jnp.tile(q_gain, n_heads), jnp.tile(k_gain, n_heads),
                                jnp.ones((d,), F32)])[None]

    def qk_epilogue(accs, cols, extra, outs):
        _head_norm_rope_store(accs[0], cols, extra[0], extra[1:], None, outs[0])

    def v_epilogue(accs, cols, extra, outs):
        outs[0][:, cols] = accs[0].astype(BF16)

    tm = min(1024, m)
    tab = pl.BlockSpec((tm, HEAD_DIM), lambda i, j: (i, 0))
    (qkv,) = norm_matmul(
        h, gain, w_qkv, [lambda i, j: (0, j)], 3 * d // tn, tn,
        extra=[(gain_row, pl.BlockSpec((1, tn), lambda i, j: (0, j)))] + [(t, tab) for t in tables],
        outs=[(jax.ShapeDtypeStruct((m, 3 * d), BF16), pl.BlockSpec((tm, tn), lambda i, j: (i, j)))],
        variants=[(lambda j: j < n_qk_tiles, qk_epilogue), (lambda j: j >= n_qk_tiles, v_epilogue)],
        sub=4, tm=tm)
    o = moba_attention(qkv, batch, seq, n_heads)
    tmo = min(512, m)
    return out_matmul([(o, pl.BlockSpec((tmo, d), lambda i: (i, 0)))],
                      lambda ins: ins[0][...], w_o, h, tm=tmo)


def _pool_body(x_ref, halo_ref, g_ref, w_ref, s_ref, o_ref, *, tiles_per_seq, halo):
    i = pl.program_id(0)
    tm, d = x_ref.shape
    group = d // len(POOL_WINDOWS)
    x = x_ref[...]
    gain = g_ref[...]
    xn = _rms(x, gain)
    prev = jnp.where(i % tiles_per_seq == 0, 0.0, _rms(halo_ref[...], gain))
    pos = (i % tiles_per_seq) * tm + lax.broadcasted_iota(jnp.int32, (tm, 1), 0)
    for g, win in enumerate(POOL_WINDOWS):
        sl = slice(g * group, (g + 1) * group)
        run = jnp.concatenate([prev[:, sl], xn[:, sl]], axis=0)
        span = 1
        while span < win:
            run = run + pltpu.roll(run, span, 0)
            span *= 2
        cnt = jnp.minimum(pos + 1, win).astype(F32)
        mean = run[halo:, :] / cnt
        mix = jnp.dot((mean - xn[:, sl]).astype(BF16), w_ref[g], preferred_element_type=F32)
        o_ref[:, sl] = x[:, sl] + mix * s_ref[:, sl]


def pool_layer(h, gain, w_groups, scale, seq, tm=512):
    m, d = h.shape
    halo = 16
    assert max(POOL_WINDOWS) <= halo
    tm = min(tm, seq)
    group = d // len(POOL_WINDOWS)
    return pl.pallas_call(
        functools.partial(_pool_body, tiles_per_seq=seq // tm, halo=halo),
        grid=(m // tm,),
        in_specs=[pl.BlockSpec((tm, d), lambda i: (i, 0)),
                  pl.BlockSpec((halo, d), lambda i: (jnp.maximum(i * (tm // halo) - 1, 0), 0)),
                  pl.BlockSpec((1, d), lambda i: (0, 0)),
                  _resident((len(POOL_WINDOWS), group, group), lambda i: (0, 0, 0)),
                  pl.BlockSpec((1, d), lambda i: (0, 0))],
        out_specs=pl.BlockSpec((tm, d), lambda i: (i, 0)),
        out_shape=jax.ShapeDtypeStruct((m, d), F32),
        compiler_params=_params(("parallel",), 48),
        name="pool_mixer",
    )(h, h, gain.reshape(1, d), w_groups, scale.reshape(1, d))


def conv_layer(h, gain, w_in, conv_w, conv_b, w_o, seq):
    m, d = h.shape
    tn = 512
    nj = d // tn

    tm = min(1024, seq)
    tiles_per_seq = seq // tm
    halo = 8

    def on_row_start(extra):
        @pl.when(pl.program_id(0) % tiles_per_seq == 0)
        def _():
            extra[2][...] = jnp.zeros_like(extra[2])

    def epilogue(accs, cols, extra, outs):
        cw_ref, cb_ref, carry_ref = extra
        j = pl.program_id(1)
        u0 = accs[1] * accs[2]
        prev = carry_ref[j, :, cols]
        row = lax.broadcasted_iota(jnp.int32, u0.shape, 0)
        u1 = jnp.where(row == 0, prev[halo - 1:halo, :], pltpu.roll(u0, 1, 0))
        u2 = jnp.where(row == 0, prev[halo - 2:halo - 1, :],
                       jnp.where(row == 1, prev[halo - 1:halo, :], pltpu.roll(u0, 2, 0)))
        conv = (cw_ref[0:1, cols] * u2 + cw_ref[1:2, cols] * u1 + cw_ref[2:3, cols] * u0
                + cb_ref[:, cols])
        outs[0][:, cols] = (accs[0] * conv).astype(BF16)
        carry_ref[j, :, cols] = u0[tm - halo:, :]

    (y,) = norm_matmul(
        h, gain, w_in, [lambda i, j: (0, j), lambda i, j: (0, j + nj), lambda i, j: (0, j + 2 * nj)],
        nj, tn,
        extra=[(conv_w, pl.BlockSpec((CONV_WIDTH, tn), lambda i, j: (0, j))),
               (conv_b.reshape(1, d), pl.BlockSpec((1, tn), lambda i, j: (0, j)))],
        outs=[(jax.ShapeDtypeStruct((m, d), BF16), pl.BlockSpec((tm, tn), lambda i, j: (i, j)))],
        variants=[(None, epilogue)], sub=2, tm=tm,
        scratch=[pltpu.VMEM((nj, halo, tn), F32)], on_row_start=on_row_start)

    tmo = min(512, m)
    return out_matmul([(y, pl.BlockSpec((tmo, d), lambda i: (i, 0)))],
                      lambda ins: ins[0][...], w_o, h, tm=tmo)


def _compress_body(x_ref, pos_ref, w1_ref, w2_ref, g_ref, kc_ref, vc_ref, lo_ref, hi_ref, *, groups):
    step = pl.program_id(1)
    stride = NSA_CMP_STRIDE

    @pl.when(step == 0)
    def _():
        lo_ref[...] = jnp.zeros_like(lo_ref)
        hi_ref[...] = jnp.zeros_like(hi_ref)

    for kv in range(2):
        p_lo = pos_ref[kv, pl.ds(step, 1), :]
        p_hi = pos_ref[kv, pl.ds(stride + step, 1), :]
        w_lo = w1_ref[kv, step]
        w_hi = w1_ref[kv, stride + step]
        for g in range(groups):
            c = kv * groups + g
            t = x_ref[:, c * HEAD_DIM:(c + 1) * HEAD_DIM]
            lo_ref[c] += jnp.dot((t + p_lo).astype(BF16), w_lo, preferred_element_type=F32)
            hi_ref[c] += jnp.dot((t + p_hi).astype(BF16), w_hi, preferred_element_type=F32)

    @pl.when(step == stride - 1)
    def _():
        n_half = lo_ref.shape[1]
        for kv in range(2):
            for g in range(groups):
                c = kv * groups + g
                pre = lo_ref[c] + pltpu.roll(hi_ref[c], n_half - 1, 0)
                hid = jax.nn.gelu(pre)
                if kv == 0:
                    out = jnp.dot(hid.astype(BF16), w2_ref[0], preferred_element_type=F32)
                    kc_ref[g] = _rms(out, g_ref[...]).astype(BF16)
                else:
                    vc_ref[g] = jnp.dot(w2_ref[1], hid.T.astype(BF16),
                                        preferred_element_type=F32).astype(BF16)


def nsa_compress(raw, cmp_pos, cmp_w1, cmp_w2, k_gain0, batch, seq):
    assert NSA_CMP_LEN == 2 * NSA_CMP_STRIDE
    stride = NSA_CMP_STRIDE
    groups = NSA_KV_GROUPS
    width = raw.shape[1]
    n_half = seq // stride
    x = raw.reshape(batch * n_half, stride * width)
    w1 = cmp_w1.reshape(2, NSA_CMP_LEN, HEAD_DIM, HEAD_DIM)
    cmp_w2 = jnp.stack([cmp_w2[0], cmp_w2[1].T])
    k_out = jax.ShapeDtypeStruct((batch, groups, n_half, HEAD_DIM), BF16)
    v_out = jax.ShapeDtypeStruct((batch, groups, HEAD_DIM, n_half), BF16)
    k_spec = pl.BlockSpec((None, groups, n_half, HEAD_DIM), lambda b, s: (b, 0, 0, 0))
    v_spec = pl.BlockSpec((None, groups, HEAD_DIM, n_half), lambda b, s: (b, 0, 0, 0))
    return pl.pallas_call(
        functools.partial(_compress_body, groups=groups),
        grid=(batch, stride),
        in_specs=[pl.BlockSpec((n_half, width), lambda b, s: (b, s)),
                  pl.BlockSpec((2, NSA_CMP_LEN, HEAD_DIM), lambda b, s: (0, 0, 0)),
                  pl.BlockSpec((2, NSA_CMP_LEN, HEAD_DIM, HEAD_DIM), lambda b, s: (0, 0, 0, 0)),
                  pl.BlockSpec((2, HEAD_DIM, HEAD_DIM), lambda b, s: (0, 0, 0)),
                  pl.BlockSpec((1, HEAD_DIM), lambda b, s: (0, 0))],
        out_specs=[k_spec, v_spec],
        out_shape=[k_out, v_out],
        scratch_shapes=[pltpu.VMEM((2 * groups, n_half, HEAD_DIM), F32)] * 2,
        compiler_params=_params(("parallel", "arbitrary"), 32),
        name="nsa_compress",
    )(x, cmp_pos, w1, cmp_w2, k_gain0.reshape(1, HEAD_DIM))


def _cmp_select_body(q_ref, kc_ref, vct_ref, gate_ref, o_ref, sel_ref, *, n_cmp, n_top, q_per_kv):
    tq = q_ref.shape[0]
    first_head = pl.program_id(1) * q_per_kv
    n_pad = kc_ref.shape[0]
    n_slc = sel_ref.shape[0]
    scale = HEAD_DIM ** -0.5
    t = pl.program_id(2) * tq + lax.broadcasted_iota(jnp.int32, (1, tq), 1)
    n = lax.broadcasted_iota(jnp.int32, (n_pad, 1), 0)
    ok = (n < n_cmp) & (n * NSA_CMP_STRIDE + (NSA_CMP_LEN - 1) <= t)
    kc = kc_ref[...]
    vct = vct_ref[...]
    p_sum = jnp.zeros((n_pad, tq), F32)
    for r in range(q_per_kv):
        sl = slice(r * HEAD_DIM, (r + 1) * HEAD_DIM)
        s = jnp.where(ok, _nt_dot(kc, q_ref[:, sl]) * scale, -jnp.inf)
        mx = jnp.max(s, axis=0, keepdims=True)
        mx = jnp.where(mx > -jnp.inf, mx, 0.0)
        e = jnp.where(ok, jnp.exp(s - mx), 0.0)
        den = jnp.maximum(jnp.sum(e, axis=0, keepdims=True), jnp.finfo(F32).tiny)
        o_t = jnp.dot(vct, e.astype(BF16), preferred_element_type=F32) / den
        gate = gate_ref[pl.ds(3 * (first_head + r), 1), :]
        o_ref[:, sl] = (o_t * gate).T.astype(BF16)
        p_sum = p_sum + e / den

    jj = lax.broadcasted_iota(jnp.int32, (n_slc, n_pad), 0) * NSA_SLC_LEN
    nn = lax.broadcasted_iota(jnp.int32, (n_slc, n_pad), 1) * NSA_CMP_STRIDE
    overlap = jnp.where((nn < jj + NSA_SLC_LEN) & (jj < nn + NSA_CMP_LEN)
                        & (nn < n_cmp * NSA_CMP_STRIDE), 1.0, 0.0).astype(BF16)
    p_hi = p_sum.astype(BF16)
    p_lo = (p_sum - p_hi.astype(F32)).astype(BF16)
    imp = (jnp.dot(overlap, p_hi, preferred_element_type=F32)
           + jnp.dot(overlap, p_lo, preferred_element_type=F32))

    cur = t // NSA_SLC_LEN
    jb = lax.broadcasted_iota(jnp.int32, (n_slc, tq), 0)
    val = jnp.where(jb == cur, jnp.inf,
                    jnp.where(jb == 0, jnp.inf, jnp.where(jb < cur, imp, -jnp.inf)))
    rank = _top_rank(val, jb)
    sel_ref[...] = jnp.where(rank < n_top, jnp.where(val > -jnp.inf, 1.0, 0.0), 0.0)


def nsa_cmp_select(q_cmp, k_cmp, v_cmp_t, gates_t, batch, seq, tq=256):
    m, d = q_cmp.shape
    groups = NSA_KV_GROUPS
    q_per_kv = d // HEAD_DIM // groups
    gw = q_per_kv * HEAD_DIM
    tq = min(tq, seq)
    nq = seq // tq
    n_pad = k_cmp.shape[2]
    n_cmp = (seq - NSA_CMP_LEN) // NSA_CMP_STRIDE + 1
    n_slc = seq // NSA_SLC_LEN
    return pl.pallas_call(
        functools.partial(_cmp_select_body, n_cmp=n_cmp, n_top=min(NSA_SLC_TOPK, n_slc),
                          q_per_kv=q_per_kv),
        grid=(batch, groups, nq),
        in_specs=[pl.BlockSpec((tq, gw), lambda b, g, i: (b * nq + i, g)),
                  pl.BlockSpec((None, None, n_pad, HEAD_DIM), lambda b, g, i: (b, g, 0, 0)),
                  pl.BlockSpec((None, None, HEAD_DIM, n_pad), lambda b, g, i: (b, g, 0, 0)),
                  pl.BlockSpec((gates_t.shape[0], tq), lambda b, g, i: (0, b * nq + i))],
        out_specs=[pl.BlockSpec((tq, gw), lambda b, g, i: (b * nq + i, g)),
                   pl.BlockSpec((None, None, n_slc, tq), lambda b, g, i: (b, g, 0, i))],
        out_shape=[jax.ShapeDtypeStruct((m, d), BF16),
                   jax.ShapeDtypeStruct((batch, groups, n_slc, seq), F32)],
        compiler_params=_params(("parallel", "parallel", "parallel"), 32),
        name="nsa_cmp_select",
    )(q_cmp, k_cmp, v_cmp_t, gates_t)


def _nsa_attn_body(q_ref, ks_ref, vs_ref, kw_ref, vw_ref, sel_ref, gate_ref, oc_ref, o_ref,
                   vst_ref, vwt_ref, bias_ref, s_ref, p_ref, acc_ref, *, q_per_kv, per):
    tq = q_ref.shape[0]
    kc = KV_CHUNK
    sub = kc // NSA_SLC_LEN
    step = pl.program_id(2)
    q0 = step * tq

    @pl.when(step == 0)
    def _():
        _store_transposed(vs_ref, vst_ref, kc)
        _store_transposed(vw_ref, vwt_ref, kc)

    n_chain = q_per_kv // per
    q_ts = [jnp.concatenate([q_ref[:, r * HEAD_DIM:(r + 1) * HEAD_DIM]
                             for r in range(c * per, (c + 1) * per)], axis=0).astype(F32).T.astype(BF16)
            for c in range(n_chain)]
    t = q0 + lax.broadcasted_iota(jnp.int32, (1, tq), 1)
    bias_ref[...] = jnp.where(sel_ref[...] > 0.5, 0.0, MASKED_LOGIT)

    def chain_bias(bias):
        return jnp.concatenate([bias] * per, axis=1)

    def selected_bias(c, kpos):
        picked = jnp.concatenate(
            [jnp.broadcast_to(bias_ref[pl.ds(c * sub + a, 1), :], (NSA_SLC_LEN, tq)) for a in range(sub)],
            axis=0)
        return chain_bias(jnp.where(kpos <= t, picked, MASKED_LOGIT))

    def window_bias(kpos):
        return chain_bias(jnp.where(kpos <= t, jnp.where(t - kpos < NSA_WINDOW, 0.0, MASKED_LOGIT),
                                    MASKED_LOGIT))

    def selected_logits(c):
        start = pl.multiple_of(c * kc, kc)
        kpos = start + lax.broadcasted_iota(jnp.int32, (kc, 1), 0)
        k, bias = ks_ref[pl.ds(start, kc), :], selected_bias(c, kpos)
        return [(k, q_t, bias) for q_t in q_ts]

    def window_logits(c):
        start = pl.multiple_of(c * kc, kc)
        kpos = start + lax.broadcasted_iota(jnp.int32, (kc, 1), 0)
        k, bias = kw_ref[pl.ds(start, kc), :], window_bias(kpos)
        return [(k, q_t, bias) for q_t in q_ts]

    bufs = [(s_ref.at[c], p_ref.at[c], acc_ref.at[c]) for c in range(2 * n_chain)]

    def selected_only(c, states):
        return _flash_stage(bufs[:n_chain], states, [vst_ref[jnp.maximum(c - 1, 0)]] * n_chain,
                            selected_logits(c + 1))

    def selected_and_window(c, states, is_last=False):
        prev = jnp.maximum(c - 1, 0)
        following = None if is_last else selected_logits(c + 1) + window_logits(c + 1)
        return _flash_stage(bufs, states, [vst_ref[prev]] * n_chain + [vwt_ref[prev]] * n_chain, following)

    first = jnp.maximum(q0 - (NSA_WINDOW - 1), 0) // kc
    last = (q0 + tq - 1) // kc
    states = tuple(_flash_start(buf, *f) for buf, f in zip(bufs[:n_chain], selected_logits(0)))
    states = lax.fori_loop(0, first, selected_only, states)
    states = states + tuple(_flash_start(buf, *f) for buf, f in zip(bufs[n_chain:], window_logits(first)))
    states = lax.fori_loop(first, last, selected_and_window, states)
    states = selected_and_window(last, states, is_last=True)

    first_head = pl.program_id(1) * q_per_kv
    for c in range(n_chain):
        selected = _flash_finish(bufs[c], states[c], vst_ref[last])
        window = _flash_finish(bufs[n_chain + c], states[n_chain + c], vwt_ref[last])
        for i in range(per):
            r = c * per + i
            lanes = slice(i * tq, (i + 1) * tq)
            cols = slice(r * HEAD_DIM, (r + 1) * HEAD_DIM)
            g_slc = gate_ref[pl.ds(3 * (first_head + r) + 1, 1), :]
            g_win = gate_ref[pl.ds(3 * (first_head + r) + 2, 1), :]
            mixed = (selected[:, lanes] * g_slc + window[:, lanes] * g_win).T
            o_ref[:, cols] = (mixed + oc_ref[:, cols].astype(F32)).astype(BF16)


def nsa_attention(q_rot, kv, sel, gates_t, o_cmp, batch, seq, tq=256):
    m, d = q_rot.shape
    groups = NSA_KV_GROUPS
    q_per_kv = d // HEAD_DIM // groups
    gw = q_per_kv * HEAD_DIM
    tq = min(tq, seq)
    nq = seq // tq
    n_slc = sel.shape[2]
    assert seq % KV_CHUNK == 0 and KV_CHUNK % NSA_SLC_LEN == 0

    def kv_spec(part):
        return pl.BlockSpec((seq, HEAD_DIM), lambda b, g, i: (b, part * groups + g))

    q_spec = pl.BlockSpec((tq, gw), lambda b, g, i: (b * nq + i, g))
    out = jax.ShapeDtypeStruct((m, d), BF16)
    v_t = pltpu.VMEM((seq // KV_CHUNK, HEAD_DIM, KV_CHUNK), BF16)
    per = 1
    return pl.pallas_call(
        functools.partial(_nsa_attn_body, q_per_kv=q_per_kv, per=per),
        grid=(batch, groups, nq),
        in_specs=[q_spec, kv_spec(0), kv_spec(1), kv_spec(2), kv_spec(3),
                  pl.BlockSpec((None, None, n_slc, tq), lambda b, g, i: (b, g, 0, i)),
                  pl.BlockSpec((gates_t.shape[0], tq), lambda b, g, i: (0, b * nq + i)),
                  q_spec],
        out_specs=q_spec,
        out_shape=out,
        scratch_shapes=[v_t, v_t, pltpu.VMEM((n_slc, tq), F32)]
        + _flash_buffers(2 * q_per_kv // per, KV_CHUNK, per * tq),
        compiler_params=_params(("parallel", "parallel", "arbitrary"), 32),
        name="nsa_attention",
    )(q_rot, kv, kv, kv, kv, sel, gates_t, o_cmp)


def nsa_layer(h, gain, w_proj, q_gain, k_gain, cmp_pos, cmp_w1, cmp_w2, w_o, tables, batch, seq):
    m, d = h.shape
    n_heads = d // HEAD_DIM
    groups = NSA_KV_GROUPS
    gwk = groups * HEAD_DIM
    tn = 512
    tm = min(1024, m)
    assert tn == gwk and 3 * n_heads <= HEAD_DIM
    raw0, kv0, gate0 = d // tn, d // tn + 2, d // tn + 6
    ones = jnp.ones((gwk,), F32)
    gain_row = jnp.concatenate([jnp.tile(q_gain, n_heads), ones, ones, jnp.tile(k_gain[1], groups), ones,
                                jnp.tile(k_gain[2], groups), ones, ones])[None]

    def q_epilogue(accs, cols, extra, outs):
        _head_norm_rope_store(accs[0], cols, extra[0], extra[1:], outs[0], outs[1])

    def raw_epilogue(accs, cols, extra, outs):
        outs[2][:, cols] = accs[0]

    def k_epilogue(accs, cols, extra, outs):
        _head_norm_rope_store(accs[0], cols, extra[0], extra[1:], None, outs[3])

    def v_epilogue(accs, cols, extra, outs):
        outs[3][:, cols] = accs[0].astype(BF16)

    def gate_epilogue(accs, cols, extra, outs):
        if cols.start == 0:
            outs[4][...] = jax.nn.sigmoid(accs[0][:, :HEAD_DIM]).T

    def block(first, count):
        return pl.BlockSpec((tm, tn), lambda i, j: (i, jnp.clip(j - first, 0, count - 1)))

    tab = pl.BlockSpec((tm, HEAD_DIM), lambda i, j: (i, 0))
    q_shape = jax.ShapeDtypeStruct((m, d), BF16)
    q_cmp, q_rot, raw, kv, gates_t = norm_matmul(
        h, gain, w_proj, [lambda i, j: (0, j)], gate0 + 1, tn,
        extra=[(gain_row, pl.BlockSpec((1, tn), lambda i, j: (0, j)))] + [(t, tab) for t in tables],
        outs=[(q_shape, block(0, raw0)), (q_shape, block(0, raw0)),
              (jax.ShapeDtypeStruct((m, 2 * gwk), F32), block(raw0, 2)),
              (jax.ShapeDtypeStruct((m, 4 * gwk), BF16), block(kv0, 4)),
              (jax.ShapeDtypeStruct((HEAD_DIM, m), F32), pl.BlockSpec((HEAD_DIM, tm), lambda i, j: (0, i)))],
        variants=[(lambda j: j < raw0, q_epilogue),
                  (lambda j: (j >= raw0) & (j < kv0), raw_epilogue),
                  (lambda j: (j >= kv0) & (j < gate0) & ((j - kv0) % 2 == 0), k_epilogue),
                  (lambda j: (j >= kv0) & (j < gate0) & ((j - kv0) % 2 == 1), v_epilogue),
                  (lambda j: j == gate0, gate_epilogue)],
        sub=2, tm=tm)

    k_cmp, v_cmp = nsa_compress(raw, cmp_pos, cmp_w1, cmp_w2, k_gain[0], batch, seq)
    o_cmp, sel = nsa_cmp_select(q_cmp, k_cmp, v_cmp, gates_t, batch, seq)
    o = nsa_attention(q_rot, kv, sel, gates_t, o_cmp, batch, seq)
    tmo = min(512, m)
    return out_matmul([(o, pl.BlockSpec((tmo, d), lambda i: (i, 0)))],
                      lambda ins: ins[0][...], w_o, h, tm=tmo)


def kernel(x, p, positions, mixer_norm, mlp_norm, mlp_w1, mlp_w2, ple_norm, ple_gate, ple_proj,
           moba_w_qkv, moba_q_gain, moba_k_gain, moba_w_o, pool_w, pool_scale,
           nsa_w_q, nsa_w_kv, nsa_q_gain, nsa_k_gain, nsa_cmp_pos, nsa_cmp_w1, nsa_cmp_w2,
           nsa_w_gate, nsa_w_o, conv_w_in, conv_w, conv_b, conv_w_o):
    batch, seq, d = x.shape
    depth = p.shape[0]
    m = batch * seq
    n_heads = d // HEAD_DIM
    gwk = NSA_KV_GROUPS * HEAD_DIM
    tables = rope_tables(positions)
    bf = lambda w: w.astype(BF16)
    ple_gate_bf, ple_proj_bf = bf(ple_gate), bf(ple_proj)
    p_rows = p.reshape(depth, m, -1)

    h = x.reshape(m, d)
    for i in range(depth):
        kind, j = i % 4, i // 4
        if kind == 0:
            h = moba_layer(h, mixer_norm[i], bf(moba_w_qkv[j]), moba_q_gain[j], moba_k_gain[j],
                           bf(moba_w_o[j]), tables, batch, seq)
        elif kind == 1:
            h = pool_layer(h, mixer_norm[i], bf(pool_w[j]), pool_scale[j], seq)
        elif kind == 2:
            w_gate = jnp.pad(nsa_w_gate[j], ((0, 0), (0, gwk - 3 * n_heads)))
            w_proj = bf(jnp.concatenate([nsa_w_q[j], nsa_w_kv[j], w_gate], axis=1))
            h = nsa_layer(h, mixer_norm[i], w_proj, nsa_q_gain[j], nsa_k_gain[j], nsa_cmp_pos[j],
                          bf(nsa_cmp_w1[j]), bf(nsa_cmp_w2[j]), bf(nsa_w_o[j]), tables, batch, seq)
        else:
            h = conv_layer(h, mixer_norm[i], bf(conv_w_in[j]), conv_w[j], conv_b[j], bf(conv_w_o[j]), seq)
        h = mlp_layer(h, mlp_norm[i], mlp_w1, mlp_w2, i)
        h = ple_layer(h, ple_norm[i], ple_gate_bf, p_rows, ple_proj_bf, i)
    return h.reshape(batch, seq, d)
```

```python
import functools

import jax
import jax.numpy as jnp
from jax import lax
from jax.experimental import pallas as pl
from jax.experimental.pallas import tpu as pltpu

F32 = jnp.float32
BF16 = jnp.bfloat16

HEAD_DIM = 128
ROT_DIM = HEAD_DIM // 4
ROPE_THETA = 500000.0
NORM_EPS = 1e-6
MOBA_BLOCK = 256
MOBA_TOPK = 3
POOL_WINDOWS = (2, 4, 8, 16)
NSA_KV_GROUPS = 4
NSA_CMP_LEN = 32
NSA_CMP_STRIDE = 16
NSA_SLC_LEN = 64
NSA_SLC_TOPK = 16
NSA_WINDOW = 512
CONV_WIDTH = 3

MASKED_LOGIT = -1e30
LN2 = 0.6931471805599453
SOFTMAX_EXP2_SCALE = HEAD_DIM ** -0.5 / LN2
KV_CHUNK = 256
MIB = 1024 * 1024


def _params(semantics, vmem_mib):
    return pltpu.CompilerParams(dimension_semantics=semantics,
                                vmem_limit_bytes=vmem_mib * MIB)


def _resident(shape, index_map):
    return pl.BlockSpec(shape, index_map, pipeline_mode=pl.Buffered(1))


def _rms(x, gain):
    ms = jnp.mean(x * x, axis=-1, keepdims=True)
    return x * lax.rsqrt(ms + NORM_EPS) * gain


def _split_bf16(x):
    hi = x.astype(BF16)
    return hi, (x - hi.astype(F32)).astype(BF16)


def _rope(x, cos, sin):
    half = ROT_DIM // 2
    lane = lax.broadcasted_iota(jnp.int32, x.shape, 1)
    partner = jnp.where(lane < half, -pltpu.roll(x, HEAD_DIM - half, 1), pltpu.roll(x, half, 1))
    return x * cos + partner * sin


def _nt_dot(a, b):
    return lax.dot_general(a, b, (((1,), (1,)), ((), ())), preferred_element_type=F32)


def _rope_table_body(pos_ref, freq_ref, cos_ref, sin_ref):
    ang = pos_ref[...].astype(F32) * freq_ref[...]
    cos_ref[...] = jnp.cos(ang)
    sin_ref[...] = jnp.sin(ang)


def rope_tables(positions):
    m = positions.size
    half = ROT_DIM // 2
    freqs = jnp.float32(ROPE_THETA) ** (-jnp.arange(half, dtype=F32) * 2.0 / ROT_DIM)
    freq_row = jnp.concatenate([freqs, freqs, jnp.zeros((HEAD_DIM - ROT_DIM,), F32)])[None]
    tm = min(m, 1024)
    tab = pl.BlockSpec((tm, HEAD_DIM), lambda i: (i, 0))
    return pl.pallas_call(
        _rope_table_body,
        grid=(m // tm,),
        in_specs=[pl.BlockSpec((tm, 1), lambda i: (i, 0)), pl.BlockSpec((1, HEAD_DIM), lambda i: (0, 0))],
        out_specs=[tab, tab],
        out_shape=[jax.ShapeDtypeStruct((m, HEAD_DIM), F32)] * 2,
        compiler_params=_params(("parallel",), 32),
        name="rope_tables",
    )(positions.reshape(m, 1), freq_row)


def _nmm_body(*refs, n_w, n_extra, n_out, sub, variants, on_row_start):
    x_ref, g_ref = refs[0], refs[1]
    w_refs = refs[2:2 + n_w]
    extra = refs[2 + n_w:2 + n_w + n_extra]
    outs = refs[2 + n_w + n_extra:2 + n_w + n_extra + n_out]
    xn_ref = refs[2 + n_w + n_extra + n_out]
    extra = extra + refs[3 + n_w + n_extra + n_out:]
    j = pl.program_id(1)

    @pl.when(j == 0)
    def _():
        xn_ref[...] = _rms(x_ref[...], g_ref[...]).astype(BF16)
        if on_row_start is not None:
            on_row_start(extra)

    width = w_refs[0].shape[1] // sub
    cols = [slice(s * width, (s + 1) * width) for s in range(sub)]

    def run(epilogue):
        xn = xn_ref[...]
        accs = [[jnp.dot(xn, w[:, c], preferred_element_type=F32) for w in w_refs] for c in cols]
        for c, acc in zip(cols, accs):
            epilogue(acc, c, extra, outs)

    if len(variants) == 1:
        run(variants[0][1])
    else:
        for applies, epilogue in variants:
            pl.when(applies(j))(functools.partial(run, epilogue))


def norm_matmul(h, gain, w, w_maps, n_col_tiles, tn, extra, outs, variants, sub=1, tm=1024, vmem_mib=56,
                scratch=(), on_row_start=None):
    m, d = h.shape
    tm = min(tm, m)
    in_specs = [pl.BlockSpec((tm, d), lambda i, j: (i, 0)),
                pl.BlockSpec((1, d), lambda i, j: (0, 0))]
    in_specs += [pl.BlockSpec((d, tn), wm) for wm in w_maps]
    in_specs += [spec for _, spec in extra]
    return pl.pallas_call(
        functools.partial(_nmm_body, n_w=len(w_maps), n_extra=len(extra), n_out=len(outs), sub=sub,
                          variants=variants, on_row_start=on_row_start),
        grid=(m // tm, n_col_tiles),
        in_specs=in_specs,
        out_specs=[spec for _, spec in outs],
        out_shape=[shape for shape, _ in outs],
        scratch_shapes=[pltpu.VMEM((tm, d), BF16)] + list(scratch),
        compiler_params=_params(("arbitrary" if scratch else "parallel", "arbitrary"), vmem_mib),
        name="norm_matmul",
    )(h, gain.reshape(1, d), *([w] * len(w_maps)), *[a for a, _ in extra])


def _head_norm_rope_store(acc, cols, gain_ref, table_refs, out_plain, out_rot):
    width = acc.shape[1]
    same_head = (lax.broadcasted_iota(jnp.int32, (width, width), 0) // HEAD_DIM
                 == lax.broadcasted_iota(jnp.int32, (width, width), 1) // HEAD_DIM)
    ones = jnp.where(same_head, 1.0, 0.0).astype(BF16)
    sq_hi, sq_lo = _split_bf16(acc * acc)
    ssq = (jnp.dot(sq_hi, ones, preferred_element_type=F32)
           + jnp.dot(sq_lo, ones, preferred_element_type=F32))
    normed = acc * lax.rsqrt(ssq * (1.0 / HEAD_DIM) + NORM_EPS) * gain_ref[:, cols]
    if out_plain is not None:
        out_plain[:, cols] = normed.astype(BF16)
    if out_rot is not None:
        cos, sin = table_refs[0][...], table_refs[1][...]
        for hh in range(width // HEAD_DIM):
            sl = slice(hh * HEAD_DIM, (hh + 1) * HEAD_DIM)
            dst = slice(cols.start + hh * HEAD_DIM, cols.start + (hh + 1) * HEAD_DIM)
            out_rot[:, dst] = _rope(normed[:, sl], cos, sin).astype(BF16)


def _omm_body(*refs, n_in, prologue):
    ins = refs[:n_in]
    w_ref, res_ref, o_ref = refs[n_in:n_in + 3]
    a = prologue(ins)
    o_ref[...] = res_ref[...] + jnp.dot(a, w_ref[...], preferred_element_type=F32)


def out_matmul(ins, prologue, w, res, tm=512, vmem_mib=48):
    m, d = res.shape
    tm = min(tm, m)
    k = w.shape[0]
    return pl.pallas_call(
        functools.partial(_omm_body, n_in=len(ins), prologue=prologue),
        grid=(m // tm,),
        in_specs=[spec for _, spec in ins] + [
            _resident((k, d), lambda i: (0, 0)),
            pl.BlockSpec((tm, d), lambda i: (i, 0))],
        out_specs=pl.BlockSpec((tm, d), lambda i: (i, 0)),
        out_shape=jax.ShapeDtypeStruct((m, d), F32),
        compiler_params=_params(("parallel",), vmem_mib),
        name="out_matmul",
    )(*[a for a, _ in ins], w, res)


def _mlp_body(x_ref, g_ref, w1_ref, w2_ref, o_ref, xn_ref):
    @pl.when(pl.program_id(1) == 0)
    def _():
        x = x_ref[...]
        xn_ref[...] = _rms(x, g_ref[...]).astype(BF16)
        o_ref[...] = x

    a = jnp.dot(xn_ref[...], w1_ref[...].astype(BF16), preferred_element_type=F32)
    a = jnp.square(jnp.maximum(a, 0.0)).astype(BF16)
    o_ref[...] += jnp.dot(a, w2_ref[...].astype(BF16), preferred_element_type=F32)


def mlp_layer(h, gain, w1, w2, layer, tm=1024, tf=512):
    m, d = h.shape
    dff = w1.shape[2]
    tm = min(tm, m)
    return pl.pallas_call(
        _mlp_body,
        grid=(m // tm, dff // tf),
        in_specs=[_resident((tm, d), lambda i, f: (i, 0)),
                  pl.BlockSpec((1, d), lambda i, f: (0, 0)),
                  pl.BlockSpec((None, d, tf), lambda i, f: (layer, 0, f)),
                  pl.BlockSpec((None, tf, d), lambda i, f: (layer, f, 0))],
        out_specs=pl.BlockSpec((tm, d), lambda i, f: (i, 0)),
        out_shape=jax.ShapeDtypeStruct((m, d), F32),
        scratch_shapes=[pltpu.VMEM((tm, d), BF16)],
        compiler_params=_params(("parallel", "arbitrary"), 56),
        name="mlp",
    )(h, gain.reshape(1, d), w1, w2)


def _ple_body(x_ref, g_ref, wg_ref, p_ref, wp_ref, o_ref):
    x = x_ref[...]
    xn = _rms(x, g_ref[...]).astype(BF16)
    gate = jax.nn.sigmoid(jnp.dot(xn, wg_ref[...], preferred_element_type=F32))
    emb = jnp.dot(p_ref[...].astype(BF16), wp_ref[...], preferred_element_type=F32)
    o_ref[...] = x + gate * emb


def ple_layer(h, gain, wg, p, wp, layer, tm=512):
    m, d = h.shape
    pd = p.shape[2]
    tm = min(tm, m)
    return pl.pallas_call(
        _ple_body,
        grid=(m // tm,),
        in_specs=[pl.BlockSpec((tm, d), lambda i: (i, 0)),
                  pl.BlockSpec((1, d), lambda i: (0, 0)),
                  _resident((None, d, d), lambda i: (layer, 0, 0)),
                  pl.BlockSpec((None, tm, pd), lambda i: (layer, i, 0)),
                  _resident((None, pd, d), lambda i: (layer, 0, 0))],
        out_specs=pl.BlockSpec((tm, d), lambda i: (i, 0)),
        out_shape=jax.ShapeDtypeStruct((m, d), F32),
        compiler_params=_params(("parallel",), 48),
        name="ple",
    )(h, gain.reshape(1, d), wg, p, wp)


def _flash_buffers(chains, keys, queries):
    return [pltpu.VMEM((chains, keys, queries), F32), pltpu.VMEM((chains, keys, queries), BF16),
            pltpu.VMEM((chains, HEAD_DIM, queries), F32)]


def _flash_keep(s_ref, s, bias):
    if bias.shape[0] == 1:
        s_ref[...] = s
        raw_max = jnp.max(s, axis=0, keepdims=True)
        return raw_max + bias, raw_max - 2.0 * bias
    s = s + bias
    s_ref[...] = s
    s_max = jnp.max(s, axis=0, keepdims=True)
    return s_max, s_max


def _flash_start(buf, k, q_t, bias):
    s_ref, p_ref, acc_ref = buf
    p_ref[...] = jnp.zeros_like(p_ref)
    acc_ref[...] = jnp.zeros_like(acc_ref)
    queries = s_ref.shape[1]
    return _flash_keep(s_ref, jnp.dot(k, q_t, preferred_element_type=F32), bias) + (
        jnp.ones((1, queries), F32), jnp.full((1, queries), MASKED_LOGIT, F32), jnp.zeros((1, queries), F32))


def _flash_stage(bufs, states, v_prev, following):
    products = [jnp.dot(v, buf[1][...], preferred_element_type=F32) for v, buf in zip(v_prev, bufs)]
    if following is not None:
        upcoming = [jnp.dot(k, q_t, preferred_element_type=F32) for k, q_t, _ in following]
    out = []
    for c, ((s_ref, p_ref, acc_ref), state, pv) in enumerate(zip(bufs, states, products)):
        s_max, floor, alpha_prev, m, l = state
        m_new = jnp.maximum(m, s_max)
        p = jnp.exp2(s_ref[...] - jnp.maximum(m_new, floor))
        alpha = jnp.exp((m - m_new) * LN2)
        l = alpha * l + jnp.sum(p, axis=0, keepdims=True)
        p_ref[...] = p.astype(BF16)
        acc_ref[...] = alpha_prev * acc_ref[...] + pv
        kept = (s_max, floor) if following is None else _flash_keep(s_ref, upcoming[c], following[c][2])
        out.append(kept + (alpha, m_new, l))
    return tuple(out)


def _flash_finish(buf, state, v_last):
    _, p_ref, acc_ref = buf
    alpha, l = state[2], state[4]
    return (alpha * acc_ref[...] + jnp.dot(v_last, p_ref[...], preferred_element_type=F32)) / l


def _store_transposed(v_ref, vt_ref, chunk):
    for n in range(vt_ref.shape[0]):
        vt_ref[n] = v_ref[n * chunk:(n + 1) * chunk, :].astype(F32).T.astype(BF16)


def _top_rank(v, row):
    n = v.shape[0]
    if n % 8:
        rank = jnp.zeros(v.shape, jnp.int32)
        for j in range(n):
            r = v[j:j + 1, :]
            rank = rank + jnp.where(row > j, jnp.where(r >= v, 1, 0), jnp.where(r > v, 1, 0))
        return rank
    starts = range(0, n, 8)
    groups = [v[lo:lo + 8, :] for lo in starts]
    ranks = [jnp.zeros((8, v.shape[1]), jnp.int32) for _ in starts]
    for j in range(n):
        r = jnp.broadcast_to(v[j:j + 1, :], (8, v.shape[1]))
        for g, lo in enumerate(starts):
            if lo > j:
                ranks[g] = ranks[g] + jnp.where(r >= groups[g], 1, 0)
            elif lo + 7 <= j:
                ranks[g] = ranks[g] + jnp.where(r > groups[g], 1, 0)
            else:
                below = lax.broadcasted_iota(jnp.int32, (8, v.shape[1]), 0) > j - lo
                ranks[g] = ranks[g] + jnp.where(below, jnp.where(r >= groups[g], 1, 0),
                                                jnp.where(r > groups[g], 1, 0))
    return jnp.concatenate(ranks, axis=0)


def _moba_body(q_ref, k_ref, v_ref, o_ref, kmean_ref, vt_ref, bias_ref, s_ref, p_ref, acc_ref,
               *, n_blocks, heads):
    blk = MOBA_BLOCK
    own = pl.program_id(2)
    cols = [slice(e * HEAD_DIM, (e + 1) * HEAD_DIM) for e in range(heads)]
    bufs = [(s_ref.at[e], p_ref.at[e], acc_ref.at[e]) for e in range(heads)]

    @pl.when(own == 0)
    def _():
        for e in range(heads):
            for n in range(n_blocks):
                kb = k_ref[n * blk:(n + 1) * blk, cols[e]].astype(F32)
                kmean_ref[e, n:n + 1, :] = jnp.mean(kb, axis=0, keepdims=True)
                vt_ref[e, n] = v_ref[n * blk:(n + 1) * blk, cols[e]].astype(F32).T.astype(BF16)

    q_ts = [q_ref[:, cols[e]].astype(F32).T.astype(BF16) for e in range(heads)]
    for e in range(heads):
        gate = jnp.dot(kmean_ref[e].astype(BF16), q_ts[e], preferred_element_type=F32)
        row = lax.broadcasted_iota(jnp.int32, gate.shape, 0)
        gate = jnp.where(row < own, gate, -jnp.inf)
        rank = _top_rank(gate, row)
        bias = jnp.where(row < own, jnp.where(rank < MOBA_TOPK, 0.0, MASKED_LOGIT), MASKED_LOGIT)
        for n in range(n_blocks):
            bias_ref[e, n] = jnp.broadcast_to(bias[n:n + 1, :], (8, blk))

    def keys(j):
        start = pl.multiple_of(j * blk, blk)
        return [k_ref[pl.ds(start, blk), cols[e]] for e in range(heads)]

    def values(j):
        return [vt_ref[e, j] for e in range(heads)]

    def stage(j, states, next_biases):
        following = None if next_biases is None else list(zip(keys(j + 1), q_ts, next_biases))
        return _flash_stage(bufs, states, values(jnp.maximum(j - 1, 0)), following)

    def chosen_bias(j):
        return [jnp.max(bias_ref[e, j], axis=0, keepdims=True) for e in range(heads)]

    causal = jnp.where(lax.broadcasted_iota(jnp.int32, (blk, blk), 0)
                       <= lax.broadcasted_iota(jnp.int32, (blk, blk), 1), 0.0, MASKED_LOGIT)
    first_bias = [jnp.where(own == 0, causal, b) for b in chosen_bias(0)]
    states = tuple(_flash_start(buf, k, q_t, b) for buf, k, q_t, b in zip(bufs, keys(0), q_ts, first_bias))
    states = lax.fori_loop(0, own - 1, lambda j, st: stage(j, st, chosen_bias(j + 1)), states)
    states = lax.cond(own > 0, lambda st: stage(own - 1, st, [causal] * heads), lambda st: st, states)
    states = stage(own, states, None)
    for e, v_last in enumerate(values(own)):
        o_ref[:, cols[e]] = _flash_finish(bufs[e], states[e], v_last).T.astype(BF16)


def moba_attention(qkv, batch, seq, n_heads, heads_per_step=4):
    blk = MOBA_BLOCK
    nq = seq // blk
    hp = heads_per_step
    width = hp * HEAD_DIM
    groups = n_heads // hp
    return pl.pallas_call(
        functools.partial(_moba_body, n_blocks=nq, heads=hp),
        grid=(batch, groups, nq),
        in_specs=[pl.BlockSpec((blk, width), lambda b, h, i: (b * nq + i, h)),
                  pl.BlockSpec((seq, width), lambda b, h, i: (b, groups + h)),
                  pl.BlockSpec((seq, width), lambda b, h, i: (b, 2 * groups + h))],
        out_specs=pl.BlockSpec((blk, width), lambda b, h, i: (b * nq + i, h)),
        out_shape=jax.ShapeDtypeStruct((batch * seq, n_heads * HEAD_DIM), BF16),
        scratch_shapes=[pltpu.VMEM((hp, nq, HEAD_DIM), F32),
                        pltpu.VMEM((hp, nq, HEAD_DIM, blk), BF16),
                        pltpu.VMEM((hp, nq, 8, blk), F32)] + _flash_buffers(hp, blk, blk),
        compiler_params=_params(("parallel", "parallel", "arbitrary"), 32),
        name="moba_attention",
    )(qkv, qkv, qkv)


def moba_layer(h, gain, w_qkv, q_gain, k_gain, w_o, tables, batch, seq):
    m, d = h.shape
    n_heads = d // HEAD_DIM
    tn = 1024
    n_qk_tiles = 2 * d // tn
    gain_row = jnp.concatenate([jnp.tile(q_gain, n_heads), jnp.tile(k_gain * SOFTMAX_EXP2_SCALE, n_heads),
                                jnp.ones((d,), F32)])[None]

    def qk_epilogue(accs, cols, extra, outs):
        _head_norm_rope_store(accs[0], cols, extra[0], extra[1:], None, outs[0])

    def v_epilogue(accs, cols, extra, outs):
        outs[0][:, cols] = accs[0].astype(BF16)

    tm = min(1024, m)
    tab = pl.BlockSpec((tm, HEAD_DIM), lambda i, j: (i, 0))
    (qkv,) = norm_matmul(
        h, gain, w_qkv, [lambda i, j: (0, j)], 3 * d // tn, tn,
        extra=[(gain_row, pl.BlockSpec((1, tn), lambda i, j: (0, j)))] + [(t, tab) for t in tables],
        outs=[(jax.ShapeDtypeStruct((m, 3 * d), BF16), pl.BlockSpec((tm, tn), lambda i, j: (i, j)))],
        variants=[(lambda j: j < n_qk_tiles, qk_epilogue), (lambda j: j >= n_qk_tiles, v_epilogue)],
        sub=4, tm=tm)
    o = moba_attention(qkv, batch, seq, n_heads)
    tmo = min(512, m)
    return out_matmul([(o, pl.BlockSpec((tmo, d), lambda i: (i, 0)))],
                      lambda ins: ins[0][...], w_o, h, tm=tmo)


def _pool_body(x_ref, halo_ref, g_ref, w_ref, s_ref, o_ref, *, tiles_per_seq, halo):
    i = pl.program_id(0)
    tm, d = x_ref.shape
    group = d // len(POOL_WINDOWS)
    x = x_ref[...]
    gain = g_ref[...]
    xn = _rms(x, gain)
    prev = jnp.where(i % tiles_per_seq == 0, 0.0, _rms(halo_ref[...], gain))
    pos = (i % tiles_per_seq) * tm + lax.broadcasted_iota(jnp.int32, (tm, 1), 0)
    for g, win in enumerate(POOL_WINDOWS):
        sl = slice(g * group, (g + 1) * group)
        run = jnp.concatenate([prev[:, sl], xn[:, sl]], axis=0)
        span = 1
        while span < win:
            run = run + pltpu.roll(run, span, 0)
            span *= 2
        cnt = jnp.minimum(pos + 1, win).astype(F32)
        mean = run[halo:, :] / cnt
        mix = jnp.dot((mean - xn[:, sl]).astype(BF16), w_ref[g], preferred_element_type=F32)
        o_ref[:, sl] = x[:, sl] + mix * s_ref[:, sl]


def pool_layer(h, gain, w_groups, scale, seq, tm=512):
    m, d = h.shape
    halo = 16
    assert max(POOL_WINDOWS) <= halo
    tm = min(tm, seq)
    group = d // len(POOL_WINDOWS)
    return pl.pallas_call(
        functools.partial(_pool_body, tiles_per_seq=seq // tm, halo=halo),
        grid=(m // tm,),
        in_specs=[pl.BlockSpec((tm, d), lambda i: (i, 0)),
                  pl.BlockSpec((halo, d), lambda i: (jnp.maximum(i * (tm // halo) - 1, 0), 0)),
                  pl.BlockSpec((1, d), lambda i: (0, 0)),
                  _resident((len(POOL_WINDOWS), group, group), lambda i: (0, 0, 0)),
                  pl.BlockSpec((1, d), lambda i: (0, 0))],
        out_specs=pl.BlockSpec((tm, d), lambda i: (i, 0)),
        out_shape=jax.ShapeDtypeStruct((m, d), F32),
        compiler_params=_params(("parallel",), 48),
        name="pool_mixer",
    )(h, h, gain.reshape(1, d), w_groups, scale.reshape(1, d))


def conv_layer(h, gain, w_in, conv_w, conv_b, w_o, seq):
    m, d = h.shape
    tn = 512
    nj = d // tn

    tm = min(1024, seq)
    tiles_per_seq = seq // tm
    halo = 8

    def on_row_start(extra):
        @pl.when(pl.program_id(0) % tiles_per_seq == 0)
        def _():
            extra[2][...] = jnp.zeros_like(extra[2])

    def epilogue(accs, cols, extra, outs):
        cw_ref, cb_ref, carry_ref = extra
        j = pl.program_id(1)
        u0 = accs[1] * accs[2]
        prev = carry_ref[j, :, cols]
        row = lax.broadcasted_iota(jnp.int32, u0.shape, 0)
        u1 = jnp.where(row == 0, prev[halo - 1:halo, :], pltpu.roll(u0, 1, 0))
        u2 = jnp.where(row == 0, prev[halo - 2:halo - 1, :],
                       jnp.where(row == 1, prev[halo - 1:halo, :], pltpu.roll(u0, 2, 0)))
        conv = (cw_ref[0:1, cols] * u2 + cw_ref[1:2, cols] * u1 + cw_ref[2:3, cols] * u0
                + cb_ref[:, cols])
        outs[0][:, cols] = (accs[0] * conv).astype(BF16)
        carry_ref[j, :, cols] = u0[tm - halo:, :]

    (y,) = norm_matmul(
        h, gain, w_in, [lambda i, j: (0, j), lambda i, j: (0, j + nj), lambda i, j: (0, j + 2 * nj)],
        nj, tn,
        extra=[(conv_w, pl.BlockSpec((CONV_WIDTH, tn), lambda i, j: (0, j))),
               (conv_b.reshape(1, d), pl.BlockSpec((1, tn), lambda i, j: (0, j)))],
        outs=[(jax.ShapeDtypeStruct((m, d), BF16), pl.BlockSpec((tm, tn), lambda i, j: (i, j)))],
        variants=[(None, epilogue)], sub=2, tm=tm,
        scratch=[pltpu.VMEM((nj, halo, tn), F32)], on_row_start=on_row_start)

    tmo = min(512, m)
    return out_matmul([(y, pl.BlockSpec((tmo, d), lambda i: (i, 0)))],
                      lambda ins: ins[0][...], w_o, h, tm=tmo)


def _compress_body(x_ref, pos_ref, w1_ref, w2_ref, g_ref, kc_ref, vc_ref, lo_ref, hi_ref, *, groups):
    step = pl.program_id(1)
    stride = NSA_CMP_STRIDE

    @pl.when(step == 0)
    def _():
        lo_ref[...] = jnp.zeros_like(lo_ref)
        hi_ref[...] = jnp.zeros_like(hi_ref)

    for kv in range(2):
        p_lo = pos_ref[kv, pl.ds(step, 1), :]
        p_hi = pos_ref[kv, pl.ds(stride + step, 1), :]
        w_lo = w1_ref[kv, step]
        w_hi = w1_ref[kv, stride + step]
        for g in range(groups):
            c = kv * groups + g
            t = x_ref[:, c * HEAD_DIM:(c + 1) * HEAD_DIM]
            lo_ref[c] += jnp.dot((t + p_lo).astype(BF16), w_lo, preferred_element_type=F32)
            hi_ref[c] += jnp.dot((t + p_hi).astype(BF16), w_hi, preferred_element_type=F32)

    @pl.when(step == stride - 1)
    def _():
        n_half = lo_ref.shape[1]
        for kv in range(2):
            for g in range(groups):
                c = kv * groups + g
                pre = lo_ref[c] + pltpu.roll(hi_ref[c], n_half - 1, 0)
                hid = jax.nn.gelu(pre)
                if kv == 0:
                    out = jnp.dot(hid.astype(BF16), w2_ref[0], preferred_element_type=F32)
                    kc_ref[g] = _rms(out, g_ref[...]).astype(BF16)
                else:
                    vc_ref[g] = jnp.dot(w2_ref[1], hid.T.astype(BF16),
                                        preferred_element_type=F32).astype(BF16)


def nsa_compress(raw, cmp_pos, cmp_w1, cmp_w2, k_gain0, batch, seq):
    assert NSA_CMP_LEN == 2 * NSA_CMP_STRIDE
    stride = NSA_CMP_STRIDE
    groups = NSA_KV_GROUPS
    width = raw.shape[1]
    n_half = seq // stride
    x = raw.reshape(batch * n_half, stride * width)
    w1 = cmp_w1.reshape(2, NSA_CMP_LEN, HEAD_DIM, HEAD_DIM)
    cmp_w2 = jnp.stack([cmp_w2[0], cmp_w2[1].T])
    k_out = jax.ShapeDtypeStruct((batch, groups, n_half, HEAD_DIM), BF16)
    v_out = jax.ShapeDtypeStruct((batch, groups, HEAD_DIM, n_half), BF16)
    k_spec = pl.BlockSpec((None, groups, n_half, HEAD_DIM), lambda b, s: (b, 0, 0, 0))
    v_spec = pl.BlockSpec((None, groups, HEAD_DIM, n_half), lambda b, s: (b, 0, 0, 0))
    return pl.pallas_call(
        functools.partial(_compress_body, groups=groups),
        grid=(batch, stride),
        in_specs=[pl.BlockSpec((n_half, width), lambda b, s: (b, s)),
                  pl.BlockSpec((2, NSA_CMP_LEN, HEAD_DIM), lambda b, s: (0, 0, 0)),
                  pl.BlockSpec((2, NSA_CMP_LEN, HEAD_DIM, HEAD_DIM), lambda b, s: (0, 0, 0, 0)),
                  pl.BlockSpec((2, HEAD_DIM, HEAD_DIM), lambda b, s: (0, 0, 0)),
                  pl.BlockSpec((1, HEAD_DIM), lambda b, s: (0, 0))],
        out_specs=[k_spec, v_spec],
        out_shape=[k_out, v_out],
        scratch_shapes=[pltpu.VMEM((2 * groups, n_half, HEAD_DIM), F32)] * 2,
        compiler_params=_params(("parallel", "arbitrary"), 32),
        name="nsa_compress",
    )(x, cmp_pos, w1, cmp_w2, k_gain0.reshape(1, HEAD_DIM))


def _cmp_select_body(q_ref, kc_ref, vct_ref, gate_ref, o_ref, sel_ref, *, n_cmp, n_top, q_per_kv):
    tq = q_ref.shape[0]
    first_head = pl.program_id(1) * q_per_kv
    n_pad = kc_ref.shape[0]
    n_slc = sel_ref.shape[0]
    scale = HEAD_DIM ** -0.5
    t = pl.program_id(2) * tq + lax.broadcasted_iota(jnp.int32, (1, tq), 1)
    n = lax.broadcasted_iota(jnp.int32, (n_pad, 1), 0)
    ok = (n < n_cmp) & (n * NSA_CMP_STRIDE + (NSA_CMP_LEN - 1) <= t)
    kc = kc_ref[...]
    vct = vct_ref[...]
    p_sum = jnp.zeros((n_pad, tq), F32)
    for r in range(q_per_kv):
        sl = slice(r * HEAD_DIM, (r + 1) * HEAD_DIM)
        s = jnp.where(ok, _nt_dot(kc, q_ref[:, sl]) * scale, -jnp.inf)
        mx = jnp.max(s, axis=0, keepdims=True)
        mx = jnp.where(mx > -jnp.inf, mx, 0.0)
        e = jnp.where(ok, jnp.exp(s - mx), 0.0)
        den = jnp.maximum(jnp.sum(e, axis=0, keepdims=True), jnp.finfo(F32).tiny)
        o_t = jnp.dot(vct, e.astype(BF16), preferred_element_type=F32) / den
        gate = gate_ref[pl.ds(3 * (first_head + r), 1), :]
        o_ref[:, sl] = (o_t * gate).T.astype(BF16)
        p_sum = p_sum + e / den

    jj = lax.broadcasted_iota(jnp.int32, (n_slc, n_pad), 0) * NSA_SLC_LEN
    nn = lax.broadcasted_iota(jnp.int32, (n_slc, n_pad), 1) * NSA_CMP_STRIDE
    overlap = jnp.where((nn < jj + NSA_SLC_LEN) & (jj < nn + NSA_CMP_LEN)
                        & (nn < n_cmp * NSA_CMP_STRIDE), 1.0, 0.0).astype(BF16)
    p_hi = p_sum.astype(BF16)
    p_lo = (p_sum - p_hi.astype(F32)).astype(BF16)
    imp = (jnp.dot(overlap, p_hi, preferred_element_type=F32)
           + jnp.dot(overlap, p_lo, preferred_element_type=F32))

    cur = t // NSA_SLC_LEN
    jb = lax.broadcasted_iota(jnp.int32, (n_slc, tq), 0)
    val = jnp.where(jb == cur, jnp.inf,
                    jnp.where(jb == 0, jnp.inf, jnp.where(jb < cur, imp, -jnp.inf)))
    rank = _top_rank(val, jb)
    sel_ref[...] = jnp.where(rank < n_top, jnp.where(val > -jnp.inf, 1.0, 0.0), 0.0)


def nsa_cmp_select(q_cmp, k_cmp, v_cmp_t, gates_t, batch, seq, tq=256):
    m, d = q_cmp.shape
    groups = NSA_KV_GROUPS
    q_per_kv = d // HEAD_DIM // groups
    gw = q_per_kv * HEAD_DIM
    tq = min(tq, seq)
    nq = seq // tq
    n_pad = k_cmp.shape[2]
    n_cmp = (seq - NSA_CMP_LEN) // NSA_CMP_STRIDE + 1
    n_slc = seq // NSA_SLC_LEN
    return pl.pallas_call(
        functools.partial(_cmp_select_body, n_cmp=n_cmp, n_top=min(NSA_SLC_TOPK, n_slc),
                          q_per_kv=q_per_kv),
        grid=(batch, groups, nq),
        in_specs=[pl.BlockSpec((tq, gw), lambda b, g, i: (b * nq + i, g)),
                  pl.BlockSpec((None, None, n_pad, HEAD_DIM), lambda b, g, i: (b, g, 0, 0)),
                  pl.BlockSpec((None, None, HEAD_DIM, n_pad), lambda b, g, i: (b, g, 0, 0)),
                  pl.BlockSpec((gates_t.shape[0], tq), lambda b, g, i: (0, b * nq + i))],
        out_specs=[pl.BlockSpec((tq, gw), lambda b, g, i: (b * nq + i, g)),
                   pl.BlockSpec((None, None, n_slc, tq), lambda b, g, i: (b, g, 0, i))],
        out_shape=[jax.ShapeDtypeStruct((m, d), BF16),
                   jax.ShapeDtypeStruct((batch, groups, n_slc, seq), F32)],
        compiler_params=_params(("parallel", "parallel", "parallel"), 32),
        name="nsa_cmp_select",
    )(q_cmp, k_cmp, v_cmp_t, gates_t)


def _nsa_attn_body(q_ref, ks_ref, vs_ref, kw_ref, vw_ref, sel_ref, gate_ref, oc_ref, o_ref,
                   vst_ref, vwt_ref, bias_ref, s_ref, p_ref, acc_ref, *, q_per_kv, per):
    tq = q_ref.shape[0]
    kc = KV_CHUNK
    sub = kc // NSA_SLC_LEN
    step = pl.program_id(2)
    q0 = step * tq

    @pl.when(step == 0)
    def _():
        _store_transposed(vs_ref, vst_ref, kc)
        _store_transposed(vw_ref, vwt_ref, kc)

    n_chain = q_per_kv // per
    q_ts = [jnp.concatenate([q_ref[:, r * HEAD_DIM:(r + 1) * HEAD_DIM]
                             for r in range(c * per, (c + 1) * per)], axis=0).astype(F32).T.astype(BF16)
            for c in range(n_chain)]
    t = q0 + lax.broadcasted_iota(jnp.int32, (1, tq), 1)
    bias_ref[...] = jnp.where(sel_ref[...] > 0.5, 0.0, MASKED_LOGIT)

    def chain_bias(bias):
        return jnp.concatenate([bias] * per, axis=1)

    def selected_bias(c, kpos):
        picked = jnp.concatenate(
            [jnp.broadcast_to(bias_ref[pl.ds(c * sub + a, 1), :], (NSA_SLC_LEN, tq)) for a in range(sub)],
            axis=0)
        return chain_bias(jnp.where(kpos <= t, picked, MASKED_LOGIT))

    def window_bias(kpos):
        return chain_bias(jnp.where(kpos <= t, jnp.where(t - kpos < NSA_WINDOW, 0.0, MASKED_LOGIT),
                                    MASKED_LOGIT))

    def selected_logits(c):
        start = pl.multiple_of(c * kc, kc)
        kpos = start + lax.broadcasted_iota(jnp.int32, (kc, 1), 0)
        k, bias = ks_ref[pl.ds(start, kc), :], selected_bias(c, kpos)
        return [(k, q_t, bias) for q_t in q_ts]

    def window_logits(c):
        start = pl.multiple_of(c * kc, kc)
        kpos = start + lax.broadcasted_iota(jnp.int32, (kc, 1), 0)
        k, bias = kw_ref[pl.ds(start, kc), :], window_bias(kpos)
        return [(k, q_t, bias) for q_t in q_ts]

    bufs = [(s_ref.at[c], p_ref.at[c], acc_ref.at[c]) for c in range(2 * n_chain)]

    def selected_only(c, states):
        return _flash_stage(bufs[:n_chain], states, [vst_ref[jnp.maximum(c - 1, 0)]] * n_chain,
                            selected_logits(c + 1))

    def selected_and_window(c, states, is_last=False):
        prev = jnp.maximum(c - 1, 0)
        following = None if is_last else selected_logits(c + 1) + window_logits(c + 1)
        return _flash_stage(bufs, states, [vst_ref[prev]] * n_chain + [vwt_ref[prev]] * n_chain, following)

    first = jnp.maximum(q0 - (NSA_WINDOW - 1), 0) // kc
    last = (q0 + tq - 1) // kc
    states = tuple(_flash_start(buf, *f) for buf, f in zip(bufs[:n_chain], selected_logits(0)))
    states = lax.fori_loop(0, first, selected_only, states)
    states = states + tuple(_flash_start(buf, *f) for buf, f in zip(bufs[n_chain:], window_logits(first)))
    states = lax.fori_loop(first, last, selected_and_window, states)
    states = selected_and_window(last, states, is_last=True)

    first_head = pl.program_id(1) * q_per_kv
    for c in range(n_chain):
        selected = _flash_finish(bufs[c], states[c], vst_ref[last])
        window = _flash_finish(bufs[n_chain + c], states[n_chain + c], vwt_ref[last])
        for i in range(per):
            r = c * per + i
            lanes = slice(i * tq, (i + 1) * tq)
            cols = slice(r * HEAD_DIM, (r + 1) * HEAD_DIM)
            g_slc = gate_ref[pl.ds(3 * (first_head + r) + 1, 1), :]
            g_win = gate_ref[pl.ds(3 * (first_head + r) + 2, 1), :]
            mixed = (selected[:, lanes] * g_slc + window[:, lanes] * g_win).T
            o_ref[:, cols] = (mixed + oc_ref[:, cols].astype(F32)).astype(BF16)


def nsa_attention(q_rot, kv, sel, gates_t, o_cmp, batch, seq, tq=256):
    m, d = q_rot.shape
    groups = NSA_KV_GROUPS
    q_per_kv = d // HEAD_DIM // groups
    gw = q_per_kv * HEAD_DIM
    tq = min(tq, seq)
    nq = seq // tq
    n_slc = sel.shape[2]
    assert seq % KV_CHUNK == 0 and KV_CHUNK % NSA_SLC_LEN == 0

    def kv_spec(part):
        return pl.BlockSpec((seq, HEAD_DIM), lambda b, g, i: (b, part * groups + g))

    q_spec = pl.BlockSpec((tq, gw), lambda b, g, i: (b * nq + i, g))
    out = jax.ShapeDtypeStruct((m, d), BF16)
    v_t = pltpu.VMEM((seq // KV_CHUNK, HEAD_DIM, KV_CHUNK), BF16)
    per = 1
    return pl.pallas_call(
        functools.partial(_nsa_attn_body, q_per_kv=q_per_kv, per=per),
        grid=(batch, groups, nq),
        in_specs=[q_spec, kv_spec(0), kv_spec(1), kv_spec(2), kv_spec(3),
                  pl.BlockSpec((None, None, n_slc, tq), lambda b, g, i: (b, g, 0, i)),
                  pl.BlockSpec((gates_t.shape[0], tq), lambda b, g, i: (0, b * nq + i)),
                  q_spec],
        out_specs=q_spec,
        out_shape=out,
        scratch_shapes=[v_t, v_t, pltpu.VMEM((n_slc, tq), F32)]
        + _flash_buffers(2 * q_per_kv // per, KV_CHUNK, per * tq),
        compiler_params=_params(("parallel", "parallel", "arbitrary"), 32),
        name="nsa_attention",
    )(q_rot, kv, kv, kv, kv, sel, gates_t, o_cmp)


def nsa_layer(h, gain, w_proj, q_gain, k_gain, cmp_pos, cmp_w1, cmp_w2, w_o, tables, batch, seq):
    m, d = h.shape
    n_heads = d // HEAD_DIM
    groups = NSA_KV_GROUPS
    gwk = groups * HEAD_DIM
    tn = 512
    tm = min(1024, m)
    assert tn == gwk and 3 * n_heads <= HEAD_DIM
    raw0, kv0, gate0 = d // tn, d // tn + 2, d // tn + 6
    ones = jnp.ones((gwk,), F32)
    gain_row = jnp.concatenate([jnp.tile(q_gain, n_heads), ones, ones,
                                jnp.tile(k_gain[1] * SOFTMAX_EXP2_SCALE, groups), ones,
                                jnp.tile(k_gain[2] * SOFTMAX_EXP2_SCALE, groups), ones, ones])[None]

    def q_epilogue(accs, cols, extra, outs):
        _head_norm_rope_store(accs[0], cols, extra[0], extra[1:], outs[0], outs[1])

    def raw_epilogue(accs, cols, extra, outs):
        outs[2][:, cols] = accs[0]

    def k_epilogue(accs, cols, extra, outs):
        _head_norm_rope_store(accs[0], cols, extra[0], extra[1:], None, outs[3])

    def v_epilogue(accs, cols, extra, outs):
        outs[3][:, cols] = accs[0].astype(BF16)

    def gate_epilogue(accs, cols, extra, outs):
        if cols.start == 0:
            outs[4][...] = jax.nn.sigmoid(accs[0][:, :HEAD_DIM]).T

    def block(first, count):
        return pl.BlockSpec((tm, tn), lambda i, j: (i, jnp.clip(j - first, 0, count - 1)))

    tab = pl.BlockSpec((tm, HEAD_DIM), lambda i, j: (i, 0))
    q_shape = jax.ShapeDtypeStruct((m, d), BF16)
    q_cmp, q_rot, raw, kv, gates_t = norm_matmul(
        h, gain, w_proj, [lambda i, j: (0, j)], gate0 + 1, tn,
        extra=[(gain_row, pl.BlockSpec((1, tn), lambda i, j: (0, j)))] + [(t, tab) for t in tables],
        outs=[(q_shape, block(0, raw0)), (q_shape, block(0, raw0)),
              (jax.ShapeDtypeStruct((m, 2 * gwk), F32), block(raw0, 2)),
              (jax.ShapeDtypeStruct((m, 4 * gwk), BF16), block(kv0, 4)),
              (jax.ShapeDtypeStruct((HEAD_DIM, m), F32), pl.BlockSpec((HEAD_DIM, tm), lambda i, j: (0, i)))],
        variants=[(lambda j: j < raw0, q_epilogue),
                  (lambda j: (j >= raw0) & (j < kv0), raw_epilogue),
                  (lambda j: (j >= kv0) & (j < gate0) & ((j - kv0) % 2 == 0), k_epilogue),
                  (lambda j: (j >= kv0) & (j < gate0) & ((j - kv0) % 2 == 1), v_epilogue),
                  (lambda j: j == gate0, gate_epilogue)],
        sub=2, tm=tm)

    k_cmp, v_cmp = nsa_compress(raw, cmp_pos, cmp_w1, cmp_w2, k_gain[0], batch, seq)
    o_cmp, sel = nsa_cmp_select(q_cmp, k_cmp, v_cmp, gates_t, batch, seq)
    o = nsa_attention(q_rot, kv, sel, gates_t, o_cmp, batch, seq)
    tmo = min(512, m)
    return out_matmul([(o, pl.BlockSpec((tmo, d), lambda i: (i, 0)))],
                      lambda ins: ins[0][...], w_o, h, tm=tmo)


def kernel(x, p, positions, mixer_norm, mlp_norm, mlp_w1, mlp_w2, ple_norm, ple_gate, ple_proj,
           moba_w_qkv, moba_q_gain, moba_k_gain, moba_w_o, pool_w, pool_scale,
           nsa_w_q, nsa_w_kv, nsa_q_gain, nsa_k_gain, nsa_cmp_pos, nsa_cmp_w1, nsa_cmp_w2,
           nsa_w_gate, nsa_w_o, conv_w_in, conv_w, conv_b, conv_w_o):
    batch, seq, d = x.shape
    depth = p.shape[0]
    m = batch * seq
    n_heads = d // HEAD_DIM
    gwk = NSA_KV_GROUPS * HEAD_DIM
    tables = rope_tables(positions)
    bf = lambda w: w.astype(BF16)
    ple_gate_bf, ple_proj_bf = bf(ple_gate), bf(ple_proj)
    p_rows = p.reshape(depth, m, -1)

    h = x.reshape(m, d)
    for i in range(depth):
        kind, j = i % 4, i // 4
        if kind == 0:
            h = moba_layer(h, mixer_norm[i], bf(moba_w_qkv[j]), moba_q_gain[j], moba_k_gain[j],
                           bf(moba_w_o[j]), tables, batch, seq)
        elif kind == 1:
            h = pool_layer(h, mixer_norm[i], bf(pool_w[j]), pool_scale[j], seq)
        elif kind == 2:
            w_gate = jnp.pad(nsa_w_gate[j], ((0, 0), (0, gwk - 3 * n_heads)))
            w_proj = bf(jnp.concatenate([nsa_w_q[j], nsa_w_kv[j], w_gate], axis=1))
            h = nsa_layer(h, mixer_norm[i], w_proj, nsa_q_gain[j], nsa_k_gain[j], nsa_cmp_pos[j],
                          bf(nsa_cmp_w1[j]), bf(nsa_cmp_w2[j]), bf(nsa_w_o[j]), tables, batch, seq)
        else:
            h = conv_layer(h, mixer_norm[i], bf(conv_w_in[j]), conv_w[j], conv_b[j], bf(conv_w_o[j]), seq)
        h = mlp_layer(h, mlp_norm[i], mlp_w1, mlp_w2, i)
        h = ple_layer(h, ple_norm[i], ple_gate_bf, p_rows, ple_proj_bf, i)
    return h.reshape(batch, seq, d)
```

```python
import functools

import jax
import jax.numpy as jnp
from jax import lax
from jax.experimental import pallas as pl
from jax.experimental.pallas import tpu as pltpu

F32 = jnp.float32
BF16 = jnp.bfloat16

HEAD_DIM = 128
ROT_DIM = HEAD_DIM // 4
ROPE_THETA = 500000.0
NORM_EPS = 1e-6
MOBA_BLOCK = 256
MOBA_TOPK = 3
POOL_WINDOWS = (2, 4, 8, 16)
NSA_KV_GROUPS = 4
NSA_CMP_LEN = 32
NSA_CMP_STRIDE = 16
NSA_SLC_LEN = 64
NSA_SLC_TOPK = 16
NSA_WINDOW = 512
CONV_WIDTH = 3

MASKED_LOGIT = -1e30
LN2 = 0.6931471805599453
SOFTMAX_EXP2_SCALE = HEAD_DIM ** -0.5 / LN2
KV_CHUNK = 256
MIB = 1024 * 1024


def _params(semantics, vmem_mib):
    return pltpu.CompilerParams(dimension_semantics=semantics,
                                vmem_limit_bytes=vmem_mib * MIB)


def _resident(shape, index_map):
    return pl.BlockSpec(shape, index_map, pipeline_mode=pl.Buffered(1))


def _rms(x, gain):
    ms = jnp.mean(x * x, axis=-1, keepdims=True)
    return x * lax.rsqrt(ms + NORM_EPS) * gain


def _split_bf16(x):
    hi = x.astype(BF16)
    return hi, (x - hi.astype(F32)).astype(BF16)


def _rope(x, cos, sin):
    half = ROT_DIM // 2
    lane = lax.broadcasted_iota(jnp.int32, x.shape, 1)
    partner = jnp.where(lane < half, -pltpu.roll(x, HEAD_DIM - half, 1), pltpu.roll(x, half, 1))
    return x * cos + partner * sin


def _nt_dot(a, b):
    return lax.dot_general(a, b, (((1,), (1,)), ((), ())), preferred_element_type=F32)


def _rope_table_body(pos_ref, freq_ref, cos_ref, sin_ref):
    ang = pos_ref[...].astype(F32) * freq_ref[...]
    cos_ref[...] = jnp.cos(ang)
    sin_ref[...] = jnp.sin(ang)


def rope_tables(positions):
    m = positions.size
    half = ROT_DIM // 2
    freqs = jnp.float32(ROPE_THETA) ** (-jnp.arange(half, dtype=F32) * 2.0 / ROT_DIM)
    freq_row = jnp.concatenate([freqs, freqs, jnp.zeros((HEAD_DIM - ROT_DIM,), F32)])[None]
    tm = min(m, 1024)
    tab = pl.BlockSpec((tm, HEAD_DIM), lambda i: (i, 0))
    return pl.pallas_call(
        _rope_table_body,
        grid=(m // tm,),
        in_specs=[pl.BlockSpec((tm, 1), lambda i: (i, 0)), pl.BlockSpec((1, HEAD_DIM), lambda i: (0, 0))],
        out_specs=[tab, tab],
        out_shape=[jax.ShapeDtypeStruct((m, HEAD_DIM), F32)] * 2,
        compiler_params=_params(("parallel",), 32),
        name="rope_tables",
    )(positions.reshape(m, 1), freq_row)


def _nmm_body(*refs, n_w, n_extra, n_out, sub, variants, on_row_start):
    x_ref, g_ref = refs[0], refs[1]
    w_refs = refs[2:2 + n_w]
    extra = refs[2 + n_w:2 + n_w + n_extra]
    outs = refs[2 + n_w + n_extra:2 + n_w + n_extra + n_out]
    xn_ref = refs[2 + n_w + n_extra + n_out]
    extra = extra + refs[3 + n_w + n_extra + n_out:]
    j = pl.program_id(1)

    @pl.when(j == 0)
    def _():
        xn_ref[...] = _rms(x_ref[...], g_ref[...]).astype(BF16)
        if on_row_start is not None:
            on_row_start(extra)

    width = w_refs[0].shape[1] // sub
    cols = [slice(s * width, (s + 1) * width) for s in range(sub)]

    def run(epilogue):
        xn = xn_ref[...]
        accs = [[jnp.dot(xn, w[:, c], preferred_element_type=F32) for w in w_refs] for c in cols]
        for c, acc in zip(cols, accs):
            epilogue(acc, c, extra, outs)

    if len(variants) == 1:
        run(variants[0][1])
    else:
        for applies, epilogue in variants:
            pl.when(applies(j))(functools.partial(run, epilogue))


def norm_matmul(h, gain, w, w_maps, n_col_tiles, tn, extra, outs, variants, sub=1, tm=1024, vmem_mib=56,
                scratch=(), on_row_start=None):
    m, d = h.shape
    tm = min(tm, m)
    in_specs = [pl.BlockSpec((tm, d), lambda i, j: (i, 0)),
                pl.BlockSpec((1, d), lambda i, j: (0, 0))]
    in_specs += [pl.BlockSpec((d, tn), wm) for wm in w_maps]
    in_specs += [spec for _, spec in extra]
    return pl.pallas_call(
        functools.partial(_nmm_body, n_w=len(w_maps), n_extra=len(extra), n_out=len(outs), sub=sub,
                          variants=variants, on_row_start=on_row_start),
        grid=(m // tm, n_col_tiles),
        in_specs=in_specs,
        out_specs=[spec for _, spec in outs],
        out_shape=[shape for shape, _ in outs],
        scratch_shapes=[pltpu.VMEM((tm, d), BF16)] + list(scratch),
        compiler_params=_params(("arbitrary" if scratch else "parallel", "arbitrary"), vmem_mib),
        name="norm_matmul",
    )(h, gain.reshape(1, d), *([w] * len(w_maps)), *[a for a, _ in extra])


def _head_norm_rope_store(acc, cols, gain_ref, table_refs, out_plain, out_rot):
    width = acc.shape[1]
    same_head = (lax.broadcasted_iota(jnp.int32, (width, width), 0) // HEAD_DIM
                 == lax.broadcasted_iota(jnp.int32, (width, width), 1) // HEAD_DIM)
    ones = jnp.where(same_head, 1.0, 0.0).astype(BF16)
    sq_hi, sq_lo = _split_bf16(acc * acc)
    ssq = (jnp.dot(sq_hi, ones, preferred_element_type=F32)
           + jnp.dot(sq_lo, ones, preferred_element_type=F32))
    normed = acc * lax.rsqrt(ssq * (1.0 / HEAD_DIM) + NORM_EPS) * gain_ref[:, cols]
    if out_plain is not None:
        out_plain[:, cols] = normed.astype(BF16)
    if out_rot is not None:
        cos, sin = table_refs[0][...], table_refs[1][...]
        for hh in range(width // HEAD_DIM):
            sl = slice(hh * HEAD_DIM, (hh + 1) * HEAD_DIM)
            dst = slice(cols.start + hh * HEAD_DIM, cols.start + (hh + 1) * HEAD_DIM)
            out_rot[:, dst] = _rope(normed[:, sl], cos, sin).astype(BF16)


def _omm_body(*refs, n_in, prologue):
    ins = refs[:n_in]
    w_ref, res_ref, o_ref = refs[n_in:n_in + 3]
    a = prologue(ins)
    o_ref[...] = res_ref[...] + jnp.dot(a, w_ref[...], preferred_element_type=F32)


def out_matmul(ins, prologue, w, res, tm=512, vmem_mib=48):
    m, d = res.shape
    tm = min(tm, m)
    k = w.shape[0]
    return pl.pallas_call(
        functools.partial(_omm_body, n_in=len(ins), prologue=prologue),
        grid=(m // tm,),
        in_specs=[spec for _, spec in ins] + [
            _resident((k, d), lambda i: (0, 0)),
            pl.BlockSpec((tm, d), lambda i: (i, 0))],
        out_specs=pl.BlockSpec((tm, d), lambda i: (i, 0)),
        out_shape=jax.ShapeDtypeStruct((m, d), F32),
        compiler_params=_params(("parallel",), vmem_mib),
        name="out_matmul",
    )(*[a for a, _ in ins], w, res)


def _mlp_body(x_ref, g_ref, w1_ref, w2_ref, o_ref, xn_ref):
    @pl.when(pl.program_id(1) == 0)
    def _():
        x = x_ref[...]
        xn_ref[...] = _rms(x, g_ref[...]).astype(BF16)
        o_ref[...] = x

    a = jnp.dot(xn_ref[...], w1_ref[...].astype(BF16), preferred_element_type=F32)
    a = jnp.square(jnp.maximum(a, 0.0)).astype(BF16)
    o_ref[...] += jnp.dot(a, w2_ref[...].astype(BF16), preferred_element_type=F32)


def mlp_layer(h, gain, w1, w2, layer, tm=1024, tf=512):
    m, d = h.shape
    dff = w1.shape[2]
    tm = min(tm, m)
    return pl.pallas_call(
        _mlp_body,
        grid=(m // tm, dff // tf),
        in_specs=[pl.BlockSpec((tm, d), lambda i, f: (i, 0)),
                  pl.BlockSpec((1, d), lambda i, f: (0, 0)),
                  pl.BlockSpec((None, d, tf), lambda i, f: (layer, 0, f)),
                  pl.BlockSpec((None, tf, d), lambda i, f: (layer, f, 0))],
        out_specs=pl.BlockSpec((tm, d), lambda i, f: (i, 0)),
        out_shape=jax.ShapeDtypeStruct((m, d), F32),
        scratch_shapes=[pltpu.VMEM((tm, d), BF16)],
        compiler_params=_params(("parallel", "arbitrary"), 56),
        name="mlp",
    )(h, gain.reshape(1, d), w1, w2)


def _ple_body(x_ref, g_ref, wg_ref, p_ref, wp_ref, o_ref):
    x = x_ref[...]
    xn = _rms(x, g_ref[...]).astype(BF16)
    gate = jax.nn.sigmoid(jnp.dot(xn, wg_ref[...], preferred_element_type=F32))
    emb = jnp.dot(p_ref[...].astype(BF16), wp_ref[...], preferred_element_type=F32)
    o_ref[...] = x + gate * emb


def ple_layer(h, gain, wg, p, wp, layer, tm=512):
    m, d = h.shape
    pd = p.shape[2]
    tm = min(tm, m)
    return pl.pallas_call(
        _ple_body,
        grid=(m // tm,),
        in_specs=[pl.BlockSpec((tm, d), lambda i: (i, 0)),
                  pl.BlockSpec((1, d), lambda i: (0, 0)),
                  _resident((None, d, d), lambda i: (layer, 0, 0)),
                  pl.BlockSpec((None, tm, pd), lambda i: (layer, i, 0)),
                  _resident((None, pd, d), lambda i: (layer, 0, 0))],
        out_specs=pl.BlockSpec((tm, d), lambda i: (i, 0)),
        out_shape=jax.ShapeDtypeStruct((m, d), F32),
        compiler_params=_params(("parallel",), 48),
        name="ple",
    )(h, gain.reshape(1, d), wg, p, wp)


def _flash_buffers(chains, keys, queries):
    return [pltpu.VMEM((chains, keys, queries), F32), pltpu.VMEM((chains, keys, queries), BF16),
            pltpu.VMEM((chains, HEAD_DIM, queries), F32)]


def _flash_keep(s_ref, s, bias):
    if bias.shape[0] == 1:
        s_ref[...] = s
        raw_max = jnp.max(s, axis=0, keepdims=True)
        return raw_max + bias, raw_max - 2.0 * bias
    s = s + bias
    s_ref[...] = s
    s_max = jnp.max(s, axis=0, keepdims=True)
    return s_max, s_max


def _flash_start(buf, k, q_t, bias):
    s_ref, p_ref, acc_ref = buf
    p_ref[...] = jnp.zeros_like(p_ref)
    acc_ref[...] = jnp.zeros_like(acc_ref)
    queries = s_ref.shape[1]
    return _flash_keep(s_ref, jnp.dot(k, q_t, preferred_element_type=F32), bias) + (
        jnp.ones((1, queries), F32), jnp.full((1, queries), MASKED_LOGIT, F32), jnp.zeros((1, queries), F32))


def _flash_stage(bufs, states, v_prev, following):
    products = [jnp.dot(v, buf[1][...], preferred_element_type=F32) for v, buf in zip(v_prev, bufs)]
    if following is not None:
        upcoming = [jnp.dot(k, q_t, preferred_element_type=F32) for k, q_t, _ in following]
    out = []
    for c, ((s_ref, p_ref, acc_ref), state, pv) in enumerate(zip(bufs, states, products)):
        s_max, floor, alpha_prev, m, l = state
        m_new = jnp.maximum(m, s_max)
        p = jnp.exp2(s_ref[...] - jnp.maximum(m_new, floor))
        alpha = jnp.exp((m - m_new) * LN2)
        l = alpha * l + jnp.sum(p, axis=0, keepdims=True)
        p_ref[...] = p.astype(BF16)
        acc_ref[...] = alpha_prev * acc_ref[...] + pv
        kept = (s_max, floor) if following is None else _flash_keep(s_ref, upcoming[c], following[c][2])
        out.append(kept + (alpha, m_new, l))
    return tuple(out)


def _flash_finish(buf, state, v_last):
    _, p_ref, acc_ref = buf
    alpha, l = state[2], state[4]
    return (alpha * acc_ref[...] + jnp.dot(v_last, p_ref[...], preferred_element_type=F32)) / l


def _store_transposed(v_ref, vt_ref, chunk):
    for n in range(vt_ref.shape[0]):
        vt_ref[n] = v_ref[n * chunk:(n + 1) * chunk, :].astype(F32).T.astype(BF16)


def _top_rank(v, row):
    n = v.shape[0]
    if n % 8:
        rank = jnp.zeros(v.shape, jnp.int32)
        for j in range(n):
            r = v[j:j + 1, :]
            rank = rank + jnp.where(row > j, jnp.where(r >= v, 1, 0), jnp.where(r > v, 1, 0))
        return rank
    starts = range(0, n, 8)
    groups = [v[lo:lo + 8, :] for lo in starts]
    ranks = [jnp.zeros((8, v.shape[1]), jnp.int32) for _ in starts]
    for j in range(n):
        r = jnp.broadcast_to(v[j:j + 1, :], (8, v.shape[1]))
        for g, lo in enumerate(starts):
            if lo > j:
                ranks[g] = ranks[g] + jnp.where(r >= groups[g], 1, 0)
            elif lo + 7 <= j:
                ranks[g] = ranks[g] + jnp.where(r > groups[g], 1, 0)
            else:
                below = lax.broadcasted_iota(jnp.int32, (8, v.shape[1]), 0) > j - lo
                ranks[g] = ranks[g] + jnp.where(below, jnp.where(r >= groups[g], 1, 0),
                                                jnp.where(r > groups[g], 1, 0))
    return jnp.concatenate(ranks, axis=0)


def _moba_body(q_ref, k_ref, v_ref, o_ref, kmean_ref, vt_ref, bias_ref, s_ref, p_ref, acc_ref,
               *, n_blocks, heads):
    blk = MOBA_BLOCK
    own = pl.program_id(2)
    cols = [slice(e * HEAD_DIM, (e + 1) * HEAD_DIM) for e in range(heads)]
    bufs = [(s_ref.at[e], p_ref.at[e], acc_ref.at[e]) for e in range(heads)]

    @pl.when(own == 0)
    def _():
        for e in range(heads):
            for n in range(n_blocks):
                kb = k_ref[n * blk:(n + 1) * blk, cols[e]].astype(F32)
                kmean_ref[e, n:n + 1, :] = jnp.mean(kb, axis=0, keepdims=True)
                vt_ref[e, n] = v_ref[n * blk:(n + 1) * blk, cols[e]].astype(F32).T.astype(BF16)

    q_ts = [q_ref[:, cols[e]].astype(F32).T.astype(BF16) for e in range(heads)]
    for e in range(heads):
        gate = jnp.dot(kmean_ref[e].astype(BF16), q_ts[e], preferred_element_type=F32)
        row = lax.broadcasted_iota(jnp.int32, gate.shape, 0)
        gate = jnp.where(row < own, gate, -jnp.inf)
        rank = _top_rank(gate, row)
        bias = jnp.where(row < own, jnp.where(rank < MOBA_TOPK, 0.0, MASKED_LOGIT), MASKED_LOGIT)
        for n in range(n_blocks):
            bias_ref[e, n] = jnp.broadcast_to(bias[n:n + 1, :], (8, blk))

    def keys(j):
        start = pl.multiple_of(j * blk, blk)
        return [k_ref[pl.ds(start, blk), cols[e]] for e in range(heads)]

    def values(j):
        return [vt_ref[e, j] for e in range(heads)]

    def stage(j, states, next_biases):
        following = None if next_biases is None else list(zip(keys(j + 1), q_ts, next_biases))
        return _flash_stage(bufs, states, values(jnp.maximum(j - 1, 0)), following)

    def chosen_bias(j):
        return [jnp.max(bias_ref[e, j], axis=0, keepdims=True) for e in range(heads)]

    causal = jnp.where(lax.broadcasted_iota(jnp.int32, (blk, blk), 0)
                       <= lax.broadcasted_iota(jnp.int32, (blk, blk), 1), 0.0, MASKED_LOGIT)
    first_bias = [jnp.where(own == 0, causal, b) for b in chosen_bias(0)]
    states = tuple(_flash_start(buf, k, q_t, b) for buf, k, q_t, b in zip(bufs, keys(0), q_ts, first_bias))
    states = lax.fori_loop(0, own - 1, lambda j, st: stage(j, st, chosen_bias(j + 1)), states)
    states = lax.cond(own > 0, lambda st: stage(own - 1, st, [causal] * heads), lambda st: st, states)
    states = stage(own, states, None)
    for e, v_last in enumerate(values(own)):
        o_ref[:, cols[e]] = _flash_finish(bufs[e], states[e], v_last).T.astype(BF16)


def moba_attention(qkv, batch, seq, n_heads, heads_per_step=4):
    blk = MOBA_BLOCK
    nq = seq // blk
    hp = heads_per_step
    width = hp * HEAD_DIM
    groups = n_heads // hp
    return pl.pallas_call(
        functools.partial(_moba_body, n_blocks=nq, heads=hp),
        grid=(batch, groups, nq),
        in_specs=[pl.BlockSpec((blk, width), lambda b, h, i: (b * nq + i, h)),
                  pl.BlockSpec((seq, width), lambda b, h, i: (b, groups + h)),
                  pl.BlockSpec((seq, width), lambda b, h, i: (b, 2 * groups + h))],
        out_specs=pl.BlockSpec((blk, width), lambda b, h, i: (b * nq + i, h)),
        out_shape=jax.ShapeDtypeStruct((batch * seq, n_heads * HEAD_DIM), BF16),
        scratch_shapes=[pltpu.VMEM((hp, nq, HEAD_DIM), F32),
                        pltpu.VMEM((hp, nq, HEAD_DIM, blk), BF16),
                        pltpu.VMEM((hp, nq, 8, blk), F32)] + _flash_buffers(hp, blk, blk),
        compiler_params=_params(("parallel", "parallel", "arbitrary"), 32),
        name="moba_attention",
    )(qkv, qkv, qkv)


def moba_layer(h, gain, w_qkv, q_gain, k_gain, w_o, tables, batch, seq):
    m, d = h.shape
    n_heads = d // HEAD_DIM
    tn = 1024
    n_qk_tiles = 2 * d // tn
    gain_row = jnp.concatenate([jnp.tile(q_gain, n_heads), jnp.tile(k_gain * SOFTMAX_EXP2_SCALE, n_heads),
                                jnp.ones((d,), F32)])[None]

    def qk_epilogue(accs, cols, extra, outs):
        _head_norm_rope_store(accs[0], cols, extra[0], extra[1:], None, outs[0])

    def v_epilogue(accs, cols, extra, outs):
        outs[0][:, cols] = accs[0].astype(BF16)

    tm = min(1024, m)
    tab = pl.BlockSpec((tm, HEAD_DIM), lambda i, j: (i, 0))
    (qkv,) = norm_matmul(
        h, gain, w_qkv, [lambda i, j: (0, j)], 3 * d // tn, tn,
        extra=[(gain_row, pl.BlockSpec((1, tn), lambda i, j: (0, j)))] + [(t, tab) for t in tables],
        outs=[(jax.ShapeDtypeStruct((m, 3 * d), BF16), pl.BlockSpec((tm, tn), lambda i, j: (i, j)))],
        variants=[(lambda j: j < n_qk_tiles, qk_epilogue), (lambda j: j >= n_qk_tiles, v_epilogue)],
        sub=4, tm=tm)
    o = moba_attention(qkv, batch, seq, n_heads)
    tmo = min(512, m)
    return out_matmul([(o, pl.BlockSpec((tmo, d), lambda i: (i, 0)))],
                      lambda ins: ins[0][...], w_o, h, tm=tmo)


def _pool_body(x_ref, halo_ref, g_ref, w_ref, s_ref, o_ref, *, tiles_per_seq, halo):
    i = pl.program_id(0)
    tm, d = x_ref.shape
    group = d // len(POOL_WINDOWS)
    x = x_ref[...]
    gain = g_ref[...]
    xn = _rms(x, gain)
    prev = jnp.where(i % tiles_per_seq == 0, 0.0, _rms(halo_ref[...], gain))
    pos = (i % tiles_per_seq) * tm + lax.broadcasted_iota(jnp.int32, (tm, 1), 0)
    for g, win in enumerate(POOL_WINDOWS):
        sl = slice(g * group, (g + 1) * group)
        run = jnp.concatenate([prev[:, sl], xn[:, sl]], axis=0)
        span = 1
        while span < win:
            run = run + pltpu.roll(run, span, 0)
            span *= 2
        cnt = jnp.minimum(pos + 1, win).astype(F32)
        mean = run[halo:, :] / cnt
        mix = jnp.dot((mean - xn[:, sl]).astype(BF16), w_ref[g], preferred_element_type=F32)
        o_ref[:, sl] = x[:, sl] + mix * s_ref[:, sl]


def pool_layer(h, gain, w_groups, scale, seq, tm=512):
    m, d = h.shape
    halo = 16
    assert max(POOL_WINDOWS) <= halo
    tm = min(tm, seq)
    group = d // len(POOL_WINDOWS)
    return pl.pallas_call(
        functools.partial(_pool_body, tiles_per_seq=seq // tm, halo=halo),
        grid=(m // tm,),
        in_specs=[pl.BlockSpec((tm, d), lambda i: (i, 0)),
                  pl.BlockSpec((halo, d), lambda i: (jnp.maximum(i * (tm // halo) - 1, 0), 0)),
                  pl.BlockSpec((1, d), lambda i: (0, 0)),
                  _resident((len(POOL_WINDOWS), group, group), lambda i: (0, 0, 0)),
                  pl.BlockSpec((1, d), lambda i: (0, 0))],
        out_specs=pl.BlockSpec((tm, d), lambda i: (i, 0)),
        out_shape=jax.ShapeDtypeStruct((m, d), F32),
        compiler_params=_params(("parallel",), 48),
        name="pool_mixer",
    )(h, h, gain.reshape(1, d), w_groups, scale.reshape(1, d))


def conv_layer(h, gain, w_in, conv_w, conv_b, w_o, seq):
    m, d = h.shape
    tn = 512
    nj = d // tn

    tm = min(1024, seq)
    tiles_per_seq = seq // tm
    halo = 8

    def on_row_start(extra):
        @pl.when(pl.program_id(0) % tiles_per_seq == 0)
        def _():
            extra[2][...] = jnp.zeros_like(extra[2])

    def epilogue(accs, cols, extra, outs):
        cw_ref, cb_ref, carry_ref = extra
        j = pl.program_id(1)
        u0 = accs[1] * accs[2]
        prev = carry_ref[j, :, cols]
        row = lax.broadcasted_iota(jnp.int32, u0.shape, 0)
        u1 = jnp.where(row == 0, prev[halo - 1:halo, :], pltpu.roll(u0, 1, 0))
        u2 = jnp.where(row == 0, prev[halo - 2:halo - 1, :],
                       jnp.where(row == 1, prev[halo - 1:halo, :], pltpu.roll(u0, 2, 0)))
        conv = (cw_ref[0:1, cols] * u2 + cw_ref[1:2, cols] * u1 + cw_ref[2:3, cols] * u0
                + cb_ref[:, cols])
        outs[0][:, cols] = (accs[0] * conv).astype(BF16)
        carry_ref[j, :, cols] = u0[tm - halo:, :]

    (y,) = norm_matmul(
        h, gain, w_in, [lambda i, j: (0, j), lambda i, j: (0, j + nj), lambda i, j: (0, j + 2 * nj)],
        nj, tn,
        extra=[(conv_w, pl.BlockSpec((CONV_WIDTH, tn), lambda i, j: (0, j))),
               (conv_b.reshape(1, d), pl.BlockSpec((1, tn), lambda i, j: (0, j)))],
        outs=[(jax.ShapeDtypeStruct((m, d), BF16), pl.BlockSpec((tm, tn), lambda i, j: (i, j)))],
        variants=[(None, epilogue)], sub=2, tm=tm,
        scratch=[pltpu.VMEM((nj, halo, tn), F32)], on_row_start=on_row_start)

    tmo = min(512, m)
    return out_matmul([(y, pl.BlockSpec((tmo, d), lambda i: (i, 0)))],
                      lambda ins: ins[0][...], w_o, h, tm=tmo)


def _compress_body(x_ref, pos_ref, w1_ref, w2_ref, g_ref, kc_ref, vc_ref, lo_ref, hi_ref, *, groups):
    step = pl.program_id(1)
    stride = NSA_CMP_STRIDE

    @pl.when(step == 0)
    def _():
        lo_ref[...] = jnp.zeros_like(lo_ref)
        hi_ref[...] = jnp.zeros_like(hi_ref)

    for kv in range(2):
        p_lo = pos_ref[kv, pl.ds(step, 1), :]
        p_hi = pos_ref[kv, pl.ds(stride + step, 1), :]
        w_lo = w1_ref[kv, step]
        w_hi = w1_ref[kv, stride + step]
        for g in range(groups):
            c = kv * groups + g
            t = x_ref[:, c * HEAD_DIM:(c + 1) * HEAD_DIM]
            lo_ref[c] += jnp.dot((t + p_lo).astype(BF16), w_lo, preferred_element_type=F32)
            hi_ref[c] += jnp.dot((t + p_hi).astype(BF16), w_hi, preferred_element_type=F32)

    @pl.when(step == stride - 1)
    def _():
        n_half = lo_ref.shape[1]
        for kv in range(2):
            for g in range(groups):
                c = kv * groups + g
                pre = lo_ref[c] + pltpu.roll(hi_ref[c], n_half - 1, 0)
                hid = jax.nn.gelu(pre)
                if kv == 0:
                    out = jnp.dot(hid.astype(BF16), w2_ref[0], preferred_element_type=F32)
                    kc_ref[g] = _rms(out, g_ref[...]).astype(BF16)
                else:
                    vc_ref[g] = jnp.dot(w2_ref[1], hid.T.astype(BF16),
                                        preferred_element_type=F32).astype(BF16)


def nsa_compress(raw, cmp_pos, cmp_w1, cmp_w2, k_gain0, batch, seq):
    assert NSA_CMP_LEN == 2 * NSA_CMP_STRIDE
    stride = NSA_CMP_STRIDE
    groups = NSA_KV_GROUPS
    width = raw.shape[1]
    n_half = seq // stride
    x = raw.reshape(batch * n_half, stride * width)
    w1 = cmp_w1.reshape(2, NSA_CMP_LEN, HEAD_DIM, HEAD_DIM)
    cmp_w2 = jnp.stack([cmp_w2[0], cmp_w2[1].T])
    k_out = jax.ShapeDtypeStruct((batch, groups, n_half, HEAD_DIM), BF16)
    v_out = jax.ShapeDtypeStruct((batch, groups, HEAD_DIM, n_half), BF16)
    k_spec = pl.BlockSpec((None, groups, n_half, HEAD_DIM), lambda b, s: (b, 0, 0, 0))
    v_spec = pl.BlockSpec((None, groups, HEAD_DIM, n_half), lambda b, s: (b, 0, 0, 0))
    return pl.pallas_call(
        functools.partial(_compress_body, groups=groups),
        grid=(batch, stride),
        in_specs=[pl.BlockSpec((n_half, width), lambda b, s: (b, s)),
                  pl.BlockSpec((2, NSA_CMP_LEN, HEAD_DIM), lambda b, s: (0, 0, 0)),
                  pl.BlockSpec((2, NSA_CMP_LEN, HEAD_DIM, HEAD_DIM), lambda b, s: (0, 0, 0, 0)),
                  pl.BlockSpec((2, HEAD_DIM, HEAD_DIM), lambda b, s: (0, 0, 0)),
                  pl.BlockSpec((1, HEAD_DIM), lambda b, s: (0, 0))],
        out_specs=[k_spec, v_spec],
        out_shape=[k_out, v_out],
        scratch_shapes=[pltpu.VMEM((2 * groups, n_half, HEAD_DIM), F32)] * 2,
        compiler_params=_params(("parallel", "arbitrary"), 32),
        name="nsa_compress",
    )(x, cmp_pos, w1, cmp_w2, k_gain0.reshape(1, HEAD_DIM))


def _cmp_select_body(q_ref, kc_ref, vct_ref, gate_ref, o_ref, sel_ref, *, n_cmp, n_top, q_per_kv):
    tq = q_ref.shape[0]
    first_head = pl.program_id(1) * q_per_kv
    n_pad = kc_ref.shape[0]
    n_slc = sel_ref.shape[0]
    scale = HEAD_DIM ** -0.5
    t = pl.program_id(2) * tq + lax.broadcasted_iota(jnp.int32, (1, tq), 1)
    n = lax.broadcasted_iota(jnp.int32, (n_pad, 1), 0)
    bias = jnp.where(n < n_cmp, jnp.where(n * NSA_CMP_STRIDE + (NSA_CMP_LEN - 1) <= t, 0.0, MASKED_LOGIT),
                     MASKED_LOGIT)
    seen = t >= NSA_CMP_LEN - 1
    kc = kc_ref[...]
    vct = vct_ref[...]
    p_sum = jnp.zeros((n_pad, tq), F32)
    for r in range(q_per_kv):
        sl = slice(r * HEAD_DIM, (r + 1) * HEAD_DIM)
        s = _nt_dot(kc, q_ref[:, sl]) * scale + bias
        e = jnp.exp(s - jnp.max(s, axis=0, keepdims=True))
        inv = jnp.where(seen, 1.0 / jnp.sum(e, axis=0, keepdims=True), 0.0)
        gate = gate_ref[pl.ds(3 * (first_head + r), 1), :]
        o_t = jnp.dot(vct, e.astype(BF16), preferred_element_type=F32) * (inv * gate)
        o_ref[:, sl] = o_t.T.astype(BF16)
        p_sum = p_sum + e * inv

    jj = lax.broadcasted_iota(jnp.int32, (n_slc, n_pad), 0) * NSA_SLC_LEN
    nn = lax.broadcasted_iota(jnp.int32, (n_slc, n_pad), 1) * NSA_CMP_STRIDE
    overlap = jnp.where((nn < jj + NSA_SLC_LEN) & (jj < nn + NSA_CMP_LEN)
                        & (nn < n_cmp * NSA_CMP_STRIDE), 1.0, 0.0).astype(BF16)
    p_hi = p_sum.astype(BF16)
    p_lo = (p_sum - p_hi.astype(F32)).astype(BF16)
    imp = (jnp.dot(overlap, p_hi, preferred_element_type=F32)
           + jnp.dot(overlap, p_lo, preferred_element_type=F32))

    cur = t // NSA_SLC_LEN
    jb = lax.broadcasted_iota(jnp.int32, (n_slc, tq), 0)
    val = jnp.where(jb == cur, jnp.inf,
                    jnp.where(jb == 0, jnp.inf, jnp.where(jb < cur, imp, -jnp.inf)))
    rank = _top_rank(val, jb)
    sel_ref[...] = jnp.where(rank < n_top, jnp.where(val > -jnp.inf, 1.0, 0.0), 0.0)


def nsa_cmp_select(q_cmp, k_cmp, v_cmp_t, gates_t, batch, seq, tq=256):
    m, d = q_cmp.shape
    groups = NSA_KV_GROUPS
    q_per_kv = d // HEAD_DIM // groups
    gw = q_per_kv * HEAD_DIM
    tq = min(tq, seq)
    nq = seq // tq
    n_pad = k_cmp.shape[2]
    n_cmp = (seq - NSA_CMP_LEN) // NSA_CMP_STRIDE + 1
    n_slc = seq // NSA_SLC_LEN
    return pl.pallas_call(
        functools.partial(_cmp_select_body, n_cmp=n_cmp, n_top=min(NSA_SLC_TOPK, n_slc),
                          q_per_kv=q_per_kv),
        grid=(batch, groups, nq),
        in_specs=[pl.BlockSpec((tq, gw), lambda b, g, i: (b * nq + i, g)),
                  pl.BlockSpec((None, None, n_pad, HEAD_DIM), lambda b, g, i: (b, g, 0, 0)),
                  pl.BlockSpec((None, None, HEAD_DIM, n_pad), lambda b, g, i: (b, g, 0, 0)),
                  pl.BlockSpec((gates_t.shape[0], tq), lambda b, g, i: (0, b * nq + i))],
        out_specs=[pl.BlockSpec((tq, gw), lambda b, g, i: (b * nq + i, g)),
                   pl.BlockSpec((None, None, n_slc, tq), lambda b, g, i: (b, g, 0, i))],
        out_shape=[jax.ShapeDtypeStruct((m, d), BF16),
                   jax.ShapeDtypeStruct((batch, groups, n_slc, seq), F32)],
        compiler_params=_params(("parallel", "parallel", "parallel"), 32),
        name="nsa_cmp_select",
    )(q_cmp, k_cmp, v_cmp_t, gates_t)


def _nsa_attn_body(q_ref, ks_ref, vs_ref, kw_ref, vw_ref, sel_ref, gate_ref, oc_ref, o_ref,
                   vst_ref, vwt_ref, bias_ref, s_ref, p_ref, acc_ref, *, q_per_kv, per):
    tq = q_ref.shape[0]
    kc = KV_CHUNK
    sub = kc // NSA_SLC_LEN
    step = pl.program_id(2)
    q0 = step * tq

    @pl.when(step == 0)
    def _():
        _store_transposed(vs_ref, vst_ref, kc)
        _store_transposed(vw_ref, vwt_ref, kc)

    n_chain = q_per_kv // per
    q_ts = [jnp.concatenate([q_ref[:, r * HEAD_DIM:(r + 1) * HEAD_DIM]
                             for r in range(c * per, (c + 1) * per)], axis=0).astype(F32).T.astype(BF16)
            for c in range(n_chain)]
    t = q0 + lax.broadcasted_iota(jnp.int32, (1, tq), 1)
    bias_ref[...] = jnp.where(sel_ref[...] > 0.5, 0.0, MASKED_LOGIT)

    def chain_bias(bias):
        return jnp.concatenate([bias] * per, axis=1)

    def selected_bias(c, kpos):
        picked = jnp.concatenate(
            [jnp.broadcast_to(bias_ref[pl.ds(c * sub + a, 1), :], (NSA_SLC_LEN, tq)) for a in range(sub)],
            axis=0)
        return chain_bias(jnp.where(kpos <= t, picked, MASKED_LOGIT))

    def window_bias(kpos):
        return chain_bias(jnp.where(kpos <= t, jnp.where(t - kpos < NSA_WINDOW, 0.0, MASKED_LOGIT),
                                    MASKED_LOGIT))

    def selected_logits(c):
        start = pl.multiple_of(c * kc, kc)
        kpos = start + lax.broadcasted_iota(jnp.int32, (kc, 1), 0)
        k, bias = ks_ref[pl.ds(start, kc), :], selected_bias(c, kpos)
        return [(k, q_t, bias) for q_t in q_ts]

    def window_logits(c):
        start = pl.multiple_of(c * kc, kc)
        kpos = start + lax.broadcasted_iota(jnp.int32, (kc, 1), 0)
        k, bias = kw_ref[pl.ds(start, kc), :], window_bias(kpos)
        return [(k, q_t, bias) for q_t in q_ts]

    bufs = [(s_ref.at[c], p_ref.at[c], acc_ref.at[c]) for c in range(2 * n_chain)]

    def selected_only(c, states):
        return _flash_stage(bufs[:n_chain], states, [vst_ref[jnp.maximum(c - 1, 0)]] * n_chain,
                            selected_logits(c + 1))

    def selected_and_window(c, states, is_last=False):
        prev = jnp.maximum(c - 1, 0)
        following = None if is_last else selected_logits(c + 1) + window_logits(c + 1)
        return _flash_stage(bufs, states, [vst_ref[prev]] * n_chain + [vwt_ref[prev]] * n_chain, following)

    first = jnp.maximum(q0 - (NSA_WINDOW - 1), 0) // kc
    last = (q0 + tq - 1) // kc
    states = tuple(_flash_start(buf, *f) for buf, f in zip(bufs[:n_chain], selected_logits(0)))
    states = lax.fori_loop(0, first, selected_only, states)
    states = states + tuple(_flash_start(buf, *f) for buf, f in zip(bufs[n_chain:], window_logits(first)))
    states = lax.fori_loop(first, last, selected_and_window, states)
    states = selected_and_window(last, states, is_last=True)

    first_head = pl.program_id(1) * q_per_kv
    for c in range(n_chain):
        selected = _flash_finish(bufs[c], states[c], vst_ref[last])
        window = _flash_finish(bufs[n_chain + c], states[n_chain + c], vwt_ref[last])
        for i in range(per):
            r = c * per + i
            lanes = slice(i * tq, (i + 1) * tq)
            cols = slice(r * HEAD_DIM, (r + 1) * HEAD_DIM)
            g_slc = gate_ref[pl.ds(3 * (first_head + r) + 1, 1), :]
            g_win = gate_ref[pl.ds(3 * (first_head + r) + 2, 1), :]
            mixed = (selected[:, lanes] * g_slc + window[:, lanes] * g_win).T
            o_ref[:, cols] = (mixed + oc_ref[:, cols].astype(F32)).astype(BF16)


def nsa_attention(q_rot, kv, sel, gates_t, o_cmp, batch, seq, tq=256):
    m, d = q_rot.shape
    groups = NSA_KV_GROUPS
    q_per_kv = d // HEAD_DIM // groups
    gw = q_per_kv * HEAD_DIM
    tq = min(tq, seq)
    nq = seq // tq
    n_slc = sel.shape[2]
    assert seq % KV_CHUNK == 0 and KV_CHUNK % NSA_SLC_LEN == 0

    def kv_spec(part):
        return pl.BlockSpec((seq, HEAD_DIM), lambda b, g, i: (b, part * groups + g))

    q_spec = pl.BlockSpec((tq, gw), lambda b, g, i: (b * nq + i, g))
    out = jax.ShapeDtypeStruct((m, d), BF16)
    v_t = pltpu.VMEM((seq // KV_CHUNK, HEAD_DIM, KV_CHUNK), BF16)
    per = 1
    return pl.pallas_call(
        functools.partial(_nsa_attn_body, q_per_kv=q_per_kv, per=per),
        grid=(batch, groups, nq),
        in_specs=[q_spec, kv_spec(0), kv_spec(1), kv_spec(2), kv_spec(3),
                  pl.BlockSpec((None, None, n_slc, tq), lambda b, g, i: (b, g, 0, i)),
                  pl.BlockSpec((gates_t.shape[0], tq), lambda b, g, i: (0, b * nq + i)),
                  q_spec],
        out_specs=q_spec,
        out_shape=out,
        scratch_shapes=[v_t, v_t, pltpu.VMEM((n_slc, tq), F32)]
        + _flash_buffers(2 * q_per_kv // per, KV_CHUNK, per * tq),
        compiler_params=_params(("parallel", "parallel", "arbitrary"), 32),
        name="nsa_attention",
    )(q_rot, kv, kv, kv, kv, sel, gates_t, o_cmp)


def nsa_layer(h, gain, w_proj, q_gain, k_gain, cmp_pos, cmp_w1, cmp_w2, w_o, tables, batch, seq):
    m, d = h.shape
    n_heads = d // HEAD_DIM
    groups = NSA_KV_GROUPS
    gwk = groups * HEAD_DIM
    tn = 512
    tm = min(1024, m)
    assert tn == gwk and 3 * n_heads <= HEAD_DIM
    raw0, kv0, gate0 = d // tn, d // tn + 2, d // tn + 6
    ones = jnp.ones((gwk,), F32)
    gain_row = jnp.concatenate([jnp.tile(q_gain, n_heads), ones, ones,
                                jnp.tile(k_gain[1] * SOFTMAX_EXP2_SCALE, groups), ones,
                                jnp.tile(k_gain[2] * SOFTMAX_EXP2_SCALE, groups), ones, ones])[None]

    def q_epilogue(accs, cols, extra, outs):
        _head_norm_rope_store(accs[0], cols, extra[0], extra[1:], outs[0], outs[1])

    def raw_epilogue(accs, cols, extra, outs):
        outs[2][:, cols] = accs[0]

    def k_epilogue(accs, cols, extra, outs):
        _head_norm_rope_store(accs[0], cols, extra[0], extra[1:], None, outs[3])

    def v_epilogue(accs, cols, extra, outs):
        outs[3][:, cols] = accs[0].astype(BF16)

    def gate_epilogue(accs, cols, extra, outs):
        if cols.start == 0:
            outs[4][...] = jax.nn.sigmoid(accs[0][:, :HEAD_DIM]).T

    def block(first, count):
        return pl.BlockSpec((tm, tn), lambda i, j: (i, jnp.clip(j - first, 0, count - 1)))

    tab = pl.BlockSpec((tm, HEAD_DIM), lambda i, j: (i, 0))
    q_shape = jax.ShapeDtypeStruct((m, d), BF16)
    q_cmp, q_rot, raw, kv, gates_t = norm_matmul(
        h, gain, w_proj, [lambda i, j: (0, j)], gate0 + 1, tn,
        extra=[(gain_row, pl.BlockSpec((1, tn), lambda i, j: (0, j)))] + [(t, tab) for t in tables],
        outs=[(q_shape, block(0, raw0)), (q_shape, block(0, raw0)),
              (jax.ShapeDtypeStruct((m, 2 * gwk), F32), block(raw0, 2)),
              (jax.ShapeDtypeStruct((m, 4 * gwk), BF16), block(kv0, 4)),
              (jax.ShapeDtypeStruct((HEAD_DIM, m), F32), pl.BlockSpec((HEAD_DIM, tm), lambda i, j: (0, i)))],
        variants=[(lambda j: j < raw0, q_epilogue),
                  (lambda j: (j >= raw0) & (j < kv0), raw_epilogue),
                  (lambda j: (j >= kv0) & (j < gate0) & ((j - kv0) % 2 == 0), k_epilogue),
                  (lambda j: (j >= kv0) & (j < gate0) & ((j - kv0) % 2 == 1), v_epilogue),
                  (lambda j: j == gate0, gate_epilogue)],
        sub=2, tm=tm)

    k_cmp, v_cmp = nsa_compress(raw, cmp_pos, cmp_w1, cmp_w2, k_gain[0], batch, seq)
    o_cmp, sel = nsa_cmp_select(q_cmp, k_cmp, v_cmp, gates_t, batch, seq)
    o = nsa_attention(q_rot, kv, sel, gates_t, o_cmp, batch, seq)
    tmo = min(512, m)
    return out_matmul([(o, pl.BlockSpec((tmo, d), lambda i: (i, 0)))],
                      lambda ins: ins[0][...], w_o, h, tm=tmo)


def kernel(x, p, positions, mixer_norm, mlp_norm, mlp_w1, mlp_w2, ple_norm, ple_gate, ple_proj,
           moba_w_qkv, moba_q_gain, moba_k_gain, moba_w_o, pool_w, pool_scale,
           nsa_w_q, nsa_w_kv, nsa_q_gain, nsa_k_gain, nsa_cmp_pos, nsa_cmp_w1, nsa_cmp_w2,
           nsa_w_gate, nsa_w_o, conv_w_in, conv_w, conv_b, conv_w_o):
    batch, seq, d = x.shape
    depth = p.shape[0]
    m = batch * seq
    n_heads = d // HEAD_DIM
    gwk = NSA_KV_GROUPS * HEAD_DIM
    tables = rope_tables(positions)
    bf = lambda w: w.astype(BF16)
    ple_gate_bf, ple_proj_bf = bf(ple_gate), bf(ple_proj)
    p_rows = p.reshape(depth, m, -1)

    h = x.reshape(m, d)
    for i in range(depth):
        kind, j = i % 4, i // 4
        if kind == 0:
            h = moba_layer(h, mixer_norm[i], bf(moba_w_qkv[j]), moba_q_gain[j], moba_k_gain[j],
                           bf(moba_w_o[j]), tables, batch, seq)
        elif kind == 1:
            h = pool_layer(h, mixer_norm[i], bf(pool_w[j]), pool_scale[j], seq)
        elif kind == 2:
            w_gate = jnp.pad(nsa_w_gate[j], ((0, 0), (0, gwk - 3 * n_heads)))
            w_proj = bf(jnp.concatenate([nsa_w_q[j], nsa_w_kv[j], w_gate], axis=1))
            h = nsa_layer(h, mixer_norm[i], w_proj, nsa_q_gain[j], nsa_k_gain[j], nsa_cmp_pos[j],
                          bf(nsa_cmp_w1[j]), bf(nsa_cmp_w2[j]), bf(nsa_w_o[j]), tables, batch, seq)
        else:
            h = conv_layer(h, mixer_norm[i], bf(conv_w_in[j]), conv_w[j], conv_b[j], bf(conv_w_o[j]), seq)
        h = mlp_layer(h, mlp_norm[i], mlp_w1, mlp_w2, i)
        h = ple_layer(h, ple_norm[i], ple_gate_bf, p_rows, ple_proj_bf, i)
    return h.reshape(batch, seq, d)
```

```python
import functools

import jax
import jax.numpy as jnp
from jax import lax
from jax.experimental import pallas as pl
from jax.experimental.pallas import tpu as pltpu

F32 = jnp.float32
BF16 = jnp.bfloat16

HEAD_DIM = 128
ROT_DIM = HEAD_DIM // 4
ROPE_THETA = 500000.0
NORM_EPS = 1e-6
MOBA_BLOCK = 256
MOBA_TOPK = 3
POOL_WINDOWS = (2, 4, 8, 16)
NSA_KV_GROUPS = 4
NSA_CMP_LEN = 32
NSA_CMP_STRIDE = 16
NSA_SLC_LEN = 64
NSA_SLC_TOPK = 16
NSA_WINDOW = 512
CONV_WIDTH = 3

MASKED_LOGIT = -1e30
LN2 = 0.6931471805599453
SOFTMAX_EXP2_SCALE = HEAD_DIM ** -0.5 / LN2
KV_CHUNK = 256
MIB = 1024 * 1024


def _params(semantics, vmem_mib):
    return pltpu.CompilerParams(dimension_semantics=semantics,
                                vmem_limit_bytes=vmem_mib * MIB)


def _resident(shape, index_map):
    return pl.BlockSpec(shape, index_map, pipeline_mode=pl.Buffered(1))


def _rms(x, gain):
    ms = jnp.mean(x * x, axis=-1, keepdims=True)
    return x * lax.rsqrt(ms + NORM_EPS) * gain


def _split_bf16(x):
    hi = x.astype(BF16)
    return hi, (x - hi.astype(F32)).astype(BF16)


def _rope(x, cos, sin):
    half = ROT_DIM // 2
    lane = lax.broadcasted_iota(jnp.int32, x.shape, 1)
    partner = jnp.where(lane < half, -pltpu.roll(x, HEAD_DIM - half, 1), pltpu.roll(x, half, 1))
    return x * cos + partner * sin


def _nt_dot(a, b):
    return lax.dot_general(a, b, (((1,), (1,)), ((), ())), preferred_element_type=F32)


def _rope_table_body(pos_ref, freq_ref, cos_ref, sin_ref):
    ang = pos_ref[...].astype(F32) * freq_ref[...]
    cos_ref[...] = jnp.cos(ang)
    sin_ref[...] = jnp.sin(ang)


def rope_tables(positions):
    m = positions.size
    half = ROT_DIM // 2
    freqs = jnp.float32(ROPE_THETA) ** (-jnp.arange(half, dtype=F32) * 2.0 / ROT_DIM)
    freq_row = jnp.concatenate([freqs, freqs, jnp.zeros((HEAD_DIM - ROT_DIM,), F32)])[None]
    tm = min(m, 1024)
    tab = pl.BlockSpec((tm, HEAD_DIM), lambda i: (i, 0))
    return pl.pallas_call(
        _rope_table_body,
        grid=(m // tm,),
        in_specs=[pl.BlockSpec((tm, 1), lambda i: (i, 0)), pl.BlockSpec((1, HEAD_DIM), lambda i: (0, 0))],
        out_specs=[tab, tab],
        out_shape=[jax.ShapeDtypeStruct((m, HEAD_DIM), F32)] * 2,
        compiler_params=_params(("parallel",), 32),
        name="rope_tables",
    )(positions.reshape(m, 1), freq_row)


def _nmm_body(*refs, n_w, n_extra, n_out, sub, variants, on_row_start):
    x_ref, g_ref = refs[0], refs[1]
    w_refs = refs[2:2 + n_w]
    extra = refs[2 + n_w:2 + n_w + n_extra]
    outs = refs[2 + n_w + n_extra:2 + n_w + n_extra + n_out]
    xn_ref = refs[2 + n_w + n_extra + n_out]
    extra = extra + refs[3 + n_w + n_extra + n_out:]
    j = pl.program_id(1)

    @pl.when(j == 0)
    def _():
        xn_ref[...] = _rms(x_ref[...], g_ref[...]).astype(BF16)
        if on_row_start is not None:
            on_row_start(extra)

    width = w_refs[0].shape[1] // sub
    cols = [slice(s * width, (s + 1) * width) for s in range(sub)]

    def run(epilogue):
        xn = xn_ref[...]
        accs = [[jnp.dot(xn, w[:, c], preferred_element_type=F32) for w in w_refs] for c in cols]
        for c, acc in zip(cols, accs):
            epilogue(acc, c, extra, outs)

    if len(variants) == 1:
        run(variants[0][1])
    else:
        for applies, epilogue in variants:
            pl.when(applies(j))(functools.partial(run, epilogue))


def norm_matmul(h, gain, w, w_maps, n_col_tiles, tn, extra, outs, variants, sub=1, tm=1024, vmem_mib=56,
                scratch=(), on_row_start=None):
    m, d = h.shape
    tm = min(tm, m)
    in_specs = [pl.BlockSpec((tm, d), lambda i, j: (i, 0)),
                pl.BlockSpec((1, d), lambda i, j: (0, 0))]
    in_specs += [pl.BlockSpec((d, tn), wm) for wm in w_maps]
    in_specs += [spec for _, spec in extra]
    return pl.pallas_call(
        functools.partial(_nmm_body, n_w=len(w_maps), n_extra=len(extra), n_out=len(outs), sub=sub,
                          variants=variants, on_row_start=on_row_start),
        grid=(m // tm, n_col_tiles),
        in_specs=in_specs,
        out_specs=[spec for _, spec in outs],
        out_shape=[shape for shape, _ in outs],
        scratch_shapes=[pltpu.VMEM((tm, d), BF16)] + list(scratch),
        compiler_params=_params(("arbitrary" if scratch else "parallel", "arbitrary"), vmem_mib),
        name="norm_matmul",
    )(h, gain.reshape(1, d), *([w] * len(w_maps)), *[a for a, _ in extra])


def _head_norm_rope_store(acc, cols, gain_ref, table_refs, out_plain, out_rot):
    width = acc.shape[1]
    same_head = (lax.broadcasted_iota(jnp.int32, (width, width), 0) // HEAD_DIM
                 == lax.broadcasted_iota(jnp.int32, (width, width), 1) // HEAD_DIM)
    ones = jnp.where(same_head, 1.0, 0.0).astype(BF16)
    sq_hi, sq_lo = _split_bf16(acc * acc)
    ssq = (jnp.dot(sq_hi, ones, preferred_element_type=F32)
           + jnp.dot(sq_lo, ones, preferred_element_type=F32))
    normed = acc * lax.rsqrt(ssq * (1.0 / HEAD_DIM) + NORM_EPS) * gain_ref[:, cols]
    if out_plain is not None:
        out_plain[:, cols] = normed.astype(BF16)
    if out_rot is not None:
        cos, sin = table_refs[0][...], table_refs[1][...]
        for hh in range(width // HEAD_DIM):
            sl = slice(hh * HEAD_DIM, (hh + 1) * HEAD_DIM)
            dst = slice(cols.start + hh * HEAD_DIM, cols.start + (hh + 1) * HEAD_DIM)
            out_rot[:, dst] = _rope(normed[:, sl], cos, sin).astype(BF16)


def _omm_body(*refs, n_in, prologue):
    ins = refs[:n_in]
    w_ref, res_ref, o_ref = refs[n_in:n_in + 3]
    a = prologue(ins)
    o_ref[...] = res_ref[...] + jnp.dot(a, w_ref[...], preferred_element_type=F32)


def out_matmul(ins, prologue, w, res, tm=512, vmem_mib=48):
    m, d = res.shape
    tm = min(tm, m)
    k = w.shape[0]
    return pl.pallas_call(
        functools.partial(_omm_body, n_in=len(ins), prologue=prologue),
        grid=(m // tm,),
        in_specs=[spec for _, spec in ins] + [
            _resident((k, d), lambda i: (0, 0)),
            pl.BlockSpec((tm, d), lambda i: (i, 0))],
        out_specs=pl.BlockSpec((tm, d), lambda i: (i, 0)),
        out_shape=jax.ShapeDtypeStruct((m, d), F32),
        compiler_params=_params(("parallel",), vmem_mib),
        name="out_matmul",
    )(*[a for a, _ in ins], w, res)


def _mlp_body(x_ref, g_ref, w1_ref, w2_ref, o_ref, xn_ref):
    @pl.when(pl.program_id(1) == 0)
    def _():
        x = x_ref[...]
        xn_ref[...] = _rms(x, g_ref[...]).astype(BF16)
        o_ref[...] = x

    a = jnp.dot(xn_ref[...], w1_ref[...].astype(BF16), preferred_element_type=F32)
    a = jnp.square(jnp.maximum(a, 0.0)).astype(BF16)
    o_ref[...] += jnp.dot(a, w2_ref[...].astype(BF16), preferred_element_type=F32)


def mlp_layer(h, gain, w1, w2, layer, tm=1024, tf=512):
    m, d = h.shape
    dff = w1.shape[2]
    tm = min(tm, m)
    return pl.pallas_call(
        _mlp_body,
        grid=(m // tm, dff // tf),
        in_specs=[pl.BlockSpec((tm, d), lambda i, f: (i, 0)),
                  pl.BlockSpec((1, d), lambda i, f: (0, 0)),
                  pl.BlockSpec((None, d, tf), lambda i, f: (layer, 0, f)),
                  pl.BlockSpec((None, tf, d), lambda i, f: (layer, f, 0))],
        out_specs=pl.BlockSpec((tm, d), lambda i, f: (i, 0)),
        out_shape=jax.ShapeDtypeStruct((m, d), F32),
        scratch_shapes=[pltpu.VMEM((tm, d), BF16)],
        compiler_params=_params(("parallel", "arbitrary"), 56),
        name="mlp",
    )(h, gain.reshape(1, d), w1, w2)


def _ple_body(x_ref, g_ref, wg_ref, p_ref, wp_ref, o_ref):
    x = x_ref[...]
    xn = _rms(x, g_ref[...]).astype(BF16)
    gate = jax.nn.sigmoid(jnp.dot(xn, wg_ref[...], preferred_element_type=F32))
    emb = jnp.dot(p_ref[...].astype(BF16), wp_ref[...], preferred_element_type=F32)
    o_ref[...] = x + gate * emb


def ple_layer(h, gain, wg, p, wp, layer, tm=512):
    m, d = h.shape
    pd = p.shape[2]
    tm = min(tm, m)
    return pl.pallas_call(
        _ple_body,
        grid=(m // tm,),
        in_specs=[pl.BlockSpec((tm, d), lambda i: (i, 0)),
                  pl.BlockSpec((1, d), lambda i: (0, 0)),
                  _resident((None, d, d), lambda i: (layer, 0, 0)),
                  pl.BlockSpec((None, tm, pd), lambda i: (layer, i, 0)),
                  _resident((None, pd, d), lambda i: (layer, 0, 0))],
        out_specs=pl.BlockSpec((tm, d), lambda i: (i, 0)),
        out_shape=jax.ShapeDtypeStruct((m, d), F32),
        compiler_params=_params(("parallel",), 48),
        name="ple",
    )(h, gain.reshape(1, d), wg, p, wp)


def _flash_buffers(chains, keys, queries):
    return [pltpu.VMEM((chains, keys, queries), F32), pltpu.VMEM((chains, keys, queries), BF16),
            pltpu.VMEM((chains, HEAD_DIM, queries), F32)]


def _flash_keep(s_ref, s, bias):
    if bias.shape[0] == 1:
        s_ref[...] = s
        raw_max = jnp.max(s, axis=0, keepdims=True)
        return raw_max + bias, raw_max - 2.0 * bias
    s = s + bias
    s_ref[...] = s
    s_max = jnp.max(s, axis=0, keepdims=True)
    return s_max, s_max


def _flash_start(buf, k, q_t, bias):
    s_ref, p_ref, acc_ref = buf
    p_ref[...] = jnp.zeros_like(p_ref)
    acc_ref[...] = jnp.zeros_like(acc_ref)
    queries = s_ref.shape[1]
    return _flash_keep(s_ref, jnp.dot(k, q_t, preferred_element_type=F32), bias) + (
        jnp.ones((1, queries), F32), jnp.full((1, queries), MASKED_LOGIT, F32), jnp.zeros((1, queries), F32))


def _flash_stage(bufs, states, v_prev, following):
    products = [jnp.dot(v, buf[1][...], preferred_element_type=F32) for v, buf in zip(v_prev, bufs)]
    if following is not None:
        upcoming = [jnp.dot(k, q_t, preferred_element_type=F32) for k, q_t, _ in following]
    out = []
    for c, ((s_ref, p_ref, acc_ref), state, pv) in enumerate(zip(bufs, states, products)):
        s_max, floor, alpha_prev, m, l = state
        m_new = jnp.maximum(m, s_max)
        p = jnp.exp2(s_ref[...] - jnp.maximum(m_new, floor))
        alpha = jnp.exp((m - m_new) * LN2)
        l = alpha * l + jnp.sum(p, axis=0, keepdims=True)
        p_ref[...] = p.astype(BF16)
        acc_ref[...] = alpha_prev * acc_ref[...] + pv
        kept = (s_max, floor) if following is None else _flash_keep(s_ref, upcoming[c], following[c][2])
        out.append(kept + (alpha, m_new, l))
    return tuple(out)


def _flash_finish(buf, state, v_last):
    _, p_ref, acc_ref = buf
    alpha, l = state[2], state[4]
    return (alpha * acc_ref[...] + jnp.dot(v_last, p_ref[...], preferred_element_type=F32)) / l


def _store_transposed(v_ref, vt_ref, chunk):
    for n in range(vt_ref.shape[0]):
        vt_ref[n] = v_ref[n * chunk:(n + 1) * chunk, :].astype(F32).T.astype(BF16)


def _top_rank(v, row):
    n = v.shape[0]
    if n % 8:
        rank = jnp.zeros(v.shape, jnp.int32)
        for j in range(n):
            r = v[j:j + 1, :]
            rank = rank + jnp.where(row > j, jnp.where(r >= v, 1, 0), jnp.where(r > v, 1, 0))
        return rank
    starts = range(0, n, 8)
    groups = [v[lo:lo + 8, :] for lo in starts]
    ranks = [jnp.zeros((8, v.shape[1]), jnp.int32) for _ in starts]
    for j in range(n):
        r = jnp.broadcast_to(v[j:j + 1, :], (8, v.shape[1]))
        for g, lo in enumerate(starts):
            if lo > j:
                ranks[g] = ranks[g] + jnp.where(r >= groups[g], 1, 0)
            elif lo + 7 <= j:
                ranks[g] = ranks[g] + jnp.where(r > groups[g], 1, 0)
            else:
                below = lax.broadcasted_iota(jnp.int32, (8, v.shape[1]), 0) > j - lo
                ranks[g] = ranks[g] + jnp.where(below, jnp.where(r >= groups[g], 1, 0),
                                                jnp.where(r > groups[g], 1, 0))
    return jnp.concatenate(ranks, axis=0)


def _moba_body(q_ref, k_ref, v_ref, o_ref, kmean_ref, vt_ref, bias_ref, s_ref, p_ref, acc_ref,
               *, n_blocks, heads):
    blk = MOBA_BLOCK
    own = pl.program_id(2)
    cols = [slice(e * HEAD_DIM, (e + 1) * HEAD_DIM) for e in range(heads)]
    bufs = [(s_ref.at[e], p_ref.at[e], acc_ref.at[e]) for e in range(heads)]

    @pl.when(own == 0)
    def _():
        for e in range(heads):
            for n in range(n_blocks):
                kb = k_ref[n * blk:(n + 1) * blk, cols[e]].astype(F32)
                kmean_ref[e, n:n + 1, :] = jnp.mean(kb, axis=0, keepdims=True)
                vt_ref[e, n] = v_ref[n * blk:(n + 1) * blk, cols[e]].astype(F32).T.astype(BF16)

    q_ts = [q_ref[:, cols[e]].astype(F32).T.astype(BF16) for e in range(heads)]
    for e in range(heads):
        gate = jnp.dot(kmean_ref[e].astype(BF16), q_ts[e], preferred_element_type=F32)
        row = lax.broadcasted_iota(jnp.int32, gate.shape, 0)
        gate = jnp.where(row < own, gate, -jnp.inf)
        rank = _top_rank(gate, row)
        bias = jnp.where(row < own, jnp.where(rank < MOBA_TOPK, 0.0, MASKED_LOGIT), MASKED_LOGIT)
        for n in range(n_blocks):
            bias_ref[e, n] = jnp.broadcast_to(bias[n:n + 1, :], (8, blk))

    def keys(j):
        start = pl.multiple_of(j * blk, blk)
        return [k_ref[pl.ds(start, blk), cols[e]] for e in range(heads)]

    def values(j):
        return [vt_ref[e, j] for e in range(heads)]

    def stage(j, states, next_biases):
        following = None if next_biases is None else list(zip(keys(j + 1), q_ts, next_biases))
        return _flash_stage(bufs, states, values(jnp.maximum(j - 1, 0)), following)

    def chosen_bias(j):
        return [jnp.max(bias_ref[e, j], axis=0, keepdims=True) for e in range(heads)]

    causal = jnp.where(lax.broadcasted_iota(jnp.int32, (blk, blk), 0)
                       <= lax.broadcasted_iota(jnp.int32, (blk, blk), 1), 0.0, MASKED_LOGIT)
    first_bias = [jnp.where(own == 0, causal, b) for b in chosen_bias(0)]
    states = tuple(_flash_start(buf, k, q_t, b) for buf, k, q_t, b in zip(bufs, keys(0), q_ts, first_bias))
    states = lax.fori_loop(0, own - 1, lambda j, st: stage(j, st, chosen_bias(j + 1)), states)
    states = lax.cond(own > 0, lambda st: stage(own - 1, st, [causal] * heads), lambda st: st, states)
    states = stage(own, states, None)
    for e, v_last in enumerate(values(own)):
        o_ref[:, cols[e]] = _flash_finish(bufs[e], states[e], v_last).T.astype(BF16)


def moba_attention(qkv, batch, seq, n_heads, heads_per_step=4):
    blk = MOBA_BLOCK
    nq = seq // blk
    hp = heads_per_step
    width = hp * HEAD_DIM
    groups = n_heads // hp
    return pl.pallas_call(
        functools.partial(_moba_body, n_blocks=nq, heads=hp),
        grid=(batch, groups, nq),
        in_specs=[pl.BlockSpec((blk, width), lambda b, h, i: (b * nq + i, h)),
                  pl.BlockSpec((seq, width), lambda b, h, i: (b, groups + h)),
                  pl.BlockSpec((seq, width), lambda b, h, i: (b, 2 * groups + h))],
        out_specs=pl.BlockSpec((blk, width), lambda b, h, i: (b * nq + i, h)),
        out_shape=jax.ShapeDtypeStruct((batch * seq, n_heads * HEAD_DIM), BF16),
        scratch_shapes=[pltpu.VMEM((hp, nq, HEAD_DIM), F32),
                        pltpu.VMEM((hp, nq, HEAD_DIM, blk), BF16),
                        pltpu.VMEM((hp, nq, 8, blk), F32)] + _flash_buffers(hp, blk, blk),
        compiler_params=_params(("parallel", "parallel", "arbitrary"), 32),
        name="moba_attention",
    )(qkv, qkv, qkv)


def moba_layer(h, gain, w_qkv, q_gain, k_gain, w_o, tables, batch, seq):
    m, d = h.shape
    n_heads = d // HEAD_DIM
    tn = 1024
    n_qk_tiles = 2 * d // tn
    gain_row = jnp.concatenate([jnp.tile(q_gain, n_heads), jnp.tile(k_gain * SOFTMAX_EXP2_SCALE, n_heads),
                                jnp.ones((d,), F32)])[None]

    def qk_epilogue(accs, cols, extra, outs):
        _head_norm_rope_store(accs[0], cols, extra[0], extra[1:], None, outs[0])

    def v_epilogue(accs, cols, extra, outs):
        outs[0][:, cols] = accs[0].astype(BF16)

    tm = min(1024, m)
    tab = pl.BlockSpec((tm, HEAD_DIM), lambda i, j: (i, 0))
    (qkv,) = norm_matmul(
        h, gain, w_qkv, [lambda i, j: (0, j)], 3 * d // tn, tn,
        extra=[(gain_row, pl.BlockSpec((1, tn), lambda i, j: (0, j)))] + [(t, tab) for t in tables],
        outs=[(jax.ShapeDtypeStruct((m, 3 * d), BF16), pl.BlockSpec((tm, tn), lambda i, j: (i, j)))],
        variants=[(lambda j: j < n_qk_tiles, qk_epilogue), (lambda j: j >= n_qk_tiles, v_epilogue)],
        sub=4, tm=tm)
    o = moba_attention(qkv, batch, seq, n_heads)
    tmo = min(512, m)
    return out_matmul([(o, pl.BlockSpec((tmo, d), lambda i: (i, 0)))],
                      lambda ins: ins[0][...], w_o, h, tm=tmo)


def _pool_body(x_ref, halo_ref, g_ref, w_ref, s_ref, o_ref, *, tiles_per_seq, halo):
    i = pl.program_id(0)
    tm, d = x_ref.shape
    group = d // len(POOL_WINDOWS)
    x = x_ref[...]
    gain = g_ref[...]
    xn = _rms(x, gain)
    prev = jnp.where(i % tiles_per_seq == 0, 0.0, _rms(halo_ref[...], gain))
    pos = (i % tiles_per_seq) * tm + lax.broadcasted_iota(jnp.int32, (tm, 1), 0)
    for g, win in enumerate(POOL_WINDOWS):
        sl = slice(g * group, (g + 1) * group)
        run = jnp.concatenate([prev[:, sl], xn[:, sl]], axis=0)
        span = 1
        while span < win:
            run = run + pltpu.roll(run, span, 0)
            span *= 2
        cnt = jnp.minimum(pos + 1, win).astype(F32)
        mean = run[halo:, :] / cnt
        mix = jnp.dot((mean - xn[:, sl]).astype(BF16), w_ref[g], preferred_element_type=F32)
        o_ref[:, sl] = x[:, sl] + mix * s_ref[:, sl]


def pool_layer(h, gain, w_groups, scale, seq, tm=512):
    m, d = h.shape
    halo = 16
    assert max(POOL_WINDOWS) <= halo
    tm = min(tm, seq)
    group = d // len(POOL_WINDOWS)
    return pl.pallas_call(
        functools.partial(_pool_body, tiles_per_seq=seq // tm, halo=halo),
        grid=(m // tm,),
        in_specs=[pl.BlockSpec((tm, d), lambda i: (i, 0)),
                  pl.BlockSpec((halo, d), lambda i: (jnp.maximum(i * (tm // halo) - 1, 0), 0)),
                  pl.BlockSpec((1, d), lambda i: (0, 0)),
                  _resident((len(POOL_WINDOWS), group, group), lambda i: (0, 0, 0)),
                  pl.BlockSpec((1, d), lambda i: (0, 0))],
        out_specs=pl.BlockSpec((tm, d), lambda i: (i, 0)),
        out_shape=jax.ShapeDtypeStruct((m, d), F32),
        compiler_params=_params(("parallel",), 48),
        name="pool_mixer",
    )(h, h, gain.reshape(1, d), w_groups, scale.reshape(1, d))


def conv_layer(h, gain, w_in, conv_w, conv_b, w_o, seq):
    m, d = h.shape
    tn = 512
    nj = d // tn

    tm = min(1024, seq)
    tiles_per_seq = seq // tm
    halo = 8

    def on_row_start(extra):
        @pl.when(pl.program_id(0) % tiles_per_seq == 0)
        def _():
            extra[2][...] = jnp.zeros_like(extra[2])

    def epilogue(accs, cols, extra, outs):
        cw_ref, cb_ref, carry_ref = extra
        j = pl.program_id(1)
        u0 = accs[1] * accs[2]
        prev = carry_ref[j, :, cols]
        row = lax.broadcasted_iota(jnp.int32, u0.shape, 0)
        u1 = jnp.where(row == 0, prev[halo - 1:halo, :], pltpu.roll(u0, 1, 0))
        u2 = jnp.where(row == 0, prev[halo - 2:halo - 1, :],
                       jnp.where(row == 1, prev[halo - 1:halo, :], pltpu.roll(u0, 2, 0)))
        conv = (cw_ref[0:1, cols] * u2 + cw_ref[1:2, cols] * u1 + cw_ref[2:3, cols] * u0
                + cb_ref[:, cols])
        outs[0][:, cols] = (accs[0] * conv).astype(BF16)
        carry_ref[j, :, cols] = u0[tm - halo:, :]

    (y,) = norm_matmul(
        h, gain, w_in, [lambda i, j: (0, j), lambda i, j: (0, j + nj), lambda i, j: (0, j + 2 * nj)],
        nj, tn,
        extra=[(conv_w, pl.BlockSpec((CONV_WIDTH, tn), lambda i, j: (0, j))),
               (conv_b.reshape(1, d), pl.BlockSpec((1, tn), lambda i, j: (0, j)))],
        outs=[(jax.ShapeDtypeStruct((m, d), BF16), pl.BlockSpec((tm, tn), lambda i, j: (i, j)))],
        variants=[(None, epilogue)], sub=2, tm=tm,
        scratch=[pltpu.VMEM((nj, halo, tn), F32)], on_row_start=on_row_start)

    tmo = min(512, m)
    return out_matmul([(y, pl.BlockSpec((tmo, d), lambda i: (i, 0)))],
                      lambda ins: ins[0][...], w_o, h, tm=tmo)


def _compress_body(x_ref, pos_ref, w1_ref, w2_ref, g_ref, kc_ref, vc_ref, lo_ref, hi_ref, *, groups):
    step = pl.program_id(1)
    stride = NSA_CMP_STRIDE

    @pl.when(step == 0)
    def _():
        lo_ref[...] = jnp.zeros_like(lo_ref)
        hi_ref[...] = jnp.zeros_like(hi_ref)

    for kv in range(2):
        p_lo = pos_ref[kv, pl.ds(step, 1), :]
        p_hi = pos_ref[kv, pl.ds(stride + step, 1), :]
        w_lo = w1_ref[kv, step]
        w_hi = w1_ref[kv, stride + step]
        for g in range(groups):
            c = kv * groups + g
            t = x_ref[:, c * HEAD_DIM:(c + 1) * HEAD_DIM]
            lo_ref[c] += jnp.dot((t + p_lo).astype(BF16), w_lo, preferred_element_type=F32)
            hi_ref[c] += jnp.dot((t + p_hi).astype(BF16), w_hi, preferred_element_type=F32)

    @pl.when(step == stride - 1)
    def _():
        n_half = lo_ref.shape[1]
        for kv in range(2):
            for g in range(groups):
                c = kv * groups + g
                pre = lo_ref[c] + pltpu.roll(hi_ref[c], n_half - 1, 0)
                hid = jax.nn.gelu(pre)
                if kv == 0:
                    out = jnp.dot(hid.astype(BF16), w2_ref[0], preferred_element_type=F32)
                    kc_ref[g] = _rms(out, g_ref[...]).astype(BF16)
                else:
                    vc_ref[g] = jnp.dot(w2_ref[1], hid.T.astype(BF16),
                                        preferred_element_type=F32).astype(BF16)


def nsa_compress(raw, cmp_pos, cmp_w1, cmp_w2, k_gain0, batch, seq):
    assert NSA_CMP_LEN == 2 * NSA_CMP_STRIDE
    stride = NSA_CMP_STRIDE
    groups = NSA_KV_GROUPS
    width = raw.shape[1]
    n_half = seq // stride
    x = raw.reshape(batch * n_half, stride * width)
    w1 = cmp_w1.reshape(2, NSA_CMP_LEN, HEAD_DIM, HEAD_DIM)
    cmp_w2 = jnp.stack([cmp_w2[0], cmp_w2[1].T])
    k_out = jax.ShapeDtypeStruct((batch, groups, n_half, HEAD_DIM), BF16)
    v_out = jax.ShapeDtypeStruct((batch, groups, HEAD_DIM, n_half), BF16)
    k_spec = pl.BlockSpec((None, groups, n_half, HEAD_DIM), lambda b, s: (b, 0, 0, 0))
    v_spec = pl.BlockSpec((None, groups, HEAD_DIM, n_half), lambda b, s: (b, 0, 0, 0))
    return pl.pallas_call(
        functools.partial(_compress_body, groups=groups),
        grid=(batch, stride),
        in_specs=[pl.BlockSpec((n_half, width), lambda b, s: (b, s)),
                  pl.BlockSpec((2, NSA_CMP_LEN, HEAD_DIM), lambda b, s: (0, 0, 0)),
                  pl.BlockSpec((2, NSA_CMP_LEN, HEAD_DIM, HEAD_DIM), lambda b, s: (0, 0, 0, 0)),
                  pl.BlockSpec((2, HEAD_DIM, HEAD_DIM), lambda b, s: (0, 0, 0)),
                  pl.BlockSpec((1, HEAD_DIM), lambda b, s: (0, 0))],
        out_specs=[k_spec, v_spec],
        out_shape=[k_out, v_out],
        scratch_shapes=[pltpu.VMEM((2 * groups, n_half, HEAD_DIM), F32)] * 2,
        compiler_params=_params(("parallel", "arbitrary"), 32),
        name="nsa_compress",
    )(x, cmp_pos, w1, cmp_w2, k_gain0.reshape(1, HEAD_DIM))


def _cmp_select_body(q_ref, kc_ref, vct_ref, gate_ref, o_ref, sel_ref, *, n_cmp, n_top, q_per_kv):
    tq = q_ref.shape[0]
    first_head = pl.program_id(1) * q_per_kv
    n_pad = kc_ref.shape[0]
    n_slc = sel_ref.shape[0]
    scale = HEAD_DIM ** -0.5
    t = pl.program_id(2) * tq + lax.broadcasted_iota(jnp.int32, (1, tq), 1)
    n = lax.broadcasted_iota(jnp.int32, (n_pad, 1), 0)
    bias = jnp.where(n < n_cmp, jnp.where(n * NSA_CMP_STRIDE + (NSA_CMP_LEN - 1) <= t, 0.0, MASKED_LOGIT),
                     MASKED_LOGIT)
    seen = t >= NSA_CMP_LEN - 1
    kc = kc_ref[...]
    vct = vct_ref[...]
    heads = [slice(r * HEAD_DIM, (r + 1) * HEAD_DIM) for r in range(q_per_kv)]
    logits = [_nt_dot(kc, q_ref[:, sl]) for sl in heads]
    weights, norms = [], []
    for s in logits:
        s = s * scale + bias
        e = jnp.exp(s - jnp.max(s, axis=0, keepdims=True))
        weights.append(e)
        norms.append(jnp.where(seen, 1.0 / jnp.sum(e, axis=0, keepdims=True), 0.0))
    products = [jnp.dot(vct, e.astype(BF16), preferred_element_type=F32) for e in weights]
    p_sum = jnp.zeros((n_pad, tq), F32)
    for r, sl in enumerate(heads):
        gate = gate_ref[pl.ds(3 * (first_head + r), 1), :]
        o_ref[:, sl] = (products[r] * (norms[r] * gate)).T.astype(BF16)
        p_sum = p_sum + weights[r] * norms[r]

    jj = lax.broadcasted_iota(jnp.int32, (n_slc, n_pad), 0) * NSA_SLC_LEN
    nn = lax.broadcasted_iota(jnp.int32, (n_slc, n_pad), 1) * NSA_CMP_STRIDE
    overlap = jnp.where((nn < jj + NSA_SLC_LEN) & (jj < nn + NSA_CMP_LEN)
                        & (nn < n_cmp * NSA_CMP_STRIDE), 1.0, 0.0).astype(BF16)
    p_hi = p_sum.astype(BF16)
    p_lo = (p_sum - p_hi.astype(F32)).astype(BF16)
    imp = (jnp.dot(overlap, p_hi, preferred_element_type=F32)
           + jnp.dot(overlap, p_lo, preferred_element_type=F32))

    cur = t // NSA_SLC_LEN
    jb = lax.broadcasted_iota(jnp.int32, (n_slc, tq), 0)
    val = jnp.where(jb == cur, jnp.inf,
                    jnp.where(jb == 0, jnp.inf, jnp.where(jb < cur, imp, -jnp.inf)))
    rank = _top_rank(val, jb)
    sel_ref[...] = jnp.where(rank < n_top, jnp.where(val > -jnp.inf, 1.0, 0.0), 0.0)


def nsa_cmp_select(q_cmp, k_cmp, v_cmp_t, gates_t, batch, seq, tq=256):
    m, d = q_cmp.shape
    groups = NSA_KV_GROUPS
    q_per_kv = d // HEAD_DIM // groups
    gw = q_per_kv * HEAD_DIM
    tq = min(tq, seq)
    nq = seq // tq
    n_pad = k_cmp.shape[2]
    n_cmp = (seq - NSA_CMP_LEN) // NSA_CMP_STRIDE + 1
    n_slc = seq // NSA_SLC_LEN
    return pl.pallas_call(
        functools.partial(_cmp_select_body, n_cmp=n_cmp, n_top=min(NSA_SLC_TOPK, n_slc),
                          q_per_kv=q_per_kv),
        grid=(batch, groups, nq),
        in_specs=[pl.BlockSpec((tq, gw), lambda b, g, i: (b * nq + i, g)),
                  pl.BlockSpec((None, None, n_pad, HEAD_DIM), lambda b, g, i: (b, g, 0, 0)),
                  pl.BlockSpec((None, None, HEAD_DIM, n_pad), lambda b, g, i: (b, g, 0, 0)),
                  pl.BlockSpec((gates_t.shape[0], tq), lambda b, g, i: (0, b * nq + i))],
        out_specs=[pl.BlockSpec((tq, gw), lambda b, g, i: (b * nq + i, g)),
                   pl.BlockSpec((None, None, n_slc, tq), lambda b, g, i: (b, g, 0, i))],
        out_shape=[jax.ShapeDtypeStruct((m, d), BF16),
                   jax.ShapeDtypeStruct((batch, groups, n_slc, seq), F32)],
        compiler_params=_params(("parallel", "parallel", "parallel"), 32),
        name="nsa_cmp_select",
    )(q_cmp, k_cmp, v_cmp_t, gates_t)


def _nsa_attn_body(q_ref, ks_ref, vs_ref, kw_ref, vw_ref, sel_ref, gate_ref, oc_ref, o_ref,
                   vst_ref, vwt_ref, bias_ref, s_ref, p_ref, acc_ref, *, q_per_kv, per):
    tq = q_ref.shape[0]
    kc = KV_CHUNK
    sub = kc // NSA_SLC_LEN
    step = pl.program_id(2)
    q0 = step * tq

    @pl.when(step == 0)
    def _():
        _store_transposed(vs_ref, vst_ref, kc)
        _store_transposed(vw_ref, vwt_ref, kc)

    n_chain = q_per_kv // per
    q_ts = [jnp.concatenate([q_ref[:, r * HEAD_DIM:(r + 1) * HEAD_DIM]
                             for r in range(c * per, (c + 1) * per)], axis=0).astype(F32).T.astype(BF16)
            for c in range(n_chain)]
    t = q0 + lax.broadcasted_iota(jnp.int32, (1, tq), 1)
    bias_ref[...] = jnp.where(sel_ref[...] > 0.5, 0.0, MASKED_LOGIT)

    def chain_bias(bias):
        return jnp.concatenate([bias] * per, axis=1)

    def selected_bias(c, kpos):
        picked = jnp.concatenate(
            [jnp.broadcast_to(bias_ref[pl.ds(c * sub + a, 1), :], (NSA_SLC_LEN, tq)) for a in range(sub)],
            axis=0)
        return chain_bias(jnp.where(kpos <= t, picked, MASKED_LOGIT))

    def window_bias(kpos):
        return chain_bias(jnp.where(kpos <= t, jnp.where(t - kpos < NSA_WINDOW, 0.0, MASKED_LOGIT),
                                    MASKED_LOGIT))

    def selected_logits(c):
        start = pl.multiple_of(c * kc, kc)
        kpos = start + lax.broadcasted_iota(jnp.int32, (kc, 1), 0)
        k, bias = ks_ref[pl.ds(start, kc), :], selected_bias(c, kpos)
        return [(k, q_t, bias) for q_t in q_ts]

    def window_logits(c):
        start = pl.multiple_of(c * kc, kc)
        kpos = start + lax.broadcasted_iota(jnp.int32, (kc, 1), 0)
        k, bias = kw_ref[pl.ds(start, kc), :], window_bias(kpos)
        return [(k, q_t, bias) for q_t in q_ts]

    bufs = [(s_ref.at[c], p_ref.at[c], acc_ref.at[c]) for c in range(2 * n_chain)]

    def selected_only(c, states):
        return _flash_stage(bufs[:n_chain], states, [vst_ref[jnp.maximum(c - 1, 0)]] * n_chain,
                            selected_logits(c + 1))

    def selected_and_window(c, states, is_last=False):
        prev = jnp.maximum(c - 1, 0)
        following = None if is_last else selected_logits(c + 1) + window_logits(c + 1)
        return _flash_stage(bufs, states, [vst_ref[prev]] * n_chain + [vwt_ref[prev]] * n_chain, following)

    first = jnp.maximum(q0 - (NSA_WINDOW - 1), 0) // kc
    last = (q0 + tq - 1) // kc
    states = tuple(_flash_start(buf, *f) for buf, f in zip(bufs[:n_chain], selected_logits(0)))
    states = lax.fori_loop(0, first, selected_only, states)
    states = states + tuple(_flash_start(buf, *f) for buf, f in zip(bufs[n_chain:], window_logits(first)))
    states = lax.fori_loop(first, last, selected_and_window, states)
    states = selected_and_window(last, states, is_last=True)

    first_head = pl.program_id(1) * q_per_kv
    for c in range(n_chain):
        selected = _flash_finish(bufs[c], states[c], vst_ref[last])
        window = _flash_finish(bufs[n_chain + c], states[n_chain + c], vwt_ref[last])
        for i in range(per):
            r = c * per + i
            lanes = slice(i * tq, (i + 1) * tq)
            cols = slice(r * HEAD_DIM, (r + 1) * HEAD_DIM)
            g_slc = gate_ref[pl.ds(3 * (first_head + r) + 1, 1), :]
            g_win = gate_ref[pl.ds(3 * (first_head + r) + 2, 1), :]
            mixed = (selected[:, lanes] * g_slc + window[:, lanes] * g_win).T
            o_ref[:, cols] = (mixed + oc_ref[:, cols].astype(F32)).astype(BF16)


def nsa_attention(q_rot, kv, sel, gates_t, o_cmp, batch, seq, tq=256):
    m, d = q_rot.shape
    groups = NSA_KV_GROUPS
    q_per_kv = d // HEAD_DIM // groups
    gw = q_per_kv * HEAD_DIM
    tq = min(tq, seq)
    nq = seq // tq
    n_slc = sel.shape[2]
    assert seq % KV_CHUNK == 0 and KV_CHUNK % NSA_SLC_LEN == 0

    def kv_spec(part):
        return pl.BlockSpec((seq, HEAD_DIM), lambda b, g, i: (b, part * groups + g))

    q_spec = pl.BlockSpec((tq, gw), lambda b, g, i: (b * nq + i, g))
    out = jax.ShapeDtypeStruct((m, d), BF16)
    v_t = pltpu.VMEM((seq // KV_CHUNK, HEAD_DIM, KV_CHUNK), BF16)
    per = 1
    return pl.pallas_call(
        functools.partial(_nsa_attn_body, q_per_kv=q_per_kv, per=per),
        grid=(batch, groups, nq),
        in_specs=[q_spec, kv_spec(0), kv_spec(1), kv_spec(2), kv_spec(3),
                  pl.BlockSpec((None, None, n_slc, tq), lambda b, g, i: (b, g, 0, i)),
                  pl.BlockSpec((gates_t.shape[0], tq), lambda b, g, i: (0, b * nq + i)),
                  q_spec],
        out_specs=q_spec,
        out_shape=out,
        scratch_shapes=[v_t, v_t, pltpu.VMEM((n_slc, tq), F32)]
        + _flash_buffers(2 * q_per_kv // per, KV_CHUNK, per * tq),
        compiler_params=_params(("parallel", "parallel", "arbitrary"), 32),
        name="nsa_attention",
    )(q_rot, kv, kv, kv, kv, sel, gates_t, o_cmp)


def nsa_layer(h, gain, w_proj, q_gain, k_gain, cmp_pos, cmp_w1, cmp_w2, w_o, tables, batch, seq):
    m, d = h.shape
    n_heads = d // HEAD_DIM
    groups = NSA_KV_GROUPS
    gwk = groups * HEAD_DIM
    tn = 512
    tm = min(1024, m)
    assert tn == gwk and 3 * n_heads <= HEAD_DIM
    raw0, kv0, gate0 = d // tn, d // tn + 2, d // tn + 6
    ones = jnp.ones((gwk,), F32)
    gain_row = jnp.concatenate([jnp.tile(q_gain, n_heads), ones, ones,
                                jnp.tile(k_gain[1] * SOFTMAX_EXP2_SCALE, groups), ones,
                                jnp.tile(k_gain[2] * SOFTMAX_EXP2_SCALE, groups), ones, ones])[None]

    def q_epilogue(accs, cols, extra, outs):
        _head_norm_rope_store(accs[0], cols, extra[0], extra[1:], outs[0], outs[1])

    def raw_epilogue(accs, cols, extra, outs):
        outs[2][:, cols] = accs[0]

    def k_epilogue(accs, cols, extra, outs):
        _head_norm_rope_store(accs[0], cols, extra[0], extra[1:], None, outs[3])

    def v_epilogue(accs, cols, extra, outs):
        outs[3][:, cols] = accs[0].astype(BF16)

    def gate_epilogue(accs, cols, extra, outs):
        if cols.start == 0:
            outs[4][...] = jax.nn.sigmoid(accs[0][:, :HEAD_DIM]).T

    def block(first, count):
        return pl.BlockSpec((tm, tn), lambda i, j: (i, jnp.clip(j - first, 0, count - 1)))

    tab = pl.BlockSpec((tm, HEAD_DIM), lambda i, j: (i, 0))
    q_shape = jax.ShapeDtypeStruct((m, d), BF16)
    q_cmp, q_rot, raw, kv, gates_t = norm_matmul(
        h, gain, w_proj, [lambda i, j: (0, j)], gate0 + 1, tn,
        extra=[(gain_row, pl.BlockSpec((1, tn), lambda i, j: (0, j)))] + [(t, tab) for t in tables],
        outs=[(q_shape, block(0, raw0)), (q_shape, block(0, raw0)),
              (jax.ShapeDtypeStruct((m, 2 * gwk), F32), block(raw0, 2)),
              (jax.ShapeDtypeStruct((m, 4 * gwk), BF16), block(kv0, 4)),
              (jax.ShapeDtypeStruct((HEAD_DIM, m), F32), pl.BlockSpec((HEAD_DIM, tm), lambda i, j: (0, i)))],
        variants=[(lambda j: j < raw0, q_epilogue),
                  (lambda j: (j >= raw0) & (j < kv0), raw_epilogue),
                  (lambda j: (j >= kv0) & (j < gate0) & ((j - kv0) % 2 == 0), k_epilogue),
                  (lambda j: (j >= kv0) & (j < gate0) & ((j - kv0) % 2 == 1), v_epilogue),
                  (lambda j: j == gate0, gate_epilogue)],
        sub=2, tm=tm)

    k_cmp, v_cmp = nsa_compress(raw, cmp_pos, cmp_w1, cmp_w2, k_gain[0], batch, seq)
    o_cmp, sel = nsa_cmp_select(q_cmp, k_cmp, v_cmp, gates_t, batch, seq)
    o = nsa_attention(q_rot, kv, sel, gates_t, o_cmp, batch, seq)
    tmo = min(512, m)
    return out_matmul([(o, pl.BlockSpec((tmo, d), lambda i: (i, 0)))],
                      lambda ins: ins[0][...], w_o, h, tm=tmo)


def kernel(x, p, positions, mixer_norm, mlp_norm, mlp_w1, mlp_w2, ple_norm, ple_gate, ple_proj,
           moba_w_qkv, moba_q_gain, moba_k_gain, moba_w_o, pool_w, pool_scale,
           nsa_w_q, nsa_w_kv, nsa_q_gain, nsa_k_gain, nsa_cmp_pos, nsa_cmp_w1, nsa_cmp_w2,
           nsa_w_gate, nsa_w_o, conv_w_in, conv_w, conv_b, conv_w_o):
    batch, seq, d = x.shape
    depth = p.shape[0]
    m = batch * seq
    n_heads = d // HEAD_DIM
    gwk = NSA_KV_GROUPS * HEAD_DIM
    tables = rope_tables(positions)
    bf = lambda w: w.astype(BF16)
    ple_gate_bf, ple_proj_bf = bf(ple_gate), bf(ple_proj)
    p_rows = p.reshape(depth, m, -1)

    h = x.reshape(m, d)
    for i in range(depth):
        kind, j = i % 4, i // 4
        if kind == 0:
            h = moba_layer(h, mixer_norm[i], bf(moba_w_qkv[j]), moba_q_gain[j], moba_k_gain[j],
                           bf(moba_w_o[j]), tables, batch, seq)
        elif kind == 1:
            h = pool_layer(h, mixer_norm[i], bf(pool_w[j]), pool_scale[j], seq)
        elif kind == 2:
            w_gate = jnp.pad(nsa_w_gate[j], ((0, 0), (0, gwk - 3 * n_heads)))
            w_proj = bf(jnp.concatenate([nsa_w_q[j], nsa_w_kv[j], w_gate], axis=1))
            h = nsa_layer(h, mixer_norm[i], w_proj, nsa_q_gain[j], nsa_k_gain[j], nsa_cmp_pos[j],
                          bf(nsa_cmp_w1[j]), bf(nsa_cmp_w2[j]), bf(nsa_w_o[j]), tables, batch, seq)
        else:
            h = conv_layer(h, mixer_norm[i], bf(conv_w_in[j]), conv_w[j], conv_b[j], bf(conv_w_o[j]), seq)
        h = mlp_layer(h, mlp_norm[i], mlp_w1, mlp_w2, i)
        h = ple_layer(h, ple_norm[i], ple_gate_bf, p_rows, ple_proj_bf, i)
    return h.reshape(batch, seq, d)
```

```python
import functools

import jax
import jax.numpy as jnp
from jax import lax
from jax.experimental import pallas as pl
from jax.experimental.pallas import tpu as pltpu

F32 = jnp.float32
BF16 = jnp.bfloat16

HEAD_DIM = 128
ROT_DIM = HEAD_DIM // 4
ROPE_THETA = 500000.0
NORM_EPS = 1e-6
MOBA_BLOCK = 256
MOBA_TOPK = 3
POOL_WINDOWS = (2, 4, 8, 16)
NSA_KV_GROUPS = 4
NSA_CMP_LEN = 32
NSA_CMP_STRIDE = 16
NSA_SLC_LEN = 64
NSA_SLC_TOPK = 16
NSA_WINDOW = 512
CONV_WIDTH = 3

MASKED_LOGIT = -1e30
LN2 = 0.6931471805599453
SOFTMAX_EXP2_SCALE = HEAD_DIM ** -0.5 / LN2
KV_CHUNK = 256
MIB = 1024 * 1024


def _params(semantics, vmem_mib):
    return pltpu.CompilerParams(dimension_semantics=semantics,
                                vmem_limit_bytes=vmem_mib * MIB)


def _resident(shape, index_map):
    return pl.BlockSpec(shape, index_map, pipeline_mode=pl.Buffered(1))


def _rms(x, gain):
    ms = jnp.mean(x * x, axis=-1, keepdims=True)
    return x * lax.rsqrt(ms + NORM_EPS) * gain


def _split_bf16(x):
    hi = x.astype(BF16)
    return hi, (x - hi.astype(F32)).astype(BF16)


def _rope(x, cos, sin):
    half = ROT_DIM // 2
    lane = lax.broadcasted_iota(jnp.int32, x.shape, 1)
    partner = jnp.where(lane < half, -pltpu.roll(x, HEAD_DIM - half, 1), pltpu.roll(x, half, 1))
    return x * cos + partner * sin


def _nt_dot(a, b):
    return lax.dot_general(a, b, (((1,), (1,)), ((), ())), preferred_element_type=F32)


def _rope_table_body(pos_ref, freq_ref, cos_ref, sin_ref):
    ang = pos_ref[...].astype(F32) * freq_ref[...]
    cos_ref[...] = jnp.cos(ang)
    sin_ref[...] = jnp.sin(ang)


def rope_tables(positions):
    m = positions.size
    half = ROT_DIM // 2
    freqs = jnp.float32(ROPE_THETA) ** (-jnp.arange(half, dtype=F32) * 2.0 / ROT_DIM)
    freq_row = jnp.concatenate([freqs, freqs, jnp.zeros((HEAD_DIM - ROT_DIM,), F32)])[None]
    tm = min(m, 1024)
    tab = pl.BlockSpec((tm, HEAD_DIM), lambda i: (i, 0))
    return pl.pallas_call(
        _rope_table_body,
        grid=(m // tm,),
        in_specs=[pl.BlockSpec((tm, 1), lambda i: (i, 0)), pl.BlockSpec((1, HEAD_DIM), lambda i: (0, 0))],
        out_specs=[tab, tab],
        out_shape=[jax.ShapeDtypeStruct((m, HEAD_DIM), F32)] * 2,
        compiler_params=_params(("parallel",), 32),
        name="rope_tables",
    )(positions.reshape(m, 1), freq_row)


def _nmm_body(*refs, n_w, n_extra, n_out, sub, variants, on_row_start):
    x_ref, g_ref = refs[0], refs[1]
    w_refs = refs[2:2 + n_w]
    extra = refs[2 + n_w:2 + n_w + n_extra]
    outs = refs[2 + n_w + n_extra:2 + n_w + n_extra + n_out]
    xn_ref = refs[2 + n_w + n_extra + n_out]
    extra = extra + refs[3 + n_w + n_extra + n_out:]
    j = pl.program_id(1)

    @pl.when(j == 0)
    def _():
        xn_ref[...] = _rms(x_ref[...], g_ref[...]).astype(BF16)
        if on_row_start is not None:
            on_row_start(extra)

    width = w_refs[0].shape[1] // sub
    cols = [slice(s * width, (s + 1) * width) for s in range(sub)]

    def run(epilogue):
        xn = xn_ref[...]
        accs = [[jnp.dot(xn, w[:, c], preferred_element_type=F32) for w in w_refs] for c in cols]
        for c, acc in zip(cols, accs):
            epilogue(acc, c, extra, outs)

    if len(variants) == 1:
        run(variants[0][1])
    else:
        for applies, epilogue in variants:
            pl.when(applies(j))(functools.partial(run, epilogue))


def norm_matmul(h, gain, w, w_maps, n_col_tiles, tn, extra, outs, variants, sub=1, tm=1024, vmem_mib=56,
                scratch=(), on_row_start=None):
    m, d = h.shape
    tm = min(tm, m)
    in_specs = [pl.BlockSpec((tm, d), lambda i, j: (i, 0)),
                pl.BlockSpec((1, d), lambda i, j: (0, 0))]
    in_specs += [pl.BlockSpec((d, tn), wm) for wm in w_maps]
    in_specs += [spec for _, spec in extra]
    return pl.pallas_call(
        functools.partial(_nmm_body, n_w=len(w_maps), n_extra=len(extra), n_out=len(outs), sub=sub,
                          variants=variants, on_row_start=on_row_start),
        grid=(m // tm, n_col_tiles),
        in_specs=in_specs,
        out_specs=[spec for _, spec in outs],
        out_shape=[shape for shape, _ in outs],
        scratch_shapes=[pltpu.VMEM((tm, d), BF16)] + list(scratch),
        compiler_params=_params(("arbitrary" if scratch else "parallel", "arbitrary"), vmem_mib),
        name="norm_matmul",
    )(h, gain.reshape(1, d), *([w] * len(w_maps)), *[a for a, _ in extra])


def _head_norm_rope_store(acc, cols, gain_ref, table_refs, out_plain, out_rot):
    width = acc.shape[1]
    same_head = (lax.broadcasted_iota(jnp.int32, (width, width), 0) // HEAD_DIM
                 == lax.broadcasted_iota(jnp.int32, (width, width), 1) // HEAD_DIM)
    ones = jnp.where(same_head, 1.0, 0.0).astype(BF16)
    sq_hi, sq_lo = _split_bf16(acc * acc)
    ssq = (jnp.dot(sq_hi, ones, preferred_element_type=F32)
           + jnp.dot(sq_lo, ones, preferred_element_type=F32))
    normed = acc * lax.rsqrt(ssq * (1.0 / HEAD_DIM) + NORM_EPS) * gain_ref[:, cols]
    if out_plain is not None:
        out_plain[:, cols] = normed.astype(BF16)
    if out_rot is not None:
        cos, sin = table_refs[0][...], table_refs[1][...]
        for hh in range(width // HEAD_DIM):
            sl = slice(hh * HEAD_DIM, (hh + 1) * HEAD_DIM)
            dst = slice(cols.start + hh * HEAD_DIM, cols.start + (hh + 1) * HEAD_DIM)
            out_rot[:, dst] = _rope(normed[:, sl], cos, sin).astype(BF16)


def _omm_body(*refs, n_in, prologue):
    ins = refs[:n_in]
    w_ref, res_ref, o_ref = refs[n_in:n_in + 3]
    a = prologue(ins)
    o_ref[...] = res_ref[...] + jnp.dot(a, w_ref[...], preferred_element_type=F32)


def out_matmul(ins, prologue, w, res, tm=512, vmem_mib=48):
    m, d = res.shape
    tm = min(tm, m)
    k = w.shape[0]
    return pl.pallas_call(
        functools.partial(_omm_body, n_in=len(ins), prologue=prologue),
        grid=(m // tm,),
        in_specs=[spec for _, spec in ins] + [
            _resident((k, d), lambda i: (0, 0)),
            pl.BlockSpec((tm, d), lambda i: (i, 0))],
        out_specs=pl.BlockSpec((tm, d), lambda i: (i, 0)),
        out_shape=jax.ShapeDtypeStruct((m, d), F32),
        compiler_params=_params(("parallel",), vmem_mib),
        name="out_matmul",
    )(*[a for a, _ in ins], w, res)


def _mlp_body(x_ref, g_ref, w1_ref, w2_ref, o_ref, xn_ref):
    @pl.when(pl.program_id(1) == 0)
    def _():
        x = x_ref[...]
        xn_ref[...] = _rms(x, g_ref[...]).astype(BF16)
        o_ref[...] = x

    a = jnp.dot(xn_ref[...], w1_ref[...].astype(BF16), preferred_element_type=F32)
    a = jnp.square(jnp.maximum(a, 0.0)).astype(BF16)
    o_ref[...] += jnp.dot(a, w2_ref[...].astype(BF16), preferred_element_type=F32)


def mlp_layer(h, gain, w1, w2, layer, tm=1024, tf=512):
    m, d = h.shape
    dff = w1.shape[2]
    tm = min(tm, m)
    return pl.pallas_call(
        _mlp_body,
        grid=(m // tm, dff // tf),
        in_specs=[pl.BlockSpec((tm, d), lambda i, f: (i, 0)),
                  pl.BlockSpec((1, d), lambda i, f: (0, 0)),
                  pl.BlockSpec((None, d, tf), lambda i, f: (layer, 0, f)),
                  pl.BlockSpec((None, tf, d), lambda i, f: (layer, f, 0))],
        out_specs=pl.BlockSpec((tm, d), lambda i, f: (i, 0)),
        out_shape=jax.ShapeDtypeStruct((m, d), F32),
        scratch_shapes=[pltpu.VMEM((tm, d), BF16)],
        compiler_params=_params(("parallel", "arbitrary"), 56),
        name="mlp",
    )(h, gain.reshape(1, d), w1, w2)


def _ple_body(x_ref, g_ref, wg_ref, p_ref, wp_ref, o_ref):
    x = x_ref[...]
    xn = _rms(x, g_ref[...]).astype(BF16)
    gate = jax.nn.sigmoid(jnp.dot(xn, wg_ref[...], preferred_element_type=F32))
    emb = jnp.dot(p_ref[...].astype(BF16), wp_ref[...], preferred_element_type=F32)
    o_ref[...] = x + gate * emb


def ple_layer(h, gain, wg, p, wp, layer, tm=512):
    m, d = h.shape
    pd = p.shape[2]
    tm = min(tm, m)
    return pl.pallas_call(
        _ple_body,
        grid=(m // tm,),
        in_specs=[pl.BlockSpec((tm, d), lambda i: (i, 0)),
                  pl.BlockSpec((1, d), lambda i: (0, 0)),
                  _resident((None, d, d), lambda i: (layer, 0, 0)),
                  pl.BlockSpec((None, tm, pd), lambda i: (layer, i, 0)),
                  _resident((None, pd, d), lambda i: (layer, 0, 0))],
        out_specs=pl.BlockSpec((tm, d), lambda i: (i, 0)),
        out_shape=jax.ShapeDtypeStruct((m, d), F32),
        compiler_params=_params(("parallel",), 48),
        name="ple",
    )(h, gain.reshape(1, d), wg, p, wp)


def _flash_buffers(chains, keys, queries):
    return [pltpu.VMEM((chains, keys, queries), F32), pltpu.VMEM((chains, keys, queries), BF16),
            pltpu.VMEM((chains, HEAD_DIM, queries), F32)]


def _flash_keep(s_ref, s, bias):
    if bias.shape[0] == 1:
        s_ref[...] = s
        raw_max = jnp.max(s, axis=0, keepdims=True)
        return raw_max + bias, raw_max - 2.0 * bias
    s = s + bias
    s_ref[...] = s
    s_max = jnp.max(s, axis=0, keepdims=True)
    return s_max, s_max


def _flash_start(buf, k, q_t, bias):
    s_ref, p_ref, acc_ref = buf
    p_ref[...] = jnp.zeros_like(p_ref)
    acc_ref[...] = jnp.zeros_like(acc_ref)
    queries = s_ref.shape[1]
    return _flash_keep(s_ref, jnp.dot(k, q_t, preferred_element_type=F32), bias) + (
        jnp.ones((1, queries), F32), jnp.full((1, queries), MASKED_LOGIT, F32), jnp.zeros((1, queries), F32))


def _flash_stage(bufs, states, v_prev, following):
    products = [jnp.dot(v, buf[1][...], preferred_element_type=F32) for v, buf in zip(v_prev, bufs)]
    if following is not None:
        upcoming = [jnp.dot(k, q_t, preferred_element_type=F32) for k, q_t, _ in following]
    out = []
    for c, ((s_ref, p_ref, acc_ref), state, pv) in enumerate(zip(bufs, states, products)):
        s_max, floor, alpha_prev, m, l = state
        m_new = jnp.maximum(m, s_max)
        p = jnp.exp2(s_ref[...] - jnp.maximum(m_new, floor))
        alpha = jnp.exp((m - m_new) * LN2)
        l = alpha * l + jnp.sum(p, axis=0, keepdims=True)
        p_ref[...] = p.astype(BF16)
        acc_ref[...] = alpha_prev * acc_ref[...] + pv
        kept = (s_max, floor) if following is None else _flash_keep(s_ref, upcoming[c], following[c][2])
        out.append(kept + (alpha, m_new, l))
    return tuple(out)


def _paired_loop(lo, hi, stage, states):
    count = jnp.maximum(hi - lo, 0)

    def two(i, st):
        c = lo + 2 * i
        return stage(c + 1, stage(c, st))

    states = lax.fori_loop(0, count // 2, two, states)
    return lax.cond(count % 2 == 1, lambda st: stage(hi - 1, st), lambda st: st, states)


def _flash_finish(buf, state, v_last):
    _, p_ref, acc_ref = buf
    alpha, l = state[2], state[4]
    return (alpha * acc_ref[...] + jnp.dot(v_last, p_ref[...], preferred_element_type=F32)) / l


def _store_transposed(v_ref, vt_ref, chunk):
    for n in range(vt_ref.shape[0]):
        vt_ref[n] = v_ref[n * chunk:(n + 1) * chunk, :].astype(F32).T.astype(BF16)


def _top_rank(v, row):
    n = v.shape[0]
    if n % 8:
        rank = jnp.zeros(v.shape, jnp.int32)
        for j in range(n):
            r = v[j:j + 1, :]
            rank = rank + jnp.where(row > j, jnp.where(r >= v, 1, 0), jnp.where(r > v, 1, 0))
        return rank
    starts = range(0, n, 8)
    groups = [v[lo:lo + 8, :] for lo in starts]
    ranks = [jnp.zeros((8, v.shape[1]), jnp.int32) for _ in starts]
    for j in range(n):
        r = jnp.broadcast_to(v[j:j + 1, :], (8, v.shape[1]))
        for g, lo in enumerate(starts):
            if lo > j:
                ranks[g] = ranks[g] + jnp.where(r >= groups[g], 1, 0)
            elif lo + 7 <= j:
                ranks[g] = ranks[g] + jnp.where(r > groups[g], 1, 0)
            else:
                below = lax.broadcasted_iota(jnp.int32, (8, v.shape[1]), 0) > j - lo
                ranks[g] = ranks[g] + jnp.where(below, jnp.where(r >= groups[g], 1, 0),
                                                jnp.where(r > groups[g], 1, 0))
    return jnp.concatenate(ranks, axis=0)


def _moba_body(q_ref, k_ref, v_ref, o_ref, kmean_ref, vt_ref, bias_ref, s_ref, p_ref, acc_ref,
               *, n_blocks, heads):
    blk = MOBA_BLOCK
    own = pl.program_id(2)
    cols = [slice(e * HEAD_DIM, (e + 1) * HEAD_DIM) for e in range(heads)]
    bufs = [(s_ref.at[e], p_ref.at[e], acc_ref.at[e]) for e in range(heads)]

    @pl.when(own == 0)
    def _():
        for e in range(heads):
            for n in range(n_blocks):
                kb = k_ref[n * blk:(n + 1) * blk, cols[e]].astype(F32)
                kmean_ref[e, n:n + 1, :] = jnp.mean(kb, axis=0, keepdims=True)
                vt_ref[e, n] = v_ref[n * blk:(n + 1) * blk, cols[e]].astype(F32).T.astype(BF16)

    q_ts = [q_ref[:, cols[e]].astype(F32).T.astype(BF16) for e in range(heads)]
    for e in range(heads):
        gate = jnp.dot(kmean_ref[e].astype(BF16), q_ts[e], preferred_element_type=F32)
        row = lax.broadcasted_iota(jnp.int32, gate.shape, 0)
        gate = jnp.where(row < own, gate, -jnp.inf)
        rank = _top_rank(gate, row)
        bias = jnp.where(row < own, jnp.where(rank < MOBA_TOPK, 0.0, MASKED_LOGIT), MASKED_LOGIT)
        for n in range(n_blocks):
            bias_ref[e, n] = jnp.broadcast_to(bias[n:n + 1, :], (8, blk))

    def keys(j):
        start = pl.multiple_of(j * blk, blk)
        return [k_ref[pl.ds(start, blk), cols[e]] for e in range(heads)]

    def values(j):
        return [vt_ref[e, j] for e in range(heads)]

    def stage(j, states, next_biases):
        following = None if next_biases is None else list(zip(keys(j + 1), q_ts, next_biases))
        return _flash_stage(bufs, states, values(jnp.maximum(j - 1, 0)), following)

    def chosen_bias(j):
        return [jnp.max(bias_ref[e, j], axis=0, keepdims=True) for e in range(heads)]

    causal = jnp.where(lax.broadcasted_iota(jnp.int32, (blk, blk), 0)
                       <= lax.broadcasted_iota(jnp.int32, (blk, blk), 1), 0.0, MASKED_LOGIT)
    first_bias = [jnp.where(own == 0, causal, b) for b in chosen_bias(0)]
    states = tuple(_flash_start(buf, k, q_t, b) for buf, k, q_t, b in zip(bufs, keys(0), q_ts, first_bias))
    states = _paired_loop(0, own - 1, lambda j, st: stage(j, st, chosen_bias(j + 1)), states)
    states = lax.cond(own > 0, lambda st: stage(own - 1, st, [causal] * heads), lambda st: st, states)
    states = stage(own, states, None)
    for e, v_last in enumerate(values(own)):
        o_ref[:, cols[e]] = _flash_finish(bufs[e], states[e], v_last).T.astype(BF16)


def moba_attention(qkv, batch, seq, n_heads, heads_per_step=4):
    blk = MOBA_BLOCK
    nq = seq // blk
    hp = heads_per_step
    width = hp * HEAD_DIM
    groups = n_heads // hp
    return pl.pallas_call(
        functools.partial(_moba_body, n_blocks=nq, heads=hp),
        grid=(batch, groups, nq),
        in_specs=[pl.BlockSpec((blk, width), lambda b, h, i: (b * nq + i, h)),
                  pl.BlockSpec((seq, width), lambda b, h, i: (b, groups + h)),
                  pl.BlockSpec((seq, width), lambda b, h, i: (b, 2 * groups + h))],
        out_specs=pl.BlockSpec((blk, width), lambda b, h, i: (b * nq + i, h)),
        out_shape=jax.ShapeDtypeStruct((batch * seq, n_heads * HEAD_DIM), BF16),
        scratch_shapes=[pltpu.VMEM((hp, nq, HEAD_DIM), F32),
                        pltpu.VMEM((hp, nq, HEAD_DIM, blk), BF16),
                        pltpu.VMEM((hp, nq, 8, blk), F32)] + _flash_buffers(hp, blk, blk),
        compiler_params=_params(("parallel", "parallel", "arbitrary"), 32),
        name="moba_attention",
    )(qkv, qkv, qkv)


def moba_layer(h, gain, w_qkv, q_gain, k_gain, w_o, tables, batch, seq):
    m, d = h.shape
    n_heads = d // HEAD_DIM
    tn = 1024
    n_qk_tiles = 2 * d // tn
    gain_row = jnp.concatenate([jnp.tile(q_gain, n_heads), jnp.tile(k_gain * SOFTMAX_EXP2_SCALE, n_heads),
                                jnp.ones((d,), F32)])[None]

    def qk_epilogue(accs, cols, extra, outs):
        _head_norm_rope_store(accs[0], cols, extra[0], extra[1:], None, outs[0])

    def v_epilogue(accs, cols, extra, outs):
        outs[0][:, cols] = accs[0].astype(BF16)

    tm = min(1024, m)
    tab = pl.BlockSpec((tm, HEAD_DIM), lambda i, j: (i, 0))
    (qkv,) = norm_matmul(
        h, gain, w_qkv, [lambda i, j: (0, j)], 3 * d // tn, tn,
        extra=[(gain_row, pl.BlockSpec((1, tn), lambda i, j: (0, j)))] + [(t, tab) for t in tables],
        outs=[(jax.ShapeDtypeStruct((m, 3 * d), BF16), pl.BlockSpec((tm, tn), lambda i, j: (i, j)))],
        variants=[(lambda j: j < n_qk_tiles, qk_epilogue), (lambda j: j >= n_qk_tiles, v_epilogue)],
        sub=4, tm=tm)
    o = moba_attention(qkv, batch, seq, n_heads)
    tmo = min(512, m)
    return out_matmul([(o, pl.BlockSpec((tmo, d), lambda i: (i, 0)))],
                      lambda ins: ins[0][...], w_o, h, tm=tmo)


def _pool_body(x_ref, halo_ref, g_ref, w_ref, s_ref, o_ref, *, tiles_per_seq, halo):
    i = pl.program_id(0)
    tm, d = x_ref.shape
    group = d // len(POOL_WINDOWS)
    x = x_ref[...]
    gain = g_ref[...]
    xn = _rms(x, gain)
    prev = jnp.where(i % tiles_per_seq == 0, 0.0, _rms(halo_ref[...], gain))
    pos = (i % tiles_per_seq) * tm + lax.broadcasted_iota(jnp.int32, (tm, 1), 0)
    for g, win in enumerate(POOL_WINDOWS):
        sl = slice(g * group, (g + 1) * group)
        run = jnp.concatenate([prev[:, sl], xn[:, sl]], axis=0)
        span = 1
        while span < win:
            run = run + pltpu.roll(run, span, 0)
            span *= 2
        cnt = jnp.minimum(pos + 1, win).astype(F32)
        mean = run[halo:, :] / cnt
        mix = jnp.dot((mean - xn[:, sl]).astype(BF16), w_ref[g], preferred_element_type=F32)
        o_ref[:, sl] = x[:, sl] + mix * s_ref[:, sl]


def pool_layer(h, gain, w_groups, scale, seq, tm=512):
    m, d = h.shape
    halo = 16
    assert max(POOL_WINDOWS) <= halo
    tm = min(tm, seq)
    group = d // len(POOL_WINDOWS)
    return pl.pallas_call(
        functools.partial(_pool_body, tiles_per_seq=seq // tm, halo=halo),
        grid=(m // tm,),
        in_specs=[pl.BlockSpec((tm, d), lambda i: (i, 0)),
                  pl.BlockSpec((halo, d), lambda i: (jnp.maximum(i * (tm // halo) - 1, 0), 0)),
                  pl.BlockSpec((1, d), lambda i: (0, 0)),
                  _resident((len(POOL_WINDOWS), group, group), lambda i: (0, 0, 0)),
                  pl.BlockSpec((1, d), lambda i: (0, 0))],
        out_specs=pl.BlockSpec((tm, d), lambda i: (i, 0)),
        out_shape=jax.ShapeDtypeStruct((m, d), F32),
        compiler_params=_params(("parallel",), 48),
        name="pool_mixer",
    )(h, h, gain.reshape(1, d), w_groups, scale.reshape(1, d))


def conv_layer(h, gain, w_in, conv_w, conv_b, w_o, seq):
    m, d = h.shape
    tn = 512
    nj = d // tn

    tm = min(1024, seq)
    tiles_per_seq = seq // tm
    halo = 8

    def on_row_start(extra):
        @pl.when(pl.program_id(0) % tiles_per_seq == 0)
        def _():
            extra[2][...] = jnp.zeros_like(extra[2])

    def epilogue(accs, cols, extra, outs):
        cw_ref, cb_ref, carry_ref = extra
        j = pl.program_id(1)
        u0 = accs[1] * accs[2]
        prev = carry_ref[j, :, cols]
        row = lax.broadcasted_iota(jnp.int32, u0.shape, 0)
        u1 = jnp.where(row == 0, prev[halo - 1:halo, :], pltpu.roll(u0, 1, 0))
        u2 = jnp.where(row == 0, prev[halo - 2:halo - 1, :],
                       jnp.where(row == 1, prev[halo - 1:halo, :], pltpu.roll(u0, 2, 0)))
        conv = (cw_ref[0:1, cols] * u2 + cw_ref[1:2, cols] * u1 + cw_ref[2:3, cols] * u0
                + cb_ref[:, cols])
        outs[0][:, cols] = (accs[0] * conv).astype(BF16)
        carry_ref[j, :, cols] = u0[tm - halo:, :]

    (y,) = norm_matmul(
        h, gain, w_in, [lambda i, j: (0, j), lambda i, j: (0, j + nj), lambda i, j: (0, j + 2 * nj)],
        nj, tn,
        extra=[(conv_w, pl.BlockSpec((CONV_WIDTH, tn), lambda i, j: (0, j))),
               (conv_b.reshape(1, d), pl.BlockSpec((1, tn), lambda i, j: (0, j)))],
        outs=[(jax.ShapeDtypeStruct((m, d), BF16), pl.BlockSpec((tm, tn), lambda i, j: (i, j)))],
        variants=[(None, epilogue)], sub=2, tm=tm,
        scratch=[pltpu.VMEM((nj, halo, tn), F32)], on_row_start=on_row_start)

    tmo = min(512, m)
    return out_matmul([(y, pl.BlockSpec((tmo, d), lambda i: (i, 0)))],
                      lambda ins: ins[0][...], w_o, h, tm=tmo)


def _compress_body(x_ref, pos_ref, w1_ref, w2_ref, g_ref, kc_ref, vc_ref, lo_ref, hi_ref, *, groups):
    step = pl.program_id(1)
    stride = NSA_CMP_STRIDE

    @pl.when(step == 0)
    def _():
        lo_ref[...] = jnp.zeros_like(lo_ref)
        hi_ref[...] = jnp.zeros_like(hi_ref)

    for kv in range(2):
        p_lo = pos_ref[kv, pl.ds(step, 1), :]
        p_hi = pos_ref[kv, pl.ds(stride + step, 1), :]
        w_lo = w1_ref[kv, step]
        w_hi = w1_ref[kv, stride + step]
        for g in range(groups):
            c = kv * groups + g
            t = x_ref[:, c * HEAD_DIM:(c + 1) * HEAD_DIM]
            lo_ref[c] += jnp.dot((t + p_lo).astype(BF16), w_lo, preferred_element_type=F32)
            hi_ref[c] += jnp.dot((t + p_hi).astype(BF16), w_hi, preferred_element_type=F32)

    @pl.when(step == stride - 1)
    def _():
        n_half = lo_ref.shape[1]
        for kv in range(2):
            for g in range(groups):
                c = kv * groups + g
                pre = lo_ref[c] + pltpu.roll(hi_ref[c], n_half - 1, 0)
                hid = jax.nn.gelu(pre)
                if kv == 0:
                    out = jnp.dot(hid.astype(BF16), w2_ref[0], preferred_element_type=F32)
                    kc_ref[g] = _rms(out, g_ref[...]).astype(BF16)
                else:
                    vc_ref[g] = jnp.dot(w2_ref[1], hid.T.astype(BF16),
                                        preferred_element_type=F32).astype(BF16)


def nsa_compress(raw, cmp_pos, cmp_w1, cmp_w2, k_gain0, batch, seq):
    assert NSA_CMP_LEN == 2 * NSA_CMP_STRIDE
    stride = NSA_CMP_STRIDE
    groups = NSA_KV_GROUPS
    width = raw.shape[1]
    n_half = seq // stride
    x = raw.reshape(batch * n_half, stride * width)
    w1 = cmp_w1.reshape(2, NSA_CMP_LEN, HEAD_DIM, HEAD_DIM)
    cmp_w2 = jnp.stack([cmp_w2[0], cmp_w2[1].T])
    k_out = jax.ShapeDtypeStruct((batch, groups, n_half, HEAD_DIM), BF16)
    v_out = jax.ShapeDtypeStruct((batch, groups, HEAD_DIM, n_half), BF16)
    k_spec = pl.BlockSpec((None, groups, n_half, HEAD_DIM), lambda b, s: (b, 0, 0, 0))
    v_spec = pl.BlockSpec((None, groups, HEAD_DIM, n_half), lambda b, s: (b, 0, 0, 0))
    return pl.pallas_call(
        functools.partial(_compress_body, groups=groups),
        grid=(batch, stride),
        in_specs=[pl.BlockSpec((n_half, width), lambda b, s: (b, s)),
                  pl.BlockSpec((2, NSA_CMP_LEN, HEAD_DIM), lambda b, s: (0, 0, 0)),
                  pl.BlockSpec((2, NSA_CMP_LEN, HEAD_DIM, HEAD_DIM), lambda b, s: (0, 0, 0, 0)),
                  pl.BlockSpec((2, HEAD_DIM, HEAD_DIM), lambda b, s: (0, 0, 0)),
                  pl.BlockSpec((1, HEAD_DIM), lambda b, s: (0, 0))],
        out_specs=[k_spec, v_spec],
        out_shape=[k_out, v_out],
        scratch_shapes=[pltpu.VMEM((2 * groups, n_half, HEAD_DIM), F32)] * 2,
        compiler_params=_params(("parallel", "arbitrary"), 32),
        name="nsa_compress",
    )(x, cmp_pos, w1, cmp_w2, k_gain0.reshape(1, HEAD_DIM))


def _cmp_select_body(q_ref, kc_ref, vct_ref, gate_ref, o_ref, sel_ref, *, n_cmp, n_top, q_per_kv):
    tq = q_ref.shape[0]
    first_head = pl.program_id(1) * q_per_kv
    n_pad = kc_ref.shape[0]
    n_slc = sel_ref.shape[0]
    scale = HEAD_DIM ** -0.5
    t = pl.program_id(2) * tq + lax.broadcasted_iota(jnp.int32, (1, tq), 1)
    n = lax.broadcasted_iota(jnp.int32, (n_pad, 1), 0)
    bias = jnp.where(n < n_cmp, jnp.where(n * NSA_CMP_STRIDE + (NSA_CMP_LEN - 1) <= t, 0.0, MASKED_LOGIT),
                     MASKED_LOGIT)
    seen = t >= NSA_CMP_LEN - 1
    kc = kc_ref[...]
    vct = vct_ref[...]
    heads = [slice(r * HEAD_DIM, (r + 1) * HEAD_DIM) for r in range(q_per_kv)]
    logits = [_nt_dot(kc, q_ref[:, sl]) for sl in heads]
    weights, norms = [], []
    for s in logits:
        s = s * scale + bias
        e = jnp.exp(s - jnp.max(s, axis=0, keepdims=True))
        weights.append(e)
        norms.append(jnp.where(seen, 1.0 / jnp.sum(e, axis=0, keepdims=True), 0.0))
    products = [jnp.dot(vct, e.astype(BF16), preferred_element_type=F32) for e in weights]
    p_sum = jnp.zeros((n_pad, tq), F32)
    for r, sl in enumerate(heads):
        gate = gate_ref[pl.ds(3 * (first_head + r), 1), :]
        o_ref[:, sl] = (products[r] * (norms[r] * gate)).T.astype(BF16)
        p_sum = p_sum + weights[r] * norms[r]

    jj = lax.broadcasted_iota(jnp.int32, (n_slc, n_pad), 0) * NSA_SLC_LEN
    nn = lax.broadcasted_iota(jnp.int32, (n_slc, n_pad), 1) * NSA_CMP_STRIDE
    overlap = jnp.where((nn < jj + NSA_SLC_LEN) & (jj < nn + NSA_CMP_LEN)
                        & (nn < n_cmp * NSA_CMP_STRIDE), 1.0, 0.0).astype(BF16)
    p_hi = p_sum.astype(BF16)
    p_lo = (p_sum - p_hi.astype(F32)).astype(BF16)
    imp = (jnp.dot(overlap, p_hi, preferred_element_type=F32)
           + jnp.dot(overlap, p_lo, preferred_element_type=F32))

    cur = t // NSA_SLC_LEN
    jb = lax.broadcasted_iota(jnp.int32, (n_slc, tq), 0)
    val = jnp.where(jb == cur, jnp.inf,
                    jnp.where(jb == 0, jnp.inf, jnp.where(jb < cur, imp, -jnp.inf)))
    rank = _top_rank(val, jb)
    sel_ref[...] = jnp.where(rank < n_top, jnp.where(val > -jnp.inf, 1.0, 0.0), 0.0)


def nsa_cmp_select(q_cmp, k_cmp, v_cmp_t, gates_t, batch, seq, tq=256):
    m, d = q_cmp.shape
    groups = NSA_KV_GROUPS
    q_per_kv = d // HEAD_DIM // groups
    gw = q_per_kv * HEAD_DIM
    tq = min(tq, seq)
    nq = seq // tq
    n_pad = k_cmp.shape[2]
    n_cmp = (seq - NSA_CMP_LEN) // NSA_CMP_STRIDE + 1
    n_slc = seq // NSA_SLC_LEN
    return pl.pallas_call(
        functools.partial(_cmp_select_body, n_cmp=n_cmp, n_top=min(NSA_SLC_TOPK, n_slc),
                          q_per_kv=q_per_kv),
        grid=(batch, groups, nq),
        in_specs=[pl.BlockSpec((tq, gw), lambda b, g, i: (b * nq + i, g)),
                  pl.BlockSpec((None, None, n_pad, HEAD_DIM), lambda b, g, i: (b, g, 0, 0)),
                  pl.BlockSpec((None, None, HEAD_DIM, n_pad), lambda b, g, i: (b, g, 0, 0)),
                  pl.BlockSpec((gates_t.shape[0], tq), lambda b, g, i: (0, b * nq + i))],
        out_specs=[pl.BlockSpec((tq, gw), lambda b, g, i: (b * nq + i, g)),
                   pl.BlockSpec((None, None, n_slc, tq), lambda b, g, i: (b, g, 0, i))],
        out_shape=[jax.ShapeDtypeStruct((m, d), BF16),
                   jax.ShapeDtypeStruct((batch, groups, n_slc, seq), F32)],
        compiler_params=_params(("parallel", "parallel", "parallel"), 32),
        name="nsa_cmp_select",
    )(q_cmp, k_cmp, v_cmp_t, gates_t)


def _nsa_attn_body(q_ref, ks_ref, vs_ref, kw_ref, vw_ref, sel_ref, gate_ref, oc_ref, o_ref,
                   vst_ref, vwt_ref, bias_ref, s_ref, p_ref, acc_ref, *, q_per_kv, per):
    tq = q_ref.shape[0]
    kc = KV_CHUNK
    sub = kc // NSA_SLC_LEN
    step = pl.program_id(2)
    q0 = step * tq

    @pl.when(step == 0)
    def _():
        _store_transposed(vs_ref, vst_ref, kc)
        _store_transposed(vw_ref, vwt_ref, kc)

    n_chain = q_per_kv // per
    q_ts = [jnp.concatenate([q_ref[:, r * HEAD_DIM:(r + 1) * HEAD_DIM]
                             for r in range(c * per, (c + 1) * per)], axis=0).astype(F32).T.astype(BF16)
            for c in range(n_chain)]
    t = q0 + lax.broadcasted_iota(jnp.int32, (1, tq), 1)
    bias_ref[...] = jnp.where(sel_ref[...] > 0.5, 0.0, MASKED_LOGIT)

    def chain_bias(bias):
        return jnp.concatenate([bias] * per, axis=1)

    def selected_bias(c, kpos):
        picked = jnp.concatenate(
            [jnp.broadcast_to(bias_ref[pl.ds(c * sub + a, 1), :], (NSA_SLC_LEN, tq)) for a in range(sub)],
            axis=0)
        return chain_bias(jnp.where(kpos <= t, picked, MASKED_LOGIT))

    def window_bias(kpos):
        return chain_bias(jnp.where(kpos <= t, jnp.where(t - kpos < NSA_WINDOW, 0.0, MASKED_LOGIT),
                                    MASKED_LOGIT))

    def selected_logits(c):
        start = pl.multiple_of(c * kc, kc)
        kpos = start + lax.broadcasted_iota(jnp.int32, (kc, 1), 0)
        k, bias = ks_ref[pl.ds(start, kc), :], selected_bias(c, kpos)
        return [(k, q_t, bias) for q_t in q_ts]

    def window_logits(c):
        start = pl.multiple_of(c * kc, kc)
        kpos = start + lax.broadcasted_iota(jnp.int32, (kc, 1), 0)
        k, bias = kw_ref[pl.ds(start, kc), :], window_bias(kpos)
        return [(k, q_t, bias) for q_t in q_ts]

    bufs = [(s_ref.at[c], p_ref.at[c], acc_ref.at[c]) for c in range(2 * n_chain)]

    def selected_only(c, states):
        return _flash_stage(bufs[:n_chain], states, [vst_ref[jnp.maximum(c - 1, 0)]] * n_chain,
                            selected_logits(c + 1))

    def selected_and_window(c, states, is_last=False):
        prev = jnp.maximum(c - 1, 0)
        following = None if is_last else selected_logits(c + 1) + window_logits(c + 1)
        return _flash_stage(bufs, states, [vst_ref[prev]] * n_chain + [vwt_ref[prev]] * n_chain, following)

    first = jnp.maximum(q0 - (NSA_WINDOW - 1), 0) // kc
    last = (q0 + tq - 1) // kc
    states = tuple(_flash_start(buf, *f) for buf, f in zip(bufs[:n_chain], selected_logits(0)))
    states = _paired_loop(0, first, selected_only, states)
    states = states + tuple(_flash_start(buf, *f) for buf, f in zip(bufs[n_chain:], window_logits(first)))
    states = lax.fori_loop(first, last, selected_and_window, states)
    states = selected_and_window(last, states, is_last=True)

    first_head = pl.program_id(1) * q_per_kv
    for c in range(n_chain):
        selected = _flash_finish(bufs[c], states[c], vst_ref[last])
        window = _flash_finish(bufs[n_chain + c], states[n_chain + c], vwt_ref[last])
        for i in range(per):
            r = c * per + i
            lanes = slice(i * tq, (i + 1) * tq)
            cols = slice(r * HEAD_DIM, (r + 1) * HEAD_DIM)
            g_slc = gate_ref[pl.ds(3 * (first_head + r) + 1, 1), :]
            g_win = gate_ref[pl.ds(3 * (first_head + r) + 2, 1), :]
            mixed = (selected[:, lanes] * g_slc + window[:, lanes] * g_win).T
            o_ref[:, cols] = (mixed + oc_ref[:, cols].astype(F32)).astype(BF16)


def nsa_attention(q_rot, kv, sel, gates_t, o_cmp, batch, seq, tq=256):
    m, d = q_rot.shape
    groups = NSA_KV_GROUPS
    q_per_kv = d // HEAD_DIM // groups
    gw = q_per_kv * HEAD_DIM
    tq = min(tq, seq)
    nq = seq // tq
    n_slc = sel.shape[2]
    assert seq % KV_CHUNK == 0 and KV_CHUNK % NSA_SLC_LEN == 0

    def kv_spec(part):
        return pl.BlockSpec((seq, HEAD_DIM), lambda b, g, i: (b, part * groups + g))

    q_spec = pl.BlockSpec((tq, gw), lambda b, g, i: (b * nq + i, g))
    out = jax.ShapeDtypeStruct((m, d), BF16)
    v_t = pltpu.VMEM((seq // KV_CHUNK, HEAD_DIM, KV_CHUNK), BF16)
    per = 1
    return pl.pallas_call(
        functools.partial(_nsa_attn_body, q_per_kv=q_per_kv, per=per),
        grid=(batch, groups, nq),
        in_specs=[q_spec, kv_spec(0), kv_spec(1), kv_spec(2), kv_spec(3),
                  pl.BlockSpec((None, None, n_slc, tq), lambda b, g, i: (b, g, 0, i)),
                  pl.BlockSpec((gates_t.shape[0], tq), lambda b, g, i: (0, b * nq + i)),
                  q_spec],
        out_specs=q_spec,
        out_shape=out,
        scratch_shapes=[v_t, v_t, pltpu.VMEM((n_slc, tq), F32)]
        + _flash_buffers(2 * q_per_kv // per, KV_CHUNK, per * tq),
        compiler_params=_params(("parallel", "parallel", "arbitrary"), 32),
        name="nsa_attention",
    )(q_rot, kv, kv, kv, kv, sel, gates_t, o_cmp)


def nsa_layer(h, gain, w_proj, q_gain, k_gain, cmp_pos, cmp_w1, cmp_w2, w_o, tables, batch, seq):
    m, d = h.shape
    n_heads = d // HEAD_DIM
    groups = NSA_KV_GROUPS
    gwk = groups * HEAD_DIM
    tn = 512
    tm = min(1024, m)
    assert tn == gwk and 3 * n_heads <= HEAD_DIM
    raw0, kv0, gate0 = d // tn, d // tn + 2, d // tn + 6
    ones = jnp.ones((gwk,), F32)
    gain_row = jnp.concatenate([jnp.tile(q_gain, n_heads), ones, ones,
                                jnp.tile(k_gain[1] * SOFTMAX_EXP2_SCALE, groups), ones,
                                jnp.tile(k_gain[2] * SOFTMAX_EXP2_SCALE, groups), ones, ones])[None]

    def q_epilogue(accs, cols, extra, outs):
        _head_norm_rope_store(accs[0], cols, extra[0], extra[1:], outs[0], outs[1])

    def raw_epilogue(accs, cols, extra, outs):
        outs[2][:, cols] = accs[0]

    def k_epilogue(accs, cols, extra, outs):
        _head_norm_rope_store(accs[0], cols, extra[0], extra[1:], None, outs[3])

    def v_epilogue(accs, cols, extra, outs):
        outs[3][:, cols] = accs[0].astype(BF16)

    def gate_epilogue(accs, cols, extra, outs):
        if cols.start == 0:
            outs[4][...] = jax.nn.sigmoid(accs[0][:, :HEAD_DIM]).T

    def block(first, count):
        return pl.BlockSpec((tm, tn), lambda i, j: (i, jnp.clip(j - first, 0, count - 1)))

    tab = pl.BlockSpec((tm, HEAD_DIM), lambda i, j: (i, 0))
    q_shape = jax.ShapeDtypeStruct((m, d), BF16)
    q_cmp, q_rot, raw, kv, gates_t = norm_matmul(
        h, gain, w_proj, [lambda i, j: (0, j)], gate0 + 1, tn,
        extra=[(gain_row, pl.BlockSpec((1, tn), lambda i, j: (0, j)))] + [(t, tab) for t in tables],
        outs=[(q_shape, block(0, raw0)), (q_shape, block(0, raw0)),
              (jax.ShapeDtypeStruct((m, 2 * gwk), F32), block(raw0, 2)),
              (jax.ShapeDtypeStruct((m, 4 * gwk), BF16), block(kv0, 4)),
              (jax.ShapeDtypeStruct((HEAD_DIM, m), F32), pl.BlockSpec((HEAD_DIM, tm), lambda i, j: (0, i)))],
        variants=[(lambda j: j < raw0, q_epilogue),
                  (lambda j: (j >= raw0) & (j < kv0), raw_epilogue),
                  (lambda j: (j >= kv0) & (j < gate0) & ((j - kv0) % 2 == 0), k_epilogue),
                  (lambda j: (j >= kv0) & (j < gate0) & ((j - kv0) % 2 == 1), v_epilogue),
                  (lambda j: j == gate0, gate_epilogue)],
        sub=2, tm=tm)

    k_cmp, v_cmp = nsa_compress(raw, cmp_pos, cmp_w1, cmp_w2, k_gain[0], batch, seq)
    o_cmp, sel = nsa_cmp_select(q_cmp, k_cmp, v_cmp, gates_t, batch, seq)
    o = nsa_attention(q_rot, kv, sel, gates_t, o_cmp, batch, seq)
    tmo = min(512, m)
    return out_matmul([(o, pl.BlockSpec((tmo, d), lambda i: (i, 0)))],
                      lambda ins: ins[0][...], w_o, h, tm=tmo)


def kernel(x, p, positions, mixer_norm, mlp_norm, mlp_w1, mlp_w2, ple_norm, ple_gate, ple_proj,
           moba_w_qkv, moba_q_gain, moba_k_gain, moba_w_o, pool_w, pool_scale,
           nsa_w_q, nsa_w_kv, nsa_q_gain, nsa_k_gain, nsa_cmp_pos, nsa_cmp_w1, nsa_cmp_w2,
           nsa_w_gate, nsa_w_o, conv_w_in, conv_w, conv_b, conv_w_o):
    batch, seq, d = x.shape
    depth = p.shape[0]
    m = batch * seq
    n_heads = d // HEAD_DIM
    gwk = NSA_KV_GROUPS * HEAD_DIM
    tables = rope_tables(positions)
    bf = lambda w: w.astype(BF16)
    ple_gate_bf, ple_proj_bf = bf(ple_gate), bf(ple_proj)
    p_rows = p.reshape(depth, m, -1)

    h = x.reshape(m, d)
    for i in range(depth):
        kind, j = i % 4, i // 4
        if kind == 0:
            h = moba_layer(h, mixer_norm[i], bf(moba_w_qkv[j]), moba_q_gain[j], moba_k_gain[j],
                           bf(moba_w_o[j]), tables, batch, seq)
        elif kind == 1:
            h = pool_layer(h, mixer_norm[i], bf(pool_w[j]), pool_scale[j], seq)
        elif kind == 2:
            w_gate = jnp.pad(nsa_w_gate[j], ((0, 0), (0, gwk - 3 * n_heads)))
            w_proj = bf(jnp.concatenate([nsa_w_q[j], nsa_w_kv[j], w_gate], axis=1))
            h = nsa_layer(h, mixer_norm[i], w_proj, nsa_q_gain[j], nsa_k_gain[j], nsa_cmp_pos[j],
                          bf(nsa_cmp_w1[j]), bf(nsa_cmp_w2[j]), bf(nsa_w_o[j]), tables, batch, seq)
        else:
            h = conv_layer(h, mixer_norm[i], bf(conv_w_in[j]), conv_w[j], conv_b[j], bf(conv_w_o[j]), seq)
        h = mlp_layer(h, mlp_norm[i], mlp_w1, mlp_w2, i)
        h = ple_layer(h, ple_norm[i], ple_gate_bf, p_rows, ple_proj_bf, i)
    return h.reshape(batch, seq, d)
```

```python
import functools

import jax
import jax.numpy as jnp
from jax import lax
from jax.experimental import pallas as pl
from jax.experimental.pallas import tpu as pltpu

F32 = jnp.float32
BF16 = jnp.bfloat16

HEAD_DIM = 128
ROT_DIM = HEAD_DIM // 4
ROPE_THETA = 500000.0
NORM_EPS = 1e-6
MOBA_BLOCK = 256
MOBA_TOPK = 3
POOL_WINDOWS = (2, 4, 8, 16)
NSA_KV_GROUPS = 4
NSA_CMP_LEN = 32
NSA_CMP_STRIDE = 16
NSA_SLC_LEN = 64
NSA_SLC_TOPK = 16
NSA_WINDOW = 512
CONV_WIDTH = 3

MASKED_LOGIT = -1e30
LN2 = 0.6931471805599453
SOFTMAX_EXP2_SCALE = HEAD_DIM ** -0.5 / LN2
KV_CHUNK = 256
MIB = 1024 * 1024

STREAM_ROWS = 1024
RESIDENT_ROWS = 512
COL_TILE = 512
VMEM_BIG_MIB, VMEM_MID_MIB, VMEM_SMALL_MIB = 56, 48, 32


def _params(semantics, vmem_mib):
    return pltpu.CompilerParams(dimension_semantics=semantics,
                                vmem_limit_bytes=vmem_mib * MIB)


def _resident(shape, index_map):
    return pl.BlockSpec(shape, index_map, pipeline_mode=pl.Buffered(1))


def _rms(x, gain):
    ms = jnp.mean(x * x, axis=-1, keepdims=True)
    return x * lax.rsqrt(ms + NORM_EPS) * gain


def _split_bf16(x):
    hi = x.astype(BF16)
    return hi, (x - hi.astype(F32)).astype(BF16)


def _rope(x, cos, sin):
    half = ROT_DIM // 2
    lane = lax.broadcasted_iota(jnp.int32, x.shape, 1)
    partner = jnp.where(lane < half, -pltpu.roll(x, HEAD_DIM - half, 1), pltpu.roll(x, half, 1))
    return x * cos + partner * sin


def _nt_dot(a, b):
    return lax.dot_general(a, b, (((1,), (1,)), ((), ())), preferred_element_type=F32)


def _rope_table_body(pos_ref, freq_ref, cos_ref, sin_ref):
    ang = pos_ref[...].astype(F32) * freq_ref[...]
    cos_ref[...] = jnp.cos(ang)
    sin_ref[...] = jnp.sin(ang)


def rope_tables(positions):
    m = positions.size
    half = ROT_DIM // 2
    freqs = jnp.float32(ROPE_THETA) ** (-jnp.arange(half, dtype=F32) * 2.0 / ROT_DIM)
    freq_row = jnp.concatenate([freqs, freqs, jnp.zeros((HEAD_DIM - ROT_DIM,), F32)])[None]
    tm = min(m, STREAM_ROWS)
    tab = pl.BlockSpec((tm, HEAD_DIM), lambda i: (i, 0))
    return pl.pallas_call(
        _rope_table_body,
        grid=(m // tm,),
        in_specs=[pl.BlockSpec((tm, 1), lambda i: (i, 0)), pl.BlockSpec((1, HEAD_DIM), lambda i: (0, 0))],
        out_specs=[tab, tab],
        out_shape=[jax.ShapeDtypeStruct((m, HEAD_DIM), F32)] * 2,
        compiler_params=_params(("parallel",), VMEM_SMALL_MIB),
        name="rope_tables",
    )(positions.reshape(m, 1), freq_row)


def _nmm_body(*refs, n_w, n_extra, n_out, sub, variants, on_row_start):
    x_ref, g_ref = refs[0], refs[1]
    w_refs = refs[2:2 + n_w]
    extra = refs[2 + n_w:2 + n_w + n_extra]
    outs = refs[2 + n_w + n_extra:2 + n_w + n_extra + n_out]
    xn_ref = refs[2 + n_w + n_extra + n_out]
    extra = extra + refs[3 + n_w + n_extra + n_out:]
    j = pl.program_id(1)

    @pl.when(j == 0)
    def _():
        xn_ref[...] = _rms(x_ref[...], g_ref[...]).astype(BF16)
        if on_row_start is not None:
            on_row_start(extra)

    width = w_refs[0].shape[1] // sub
    cols = [slice(s * width, (s + 1) * width) for s in range(sub)]

    def run(epilogue):
        xn = xn_ref[...]
        accs = [[jnp.dot(xn, w[:, c], preferred_element_type=F32) for w in w_refs] for c in cols]
        for c, acc in zip(cols, accs):
            epilogue(acc, c, extra, outs)

    if len(variants) == 1:
        run(variants[0][1])
    else:
        for applies, epilogue in variants:
            pl.when(applies(j))(functools.partial(run, epilogue))


def norm_matmul(h, gain, w, w_maps, n_col_tiles, tn, extra, outs, variants, sub=1, tm=STREAM_ROWS, vmem_mib=VMEM_BIG_MIB,
                scratch=(), on_row_start=None):
    m, d = h.shape
    tm = min(tm, m)
    in_specs = [pl.BlockSpec((tm, d), lambda i, j: (i, 0)),
                pl.BlockSpec((1, d), lambda i, j: (0, 0))]
    in_specs += [pl.BlockSpec((d, tn), wm) for wm in w_maps]
    in_specs += [spec for _, spec in extra]
    return pl.pallas_call(
        functools.partial(_nmm_body, n_w=len(w_maps), n_extra=len(extra), n_out=len(outs), sub=sub,
                          variants=variants, on_row_start=on_row_start),
        grid=(m // tm, n_col_tiles),
        in_specs=in_specs,
        out_specs=[spec for _, spec in outs],
        out_shape=[shape for shape, _ in outs],
        scratch_shapes=[pltpu.VMEM((tm, d), BF16)] + list(scratch),
        compiler_params=_params(("arbitrary" if scratch else "parallel", "arbitrary"), vmem_mib),
        name="norm_matmul",
    )(h, gain.reshape(1, d), *([w] * len(w_maps)), *[a for a, _ in extra])


def _head_norm_rope_store(acc, cols, gain_ref, table_refs, out_plain, out_rot):
    width = acc.shape[1]
    same_head = (lax.broadcasted_iota(jnp.int32, (width, width), 0) // HEAD_DIM
                 == lax.broadcasted_iota(jnp.int32, (width, width), 1) // HEAD_DIM)
    ones = jnp.where(same_head, 1.0, 0.0).astype(BF16)
    sq_hi, sq_lo = _split_bf16(acc * acc)
    ssq = (jnp.dot(sq_hi, ones, preferred_element_type=F32)
           + jnp.dot(sq_lo, ones, preferred_element_type=F32))
    normed = acc * lax.rsqrt(ssq * (1.0 / HEAD_DIM) + NORM_EPS) * gain_ref[:, cols]
    if out_plain is not None:
        out_plain[:, cols] = normed.astype(BF16)
    if out_rot is not None:
        cos, sin = table_refs[0][...], table_refs[1][...]
        for hh in range(width // HEAD_DIM):
            sl = slice(hh * HEAD_DIM, (hh + 1) * HEAD_DIM)
            dst = slice(cols.start + hh * HEAD_DIM, cols.start + (hh + 1) * HEAD_DIM)
            out_rot[:, dst] = _rope(normed[:, sl], cos, sin).astype(BF16)


def _omm_body(*refs, n_in, prologue):
    ins = refs[:n_in]
    w_ref, res_ref, o_ref = refs[n_in:n_in + 3]
    a = prologue(ins)
    o_ref[...] = res_ref[...] + jnp.dot(a, w_ref[...], preferred_element_type=F32)


def out_matmul(ins, prologue, w, res, tm=RESIDENT_ROWS, vmem_mib=VMEM_MID_MIB):
    m, d = res.shape
    tm = min(tm, m)
    k = w.shape[0]
    return pl.pallas_call(
        functools.partial(_omm_body, n_in=len(ins), prologue=prologue),
        grid=(m // tm,),
        in_specs=[spec for _, spec in ins] + [
            _resident((k, d), lambda i: (0, 0)),
            pl.BlockSpec((tm, d), lambda i: (i, 0))],
        out_specs=pl.BlockSpec((tm, d), lambda i: (i, 0)),
        out_shape=jax.ShapeDtypeStruct((m, d), F32),
        compiler_params=_params(("parallel",), vmem_mib),
        name="out_matmul",
    )(*[a for a, _ in ins], w, res)


def _mlp_body(x_ref, g_ref, w1_ref, w2_ref, o_ref, xn_ref):
    @pl.when(pl.program_id(1) == 0)
    def _():
        x = x_ref[...]
        xn_ref[...] = _rms(x, g_ref[...]).astype(BF16)
        o_ref[...] = x

    a = jnp.dot(xn_ref[...], w1_ref[...].astype(BF16), preferred_element_type=F32)
    a = jnp.square(jnp.maximum(a, 0.0)).astype(BF16)
    o_ref[...] += jnp.dot(a, w2_ref[...].astype(BF16), preferred_element_type=F32)


def mlp_layer(h, gain, w1, w2, layer, tm=STREAM_ROWS, tf=COL_TILE):
    m, d = h.shape
    dff = w1.shape[2]
    tm = min(tm, m)
    return pl.pallas_call(
        _mlp_body,
        grid=(m // tm, dff // tf),
        in_specs=[pl.BlockSpec((tm, d), lambda i, f: (i, 0)),
                  pl.BlockSpec((1, d), lambda i, f: (0, 0)),
                  pl.BlockSpec((None, d, tf), lambda i, f: (layer, 0, f)),
                  pl.BlockSpec((None, tf, d), lambda i, f: (layer, f, 0))],
        out_specs=pl.BlockSpec((tm, d), lambda i, f: (i, 0)),
        out_shape=jax.ShapeDtypeStruct((m, d), F32),
        scratch_shapes=[pltpu.VMEM((tm, d), BF16)],
        compiler_params=_params(("parallel", "arbitrary"), VMEM_BIG_MIB),
        name="mlp",
    )(h, gain.reshape(1, d), w1, w2)


def _ple_body(x_ref, g_ref, wg_ref, p_ref, wp_ref, o_ref):
    x = x_ref[...]
    xn = _rms(x, g_ref[...]).astype(BF16)
    gate = jax.nn.sigmoid(jnp.dot(xn, wg_ref[...], preferred_element_type=F32))
    emb = jnp.dot(p_ref[...].astype(BF16), wp_ref[...], preferred_element_type=F32)
    o_ref[...] = x + gate * emb


def ple_layer(h, gain, wg, p, wp, layer, tm=RESIDENT_ROWS):
    m, d = h.shape
    pd = p.shape[2]
    tm = min(tm, m)
    return pl.pallas_call(
        _ple_body,
        grid=(m // tm,),
        in_specs=[pl.BlockSpec((tm, d), lambda i: (i, 0)),
                  pl.BlockSpec((1, d), lambda i: (0, 0)),
                  _resident((None, d, d), lambda i: (layer, 0, 0)),
                  pl.BlockSpec((None, tm, pd), lambda i: (layer, i, 0)),
                  _resident((None, pd, d), lambda i: (layer, 0, 0))],
        out_specs=pl.BlockSpec((tm, d), lambda i: (i, 0)),
        out_shape=jax.ShapeDtypeStruct((m, d), F32),
        compiler_params=_params(("parallel",), VMEM_MID_MIB),
        name="ple",
    )(h, gain.reshape(1, d), wg, p, wp)


def _flash_buffers(chains, keys, queries):
    return [pltpu.VMEM((chains, keys, queries), F32), pltpu.VMEM((chains, keys, queries), BF16),
            pltpu.VMEM((chains, HEAD_DIM, queries), F32)]


def _flash_keep(s_ref, s, bias):
    if bias.shape[0] == 1:
        s_ref[...] = s
        raw_max = jnp.max(s, axis=0, keepdims=True)
        return raw_max + bias, raw_max - 2.0 * bias
    s = s + bias
    s_ref[...] = s
    s_max = jnp.max(s, axis=0, keepdims=True)
    return s_max, s_max


def _flash_start(buf, k, q_t, bias):
    s_ref, p_ref, acc_ref = buf
    p_ref[...] = jnp.zeros_like(p_ref)
    acc_ref[...] = jnp.zeros_like(acc_ref)
    queries = s_ref.shape[1]
    return _flash_keep(s_ref, jnp.dot(k, q_t, preferred_element_type=F32), bias) + (
        jnp.ones((1, queries), F32), jnp.full((1, queries), MASKED_LOGIT, F32), jnp.zeros((1, queries), F32))


def _flash_stage(bufs, states, v_prev, following):
    products = [jnp.dot(v, buf[1][...], preferred_element_type=F32) for v, buf in zip(v_prev, bufs)]
    if following is not None:
        upcoming = [jnp.dot(k, q_t, preferred_element_type=F32) for k, q_t, _ in following]
    out = []
    for c, ((s_ref, p_ref, acc_ref), state, pv) in enumerate(zip(bufs, states, products)):
        s_max, floor, alpha_prev, m, l = state
        m_new = jnp.maximum(m, s_max)
        p = jnp.exp2(s_ref[...] - jnp.maximum(m_new, floor))
        alpha = jnp.exp((m - m_new) * LN2)
        l = alpha * l + jnp.sum(p, axis=0, keepdims=True)
        p_ref[...] = p.astype(BF16)
        acc_ref[...] = alpha_prev * acc_ref[...] + pv
        kept = (s_max, floor) if following is None else _flash_keep(s_ref, upcoming[c], following[c][2])
        out.append(kept + (alpha, m_new, l))
    return tuple(out)


def _paired_loop(lo, hi, stage, states):
    count = jnp.maximum(hi - lo, 0)

    def two(i, st):
        c = lo + 2 * i
        return stage(c + 1, stage(c, st))

    states = lax.fori_loop(0, count // 2, two, states)
    return lax.cond(count % 2 == 1, lambda st: stage(hi - 1, st), lambda st: st, states)


def _flash_finish(buf, state, v_last):
    _, p_ref, acc_ref = buf
    alpha, l = state[2], state[4]
    return (alpha * acc_ref[...] + jnp.dot(v_last, p_ref[...], preferred_element_type=F32)) / l


def _store_transposed(v_ref, vt_ref, chunk):
    for n in range(vt_ref.shape[0]):
        vt_ref[n] = v_ref[n * chunk:(n + 1) * chunk, :].astype(F32).T.astype(BF16)


def _top_rank(v, row):
    n = v.shape[0]
    if n % 8:
        rank = jnp.zeros(v.shape, jnp.int32)
        for j in range(n):
            r = v[j:j + 1, :]
            rank = rank + jnp.where(row > j, jnp.where(r >= v, 1, 0), jnp.where(r > v, 1, 0))
        return rank
    starts = range(0, n, 8)
    groups = [v[lo:lo + 8, :] for lo in starts]
    ranks = [jnp.zeros((8, v.shape[1]), jnp.int32) for _ in starts]
    for j in range(n):
        r = jnp.broadcast_to(v[j:j + 1, :], (8, v.shape[1]))
        for g, lo in enumerate(starts):
            if lo > j:
                ranks[g] = ranks[g] + jnp.where(r >= groups[g], 1, 0)
            elif lo + 7 <= j:
                ranks[g] = ranks[g] + jnp.where(r > groups[g], 1, 0)
            else:
                below = lax.broadcasted_iota(jnp.int32, (8, v.shape[1]), 0) > j - lo
                ranks[g] = ranks[g] + jnp.where(below, jnp.where(r >= groups[g], 1, 0),
                                                jnp.where(r > groups[g], 1, 0))
    return jnp.concatenate(ranks, axis=0)


def _moba_body(q_ref, k_ref, v_ref, o_ref, kmean_ref, vt_ref, bias_ref, s_ref, p_ref, acc_ref,
               *, n_blocks, heads):
    blk = MOBA_BLOCK
    own = pl.program_id(2)
    cols = [slice(e * HEAD_DIM, (e + 1) * HEAD_DIM) for e in range(heads)]
    bufs = [(s_ref.at[e], p_ref.at[e], acc_ref.at[e]) for e in range(heads)]

    @pl.when(own == 0)
    def _():
        for e in range(heads):
            for n in range(n_blocks):
                kb = k_ref[n * blk:(n + 1) * blk, cols[e]].astype(F32)
                kmean_ref[e, n:n + 1, :] = jnp.mean(kb, axis=0, keepdims=True)
                vt_ref[e, n] = v_ref[n * blk:(n + 1) * blk, cols[e]].astype(F32).T.astype(BF16)

    q_ts = [q_ref[:, cols[e]].astype(F32).T.astype(BF16) for e in range(heads)]
    for e in range(heads):
        gate = jnp.dot(kmean_ref[e].astype(BF16), q_ts[e], preferred_element_type=F32)
        row = lax.broadcasted_iota(jnp.int32, gate.shape, 0)
        gate = jnp.where(row < own, gate, -jnp.inf)
        rank = _top_rank(gate, row)
        bias = jnp.where(row < own, jnp.where(rank < MOBA_TOPK, 0.0, MASKED_LOGIT), MASKED_LOGIT)
        for n in range(n_blocks):
            bias_ref[e, n] = jnp.broadcast_to(bias[n:n + 1, :], (8, blk))

    def keys(j):
        start = pl.multiple_of(j * blk, blk)
        return [k_ref[pl.ds(start, blk), cols[e]] for e in range(heads)]

    def values(j):
        return [vt_ref[e, j] for e in range(heads)]

    def stage(j, states, next_biases):
        following = None if next_biases is None else list(zip(keys(j + 1), q_ts, next_biases))
        return _flash_stage(bufs, states, values(jnp.maximum(j - 1, 0)), following)

    def chosen_bias(j):
        return [jnp.max(bias_ref[e, j], axis=0, keepdims=True) for e in range(heads)]

    causal = jnp.where(lax.broadcasted_iota(jnp.int32, (blk, blk), 0)
                       <= lax.broadcasted_iota(jnp.int32, (blk, blk), 1), 0.0, MASKED_LOGIT)
    first_bias = [jnp.where(own == 0, causal, b) for b in chosen_bias(0)]
    states = tuple(_flash_start(buf, k, q_t, b) for buf, k, q_t, b in zip(bufs, keys(0), q_ts, first_bias))
    states = _paired_loop(0, own - 1, lambda j, st: stage(j, st, chosen_bias(j + 1)), states)
    states = lax.cond(own > 0, lambda st: stage(own - 1, st, [causal] * heads), lambda st: st, states)
    states = stage(own, states, None)
    for e, v_last in enumerate(values(own)):
        o_ref[:, cols[e]] = _flash_finish(bufs[e], states[e], v_last).T.astype(BF16)


def moba_attention(qkv, batch, seq, n_heads, heads_per_step=4):
    blk = MOBA_BLOCK
    nq = seq // blk
    hp = heads_per_step
    width = hp * HEAD_DIM
    groups = n_heads // hp
    return pl.pallas_call(
        functools.partial(_moba_body, n_blocks=nq, heads=hp),
        grid=(batch, groups, nq),
        in_specs=[pl.BlockSpec((blk, width), lambda b, h, i: (b * nq + i, h)),
                  pl.BlockSpec((seq, width), lambda b, h, i: (b, groups + h)),
                  pl.BlockSpec((seq, width), lambda b, h, i: (b, 2 * groups + h))],
        out_specs=pl.BlockSpec((blk, width), lambda b, h, i: (b * nq + i, h)),
        out_shape=jax.ShapeDtypeStruct((batch * seq, n_heads * HEAD_DIM), BF16),
        scratch_shapes=[pltpu.VMEM((hp, nq, HEAD_DIM), F32),
                        pltpu.VMEM((hp, nq, HEAD_DIM, blk), BF16),
                        pltpu.VMEM((hp, nq, 8, blk), F32)] + _flash_buffers(hp, blk, blk),
        compiler_params=_params(("parallel", "parallel", "arbitrary"), VMEM_SMALL_MIB),
        name="moba_attention",
    )(qkv, qkv, qkv)


def moba_layer(h, gain, w_qkv, q_gain, k_gain, w_o, tables, batch, seq):
    m, d = h.shape
    n_heads = d // HEAD_DIM
    tn = 2 * COL_TILE
    n_qk_tiles = 2 * d // tn
    gain_row = jnp.concatenate([jnp.tile(q_gain, n_heads), jnp.tile(k_gain * SOFTMAX_EXP2_SCALE, n_heads),
                                jnp.ones((d,), F32)])[None]

    def qk_epilogue(accs, cols, extra, outs):
        _head_norm_rope_store(accs[0], cols, extra[0], extra[1:], None, outs[0])

    def v_epilogue(accs, cols, extra, outs):
        outs[0][:, cols] = accs[0].astype(BF16)

    tm = min(STREAM_ROWS, m)
    tab = pl.BlockSpec((tm, HEAD_DIM), lambda i, j: (i, 0))
    (qkv,) = norm_matmul(
        h, gain, w_qkv, [lambda i, j: (0, j)], 3 * d // tn, tn,
        extra=[(gain_row, pl.BlockSpec((1, tn), lambda i, j: (0, j)))] + [(t, tab) for t in tables],
        outs=[(jax.ShapeDtypeStruct((m, 3 * d), BF16), pl.BlockSpec((tm, tn), lambda i, j: (i, j)))],
        variants=[(lambda j: j < n_qk_tiles, qk_epilogue), (lambda j: j >= n_qk_tiles, v_epilogue)],
        sub=4, tm=tm)
    o = moba_attention(qkv, batch, seq, n_heads)
    tmo = min(RESIDENT_ROWS, m)
    return out_matmul([(o, pl.BlockSpec((tmo, d), lambda i: (i, 0)))],
                      lambda ins: ins[0][...], w_o, h, tm=tmo)


def _pool_body(x_ref, halo_ref, g_ref, w_ref, s_ref, o_ref, *, tiles_per_seq, halo):
    i = pl.program_id(0)
    tm, d = x_ref.shape
    group = d // len(POOL_WINDOWS)
    x = x_ref[...]
    gain = g_ref[...]
    xn = _rms(x, gain)
    prev = jnp.where(i % tiles_per_seq == 0, 0.0, _rms(halo_ref[...], gain))
    pos = (i % tiles_per_seq) * tm + lax.broadcasted_iota(jnp.int32, (tm, 1), 0)
    for g, win in enumerate(POOL_WINDOWS):
        sl = slice(g * group, (g + 1) * group)
        run = jnp.concatenate([prev[:, sl], xn[:, sl]], axis=0)
        span = 1
        while span < win:
            run = run + pltpu.roll(run, span, 0)
            span *= 2
        cnt = jnp.minimum(pos + 1, win).astype(F32)
        mean = run[halo:, :] / cnt
        mix = jnp.dot((mean - xn[:, sl]).astype(BF16), w_ref[g], preferred_element_type=F32)
        o_ref[:, sl] = x[:, sl] + mix * s_ref[:, sl]


def pool_layer(h, gain, w_groups, scale, seq, tm=RESIDENT_ROWS):
    m, d = h.shape
    halo = 16
    assert max(POOL_WINDOWS) <= halo
    tm = min(tm, seq)
    group = d // len(POOL_WINDOWS)
    return pl.pallas_call(
        functools.partial(_pool_body, tiles_per_seq=seq // tm, halo=halo),
        grid=(m // tm,),
        in_specs=[pl.BlockSpec((tm, d), lambda i: (i, 0)),
                  pl.BlockSpec((halo, d), lambda i: (jnp.maximum(i * (tm // halo) - 1, 0), 0)),
                  pl.BlockSpec((1, d), lambda i: (0, 0)),
                  _resident((len(POOL_WINDOWS), group, group), lambda i: (0, 0, 0)),
                  pl.BlockSpec((1, d), lambda i: (0, 0))],
        out_specs=pl.BlockSpec((tm, d), lambda i: (i, 0)),
        out_shape=jax.ShapeDtypeStruct((m, d), F32),
        compiler_params=_params(("parallel",), VMEM_MID_MIB),
        name="pool_mixer",
    )(h, h, gain.reshape(1, d), w_groups, scale.reshape(1, d))


def conv_layer(h, gain, w_in, conv_w, conv_b, w_o, seq):
    m, d = h.shape
    tn = COL_TILE
    nj = d // tn

    tm = min(STREAM_ROWS, seq)
    tiles_per_seq = seq // tm
    halo = 8

    def on_row_start(extra):
        @pl.when(pl.program_id(0) % tiles_per_seq == 0)
        def _():
            extra[2][...] = jnp.zeros_like(extra[2])

    def epilogue(accs, cols, extra, outs):
        cw_ref, cb_ref, carry_ref = extra
        j = pl.program_id(1)
        u0 = accs[1] * accs[2]
        prev = carry_ref[j, :, cols]
        row = lax.broadcasted_iota(jnp.int32, u0.shape, 0)
        u1 = jnp.where(row == 0, prev[halo - 1:halo, :], pltpu.roll(u0, 1, 0))
        u2 = jnp.where(row == 0, prev[halo - 2:halo - 1, :],
                       jnp.where(row == 1, prev[halo - 1:halo, :], pltpu.roll(u0, 2, 0)))
        conv = (cw_ref[0:1, cols] * u2 + cw_ref[1:2, cols] * u1 + cw_ref[2:3, cols] * u0
                + cb_ref[:, cols])
        outs[0][:, cols] = (accs[0] * conv).astype(BF16)
        carry_ref[j, :, cols] = u0[tm - halo:, :]

    (y,) = norm_matmul(
        h, gain, w_in, [lambda i, j: (0, j), lambda i, j: (0, j + nj), lambda i, j: (0, j + 2 * nj)],
        nj, tn,
        extra=[(conv_w, pl.BlockSpec((CONV_WIDTH, tn), lambda i, j: (0, j))),
               (conv_b.reshape(1, d), pl.BlockSpec((1, tn), lambda i, j: (0, j)))],
        outs=[(jax.ShapeDtypeStruct((m, d), BF16), pl.BlockSpec((tm, tn), lambda i, j: (i, j)))],
        variants=[(None, epilogue)], sub=2, tm=tm,
        scratch=[pltpu.VMEM((nj, halo, tn), F32)], on_row_start=on_row_start)

    tmo = min(RESIDENT_ROWS, m)
    return out_matmul([(y, pl.BlockSpec((tmo, d), lambda i: (i, 0)))],
                      lambda ins: ins[0][...], w_o, h, tm=tmo)


def _compress_body(x_ref, pos_ref, w1_ref, w2_ref, g_ref, kc_ref, vc_ref, lo_ref, hi_ref, *, groups):
    step = pl.program_id(1)
    stride = NSA_CMP_STRIDE

    @pl.when(step == 0)
    def _():
        lo_ref[...] = jnp.zeros_like(lo_ref)
        hi_ref[...] = jnp.zeros_like(hi_ref)

    for kv in range(2):
        p_lo = pos_ref[kv, pl.ds(step, 1), :]
        p_hi = pos_ref[kv, pl.ds(stride + step, 1), :]
        w_lo = w1_ref[kv, step]
        w_hi = w1_ref[kv, stride + step]
        for g in range(groups):
            c = kv * groups + g
            t = x_ref[:, c * HEAD_DIM:(c + 1) * HEAD_DIM]
            lo_ref[c] += jnp.dot((t + p_lo).astype(BF16), w_lo, preferred_element_type=F32)
            hi_ref[c] += jnp.dot((t + p_hi).astype(BF16), w_hi, preferred_element_type=F32)

    @pl.when(step == stride - 1)
    def _():
        n_half = lo_ref.shape[1]
        for kv in range(2):
            for g in range(groups):
                c = kv * groups + g
                pre = lo_ref[c] + pltpu.roll(hi_ref[c], n_half - 1, 0)
                hid = jax.nn.gelu(pre)
                if kv == 0:
                    out = jnp.dot(hid.astype(BF16), w2_ref[0], preferred_element_type=F32)
                    kc_ref[g] = _rms(out, g_ref[...]).astype(BF16)
                else:
                    vc_ref[g] = jnp.dot(w2_ref[1], hid.T.astype(BF16),
                                        preferred_element_type=F32).astype(BF16)


def nsa_compress(raw, cmp_pos, cmp_w1, cmp_w2, k_gain0, batch, seq):
    assert NSA_CMP_LEN == 2 * NSA_CMP_STRIDE
    stride = NSA_CMP_STRIDE
    groups = NSA_KV_GROUPS
    width = raw.shape[1]
    n_half = seq // stride
    x = raw.reshape(batch * n_half, stride * width)
    w1 = cmp_w1.reshape(2, NSA_CMP_LEN, HEAD_DIM, HEAD_DIM)
    cmp_w2 = jnp.stack([cmp_w2[0], cmp_w2[1].T])
    k_out = jax.ShapeDtypeStruct((batch, groups, n_half, HEAD_DIM), BF16)
    v_out = jax.ShapeDtypeStruct((batch, groups, HEAD_DIM, n_half), BF16)
    k_spec = pl.BlockSpec((None, groups, n_half, HEAD_DIM), lambda b, s: (b, 0, 0, 0))
    v_spec = pl.BlockSpec((None, groups, HEAD_DIM, n_half), lambda b, s: (b, 0, 0, 0))
    return pl.pallas_call(
        functools.partial(_compress_body, groups=groups),
        grid=(batch, stride),
        in_specs=[pl.BlockSpec((n_half, width), lambda b, s: (b, s)),
                  pl.BlockSpec((2, NSA_CMP_LEN, HEAD_DIM), lambda b, s: (0, 0, 0)),
                  pl.BlockSpec((2, NSA_CMP_LEN, HEAD_DIM, HEAD_DIM), lambda b, s: (0, 0, 0, 0)),
                  pl.BlockSpec((2, HEAD_DIM, HEAD_DIM), lambda b, s: (0, 0, 0)),
                  pl.BlockSpec((1, HEAD_DIM), lambda b, s: (0, 0))],
        out_specs=[k_spec, v_spec],
        out_shape=[k_out, v_out],
        scratch_shapes=[pltpu.VMEM((2 * groups, n_half, HEAD_DIM), F32)] * 2,
        compiler_params=_params(("parallel", "arbitrary"), VMEM_SMALL_MIB),
        name="nsa_compress",
    )(x, cmp_pos, w1, cmp_w2, k_gain0.reshape(1, HEAD_DIM))


def _cmp_select_body(q_ref, kc_ref, vct_ref, gate_ref, o_ref, sel_ref, *, n_cmp, n_top, q_per_kv):
    tq = q_ref.shape[0]
    first_head = pl.program_id(1) * q_per_kv
    n_pad = kc_ref.shape[0]
    n_slc = sel_ref.shape[0]
    scale = HEAD_DIM ** -0.5
    t = pl.program_id(2) * tq + lax.broadcasted_iota(jnp.int32, (1, tq), 1)
    n = lax.broadcasted_iota(jnp.int32, (n_pad, 1), 0)
    bias = jnp.where(n < n_cmp, jnp.where(n * NSA_CMP_STRIDE + (NSA_CMP_LEN - 1) <= t, 0.0, MASKED_LOGIT),
                     MASKED_LOGIT)
    seen = t >= NSA_CMP_LEN - 1
    kc = kc_ref[...]
    vct = vct_ref[...]
    heads = [slice(r * HEAD_DIM, (r + 1) * HEAD_DIM) for r in range(q_per_kv)]
    logits = [_nt_dot(kc, q_ref[:, sl]) for sl in heads]
    weights, norms = [], []
    for s in logits:
        s = s * scale + bias
        e = jnp.exp(s - jnp.max(s, axis=0, keepdims=True))
        weights.append(e)
        norms.append(jnp.where(seen, 1.0 / jnp.sum(e, axis=0, keepdims=True), 0.0))
    products = [jnp.dot(vct, e.astype(BF16), preferred_element_type=F32) for e in weights]
    p_sum = jnp.zeros((n_pad, tq), F32)
    for r, sl in enumerate(heads):
        gate = gate_ref[pl.ds(3 * (first_head + r), 1), :]
        o_ref[:, sl] = (products[r] * (norms[r] * gate)).T.astype(BF16)
        p_sum = p_sum + weights[r] * norms[r]

    jj = lax.broadcasted_iota(jnp.int32, (n_slc, n_pad), 0) * NSA_SLC_LEN
    nn = lax.broadcasted_iota(jnp.int32, (n_slc, n_pad), 1) * NSA_CMP_STRIDE
    overlap = jnp.where((nn < jj + NSA_SLC_LEN) & (jj < nn + NSA_CMP_LEN)
                        & (nn < n_cmp * NSA_CMP_STRIDE), 1.0, 0.0).astype(BF16)
    p_hi = p_sum.astype(BF16)
    p_lo = (p_sum - p_hi.astype(F32)).astype(BF16)
    imp = (jnp.dot(overlap, p_hi, preferred_element_type=F32)
           + jnp.dot(overlap, p_lo, preferred_element_type=F32))

    cur = t // NSA_SLC_LEN
    jb = lax.broadcasted_iota(jnp.int32, (n_slc, tq), 0)
    val = jnp.where(jb == cur, jnp.inf,
                    jnp.where(jb == 0, jnp.inf, jnp.where(jb < cur, imp, -jnp.inf)))
    rank = _top_rank(val, jb)
    sel_ref[...] = jnp.where(rank < n_top, jnp.where(val > -jnp.inf, 1.0, 0.0), 0.0)


def nsa_cmp_select(q_cmp, k_cmp, v_cmp_t, gates_t, batch, seq, tq=256):
    m, d = q_cmp.shape
    groups = NSA_KV_GROUPS
    q_per_kv = d // HEAD_DIM // groups
    gw = q_per_kv * HEAD_DIM
    tq = min(tq, seq)
    nq = seq // tq
    n_pad = k_cmp.shape[2]
    n_cmp = (seq - NSA_CMP_LEN) // NSA_CMP_STRIDE + 1
    n_slc = seq // NSA_SLC_LEN
    return pl.pallas_call(
        functools.partial(_cmp_select_body, n_cmp=n_cmp, n_top=min(NSA_SLC_TOPK, n_slc),
                          q_per_kv=q_per_kv),
        grid=(batch, groups, nq),
        in_specs=[pl.BlockSpec((tq, gw), lambda b, g, i: (b * nq + i, g)),
                  pl.BlockSpec((None, None, n_pad, HEAD_DIM), lambda b, g, i: (b, g, 0, 0)),
                  pl.BlockSpec((None, None, HEAD_DIM, n_pad), lambda b, g, i: (b, g, 0, 0)),
                  pl.BlockSpec((gates_t.shape[0], tq), lambda b, g, i: (0, b * nq + i))],
        out_specs=[pl.BlockSpec((tq, gw), lambda b, g, i: (b * nq + i, g)),
                   pl.BlockSpec((None, None, n_slc, tq), lambda b, g, i: (b, g, 0, i))],
        out_shape=[jax.ShapeDtypeStruct((m, d), BF16),
                   jax.ShapeDtypeStruct((batch, groups, n_slc, seq), F32)],
        compiler_params=_params(("parallel", "parallel", "parallel"), VMEM_SMALL_MIB),
        name="nsa_cmp_select",
    )(q_cmp, k_cmp, v_cmp_t, gates_t)


def _nsa_attn_body(q_ref, ks_ref, vs_ref, kw_ref, vw_ref, sel_ref, gate_ref, oc_ref, o_ref,
                   vst_ref, vwt_ref, bias_ref, s_ref, p_ref, acc_ref, *, q_per_kv, per):
    tq = q_ref.shape[0]
    kc = KV_CHUNK
    sub = kc // NSA_SLC_LEN
    step = pl.program_id(2)
    q0 = step * tq

    @pl.when(step == 0)
    def _():
        _store_transposed(vs_ref, vst_ref, kc)
        _store_transposed(vw_ref, vwt_ref, kc)

    n_chain = q_per_kv // per
    q_ts = [jnp.concatenate([q_ref[:, r * HEAD_DIM:(r + 1) * HEAD_DIM]
                             for r in range(c * per, (c + 1) * per)], axis=0).astype(F32).T.astype(BF16)
            for c in range(n_chain)]
    t = q0 + lax.broadcasted_iota(jnp.int32, (1, tq), 1)
    bias_ref[...] = jnp.where(sel_ref[...] > 0.5, 0.0, MASKED_LOGIT)

    def chain_bias(bias):
        return jnp.concatenate([bias] * per, axis=1)

    def selected_bias(c, kpos):
        picked = jnp.concatenate(
            [jnp.broadcast_to(bias_ref[pl.ds(c * sub + a, 1), :], (NSA_SLC_LEN, tq)) for a in range(sub)],
            axis=0)
        return chain_bias(jnp.where(kpos <= t, picked, MASKED_LOGIT))

    def window_bias(kpos):
        return chain_bias(jnp.where(kpos <= t, jnp.where(t - kpos < NSA_WINDOW, 0.0, MASKED_LOGIT),
                                    MASKED_LOGIT))

    def selected_logits(c):
        start = pl.multiple_of(c * kc, kc)
        kpos = start + lax.broadcasted_iota(jnp.int32, (kc, 1), 0)
        k, bias = ks_ref[pl.ds(start, kc), :], selected_bias(c, kpos)
        return [(k, q_t, bias) for q_t in q_ts]

    def window_logits(c):
        start = pl.multiple_of(c * kc, kc)
        kpos = start + lax.broadcasted_iota(jnp.int32, (kc, 1), 0)
        k, bias = kw_ref[pl.ds(start, kc), :], window_bias(kpos)
        return [(k, q_t, bias) for q_t in q_ts]

    bufs = [(s_ref.at[c], p_ref.at[c], acc_ref.at[c]) for c in range(2 * n_chain)]

    def selected_only(c, states):
        return _flash_stage(bufs[:n_chain], states, [vst_ref[jnp.maximum(c - 1, 0)]] * n_chain,
                            selected_logits(c + 1))

    def selected_and_window(c, states, is_last=False):
        prev = jnp.maximum(c - 1, 0)
        following = None if is_last else selected_logits(c + 1) + window_logits(c + 1)
        return _flash_stage(bufs, states, [vst_ref[prev]] * n_chain + [vwt_ref[prev]] * n_chain, following)

    first = jnp.maximum(q0 - (NSA_WINDOW - 1), 0) // kc
    last = (q0 + tq - 1) // kc
    states = tuple(_flash_start(buf, *f) for buf, f in zip(bufs[:n_chain], selected_logits(0)))
    states = _paired_loop(0, first, selected_only, states)
    states = states + tuple(_flash_start(buf, *f) for buf, f in zip(bufs[n_chain:], window_logits(first)))
    states = lax.fori_loop(first, last, selected_and_window, states)
    states = selected_and_window(last, states, is_last=True)

    first_head = pl.program_id(1) * q_per_kv
    for c in range(n_chain):
        selected = _flash_finish(bufs[c], states[c], vst_ref[last])
        window = _flash_finish(bufs[n_chain + c], states[n_chain + c], vwt_ref[last])
        for i in range(per):
            r = c * per + i
            lanes = slice(i * tq, (i + 1) * tq)
            cols = slice(r * HEAD_DIM, (r + 1) * HEAD_DIM)
            g_slc = gate_ref[pl.ds(3 * (first_head + r) + 1, 1), :]
            g_win = gate_ref[pl.ds(3 * (first_head + r) + 2, 1), :]
            mixed = (selected[:, lanes] * g_slc + window[:, lanes] * g_win).T
            o_ref[:, cols] = (mixed + oc_ref[:, cols].astype(F32)).astype(BF16)


def nsa_attention(q_rot, kv, sel, gates_t, o_cmp, batch, seq, tq=256):
    m, d = q_rot.shape
    groups = NSA_KV_GROUPS
    q_per_kv = d // HEAD_DIM // groups
    gw = q_per_kv * HEAD_DIM
    tq = min(tq, seq)
    nq = seq // tq
    n_slc = sel.shape[2]
    assert seq % KV_CHUNK == 0 and KV_CHUNK % NSA_SLC_LEN == 0

    def kv_spec(part):
        return pl.BlockSpec((seq, HEAD_DIM), lambda b, g, i: (b, part * groups + g))

    q_spec = pl.BlockSpec((tq, gw), lambda b, g, i: (b * nq + i, g))
    out = jax.ShapeDtypeStruct((m, d), BF16)
    v_t = pltpu.VMEM((seq // KV_CHUNK, HEAD_DIM, KV_CHUNK), BF16)
    per = 1
    return pl.pallas_call(
        functools.partial(_nsa_attn_body, q_per_kv=q_per_kv, per=per),
        grid=(batch, groups, nq),
        in_specs=[q_spec, kv_spec(0), kv_spec(1), kv_spec(2), kv_spec(3),
                  pl.BlockSpec((None, None, n_slc, tq), lambda b, g, i: (b, g, 0, i)),
                  pl.BlockSpec((gates_t.shape[0], tq), lambda b, g, i: (0, b * nq + i)),
                  q_spec],
        out_specs=q_spec,
        out_shape=out,
        scratch_shapes=[v_t, v_t, pltpu.VMEM((n_slc, tq), F32)]
        + _flash_buffers(2 * q_per_kv // per, KV_CHUNK, per * tq),
        compiler_params=_params(("parallel", "parallel", "arbitrary"), VMEM_SMALL_MIB),
        name="nsa_attention",
    )(q_rot, kv, kv, kv, kv, sel, gates_t, o_cmp)


def nsa_layer(h, gain, w_proj, q_gain, k_gain, cmp_pos, cmp_w1, cmp_w2, w_o, tables, batch, seq):
    m, d = h.shape
    n_heads = d // HEAD_DIM
    groups = NSA_KV_GROUPS
    gwk = groups * HEAD_DIM
    tn = COL_TILE
    tm = min(STREAM_ROWS, m)
    assert tn == gwk and 3 * n_heads <= HEAD_DIM
    raw0, kv0, gate0 = d // tn, d // tn + 2, d // tn + 6
    ones = jnp.ones((gwk,), F32)
    gain_row = jnp.concatenate([jnp.tile(q_gain, n_heads), ones, ones,
                                jnp.tile(k_gain[1] * SOFTMAX_EXP2_SCALE, groups), ones,
                                jnp.tile(k_gain[2] * SOFTMAX_EXP2_SCALE, groups), ones, ones])[None]

    def q_epilogue(accs, cols, extra, outs):
        _head_norm_rope_store(accs[0], cols, extra[0], extra[1:], outs[0], outs[1])

    def raw_epilogue(accs, cols, extra, outs):
        outs[2][:, cols] = accs[0]

    def k_epilogue(accs, cols, extra, outs):
        _head_norm_rope_store(accs[0], cols, extra[0], extra[1:], None, outs[3])

    def v_epilogue(accs, cols, extra, outs):
        outs[3][:, cols] = accs[0].astype(BF16)

    def gate_epilogue(accs, cols, extra, outs):
        if cols.start == 0:
            outs[4][...] = jax.nn.sigmoid(accs[0][:, :HEAD_DIM]).T

    def block(first, count):
        return pl.BlockSpec((tm, tn), lambda i, j: (i, jnp.clip(j - first, 0, count - 1)))

    tab = pl.BlockSpec((tm, HEAD_DIM), lambda i, j: (i, 0))
    q_shape = jax.ShapeDtypeStruct((m, d), BF16)
    q_cmp, q_rot, raw, kv, gates_t = norm_matmul(
        h, gain, w_proj, [lambda i, j: (0, j)], gate0 + 1, tn,
        extra=[(gain_row, pl.BlockSpec((1, tn), lambda i, j: (0, j)))] + [(t, tab) for t in tables],
        outs=[(q_shape, block(0, raw0)), (q_shape, block(0, raw0)),
              (jax.ShapeDtypeStruct((m, 2 * gwk), F32), block(raw0, 2)),
              (jax.ShapeDtypeStruct((m, 4 * gwk), BF16), block(kv0, 4)),
              (jax.ShapeDtypeStruct((HEAD_DIM, m), F32), pl.BlockSpec((HEAD_DIM, tm), lambda i, j: (0, i)))],
        variants=[(lambda j: j < raw0, q_epilogue),
                  (lambda j: (j >= raw0) & (j < kv0), raw_epilogue),
                  (lambda j: (j >= kv0) & (j < gate0) & ((j - kv0) % 2 == 0), k_epilogue),
                  (lambda j: (j >= kv0) & (j < gate0) & ((j - kv0) % 2 == 1), v_epilogue),
                  (lambda j: j == gate0, gate_epilogue)],
        sub=2, tm=tm)

    k_cmp, v_cmp = nsa_compress(raw, cmp_pos, cmp_w1, cmp_w2, k_gain[0], batch, seq)
    o_cmp, sel = nsa_cmp_select(q_cmp, k_cmp, v_cmp, gates_t, batch, seq)
    o = nsa_attention(q_rot, kv, sel, gates_t, o_cmp, batch, seq)
    tmo = min(RESIDENT_ROWS, m)
    return out_matmul([(o, pl.BlockSpec((tmo, d), lambda i: (i, 0)))],
                      lambda ins: ins[0][...], w_o, h, tm=tmo)


def kernel(x, p, positions, mixer_norm, mlp_norm, mlp_w1, mlp_w2, ple_norm, ple_gate, ple_proj,
           moba_w_qkv, moba_q_gain, moba_k_gain, moba_w_o, pool_w, pool_scale,
           nsa_w_q, nsa_w_kv, nsa_q_gain, nsa_k_gain, nsa_cmp_pos, nsa_cmp_w1, nsa_cmp_w2,
           nsa_w_gate, nsa_w_o, conv_w_in, conv_w, conv_b, conv_w_o):
    batch, seq, d = x.shape
    depth = p.shape[0]
    m = batch * seq
    n_heads = d // HEAD_DIM
    gwk = NSA_KV_GROUPS * HEAD_DIM
    tables = rope_tables(positions)
    bf = lambda w: w.astype(BF16)
    ple_gate_bf, ple_proj_bf = bf(ple_gate), bf(ple_proj)
    p_rows = p.reshape(depth, m, -1)

    h = x.reshape(m, d)
    for i in range(depth):
        kind, j = i % 4, i // 4
        if kind == 0:
            h = moba_layer(h, mixer_norm[i], bf(moba_w_qkv[j]), moba_q_gain[j], moba_k_gain[j],
                           bf(moba_w_o[j]), tables, batch, seq)
        elif kind == 1:
            h = pool_layer(h, mixer_norm[i], bf(pool_w[j]), pool_scale[j], seq)
        elif kind == 2:
            w_gate = jnp.pad(nsa_w_gate[j], ((0, 0), (0, gwk - 3 * n_heads)))
            w_proj = bf(jnp.concatenate([nsa_w_q[j], nsa_w_kv[j], w_gate], axis=1))
            h = nsa_layer(h, mixer_norm[i], w_proj, nsa_q_gain[j], nsa_k_gain[j], nsa_cmp_pos[j],
                          bf(nsa_cmp_w1[j]), bf(nsa_cmp_w2[j]), bf(nsa_w_o[j]), tables, batch, seq)
        else:
            h = conv_layer(h, mixer_norm[i], bf(conv_w_in[j]), conv_w[j], conv_b[j], bf(conv_w_o[j]), seq)
        h = mlp_layer(h, mlp_norm[i], mlp_w1, mlp_w2, i)
        h = ple_layer(h, ple_norm[i], ple_gate_bf, p_rows, ple_proj_bf, i)
    return h.reshape(batch, seq, d)
```

```python
import functools

import jax
import jax.numpy as jnp
from jax import lax
from jax.experimental import pallas as pl
from jax.experimental.pallas import tpu as pltpu

F32 = jnp.float32
BF16 = jnp.bfloat16

HEAD_DIM = 128
ROT_DIM = HEAD_DIM // 4
ROPE_THETA = 500000.0
NORM_EPS = 1e-6
MOBA_BLOCK = 256
MOBA_TOPK = 3
POOL_WINDOWS = (2, 4, 8, 16)
NSA_KV_GROUPS = 4
NSA_CMP_LEN = 32
NSA_CMP_STRIDE = 16
NSA_SLC_LEN = 64
NSA_SLC_TOPK = 16
NSA_WINDOW = 512
CONV_WIDTH = 3

MASKED_LOGIT = -1e30
LN2 = 0.6931471805599453
SOFTMAX_EXP2_SCALE = HEAD_DIM ** -0.5 / LN2
KV_CHUNK = 256
MIB = 1024 * 1024

STREAM_ROWS = 1024
RESIDENT_ROWS = 512
COL_TILE = 512
VMEM_BIG_MIB, VMEM_MID_MIB, VMEM_SMALL_MIB = 56, 48, 32


def _params(semantics, vmem_mib):
    return pltpu.CompilerParams(dimension_semantics=semantics,
                                vmem_limit_bytes=vmem_mib * MIB)


def _resident(shape, index_map):
    return pl.BlockSpec(shape, index_map, pipeline_mode=pl.Buffered(1))


def _rms(x, gain):
    ms = jnp.mean(x * x, axis=-1, keepdims=True)
    return x * lax.rsqrt(ms + NORM_EPS) * gain


def _split_bf16(x):
    hi = x.astype(BF16)
    return hi, (x - hi.astype(F32)).astype(BF16)


def _rope(x, cos, sin):
    half = ROT_DIM // 2
    lane = lax.broadcasted_iota(jnp.int32, x.shape, 1)
    partner = jnp.where(lane < half, -pltpu.roll(x, HEAD_DIM - half, 1), pltpu.roll(x, half, 1))
    return x * cos + partner * sin


def _nt_dot(a, b):
    return lax.dot_general(a, b, (((1,), (1,)), ((), ())), preferred_element_type=F32)


def _rope_table_body(pos_ref, freq_ref, cos_ref, sin_ref):
    ang = pos_ref[...].astype(F32) * freq_ref[...]
    cos_ref[...] = jnp.cos(ang)
    sin_ref[...] = jnp.sin(ang)


def rope_tables(positions):
    m = positions.size
    half = ROT_DIM // 2
    freqs = jnp.float32(ROPE_THETA) ** (-jnp.arange(half, dtype=F32) * 2.0 / ROT_DIM)
    freq_row = jnp.concatenate([freqs, freqs, jnp.zeros((HEAD_DIM - ROT_DIM,), F32)])[None]
    tm = min(m, STREAM_ROWS)
    tab = pl.BlockSpec((tm, HEAD_DIM), lambda i: (i, 0))
    return pl.pallas_call(
        _rope_table_body,
        grid=(m // tm,),
        in_specs=[pl.BlockSpec((tm, 1), lambda i: (i, 0)), pl.BlockSpec((1, HEAD_DIM), lambda i: (0, 0))],
        out_specs=[tab, tab],
        out_shape=[jax.ShapeDtypeStruct((m, HEAD_DIM), F32)] * 2,
        compiler_params=_params(("parallel",), VMEM_SMALL_MIB),
        name="rope_tables",
    )(positions.reshape(m, 1), freq_row)


def _nmm_body(*refs, n_w, n_extra, n_out, sub, variants, on_row_start):
    x_ref, g_ref = refs[0], refs[1]
    w_refs = refs[2:2 + n_w]
    extra = refs[2 + n_w:2 + n_w + n_extra]
    outs = refs[2 + n_w + n_extra:2 + n_w + n_extra + n_out]
    xn_ref = refs[2 + n_w + n_extra + n_out]
    extra = extra + refs[3 + n_w + n_extra + n_out:]
    j = pl.program_id(1)

    @pl.when(j == 0)
    def _():
        xn_ref[...] = _rms(x_ref[...], g_ref[...]).astype(BF16)
        if on_row_start is not None:
            on_row_start(extra)

    width = w_refs[0].shape[1] // sub
    cols = [slice(s * width, (s + 1) * width) for s in range(sub)]

    def run(epilogue):
        xn = xn_ref[...]
        accs = [[jnp.dot(xn, w[:, c], preferred_element_type=F32) for w in w_refs] for c in cols]
        for c, acc in zip(cols, accs):
            epilogue(acc, c, extra, outs)

    if len(variants) == 1:
        run(variants[0][1])
    else:
        for applies, epilogue in variants:
            pl.when(applies(j))(functools.partial(run, epilogue))


def norm_matmul(h, gain, w, w_maps, n_col_tiles, tn, extra, outs, variants, sub=1, tm=STREAM_ROWS, vmem_mib=VMEM_BIG_MIB,
                scratch=(), on_row_start=None):
    m, d = h.shape
    tm = min(tm, m)
    in_specs = [pl.BlockSpec((tm, d), lambda i, j: (i, 0)),
                pl.BlockSpec((1, d), lambda i, j: (0, 0))]
    in_specs += [pl.BlockSpec((d, tn), wm) for wm in w_maps]
    in_specs += [spec for _, spec in extra]
    return pl.pallas_call(
        functools.partial(_nmm_body, n_w=len(w_maps), n_extra=len(extra), n_out=len(outs), sub=sub,
                          variants=variants, on_row_start=on_row_start),
        grid=(m // tm, n_col_tiles),
        in_specs=in_specs,
        out_specs=[spec for _, spec in outs],
        out_shape=[shape for shape, _ in outs],
        scratch_shapes=[pltpu.VMEM((tm, d), BF16)] + list(scratch),
        compiler_params=_params(("arbitrary" if scratch else "parallel", "arbitrary"), vmem_mib),
        name="norm_matmul",
    )(h, gain.reshape(1, d), *([w] * len(w_maps)), *[a for a, _ in extra])


def _head_norm_rope_store(acc, cols, gain_ref, table_refs, out_plain, out_rot):
    width = acc.shape[1]
    same_head = (lax.broadcasted_iota(jnp.int32, (width, width), 0) // HEAD_DIM
                 == lax.broadcasted_iota(jnp.int32, (width, width), 1) // HEAD_DIM)
    ones = jnp.where(same_head, 1.0, 0.0).astype(BF16)
    ssq = jnp.dot((acc * acc).astype(BF16), ones, preferred_element_type=F32)
    normed = acc * lax.rsqrt(ssq * (1.0 / HEAD_DIM) + NORM_EPS) * gain_ref[:, cols]
    if out_plain is not None:
        out_plain[:, cols] = normed.astype(BF16)
    if out_rot is not None:
        cos, sin = table_refs[0][...], table_refs[1][...]
        for hh in range(width // HEAD_DIM):
            sl = slice(hh * HEAD_DIM, (hh + 1) * HEAD_DIM)
            dst = slice(cols.start + hh * HEAD_DIM, cols.start + (hh + 1) * HEAD_DIM)
            out_rot[:, dst] = _rope(normed[:, sl], cos, sin).astype(BF16)


def _omm_body(*refs, n_in, prologue):
    ins = refs[:n_in]
    w_ref, res_ref, o_ref = refs[n_in:n_in + 3]
    a = prologue(ins)
    o_ref[...] = res_ref[...] + jnp.dot(a, w_ref[...], preferred_element_type=F32)


def out_matmul(ins, prologue, w, res, tm=RESIDENT_ROWS, vmem_mib=VMEM_MID_MIB):
    m, d = res.shape
    tm = min(tm, m)
    k = w.shape[0]
    return pl.pallas_call(
        functools.partial(_omm_body, n_in=len(ins), prologue=prologue),
        grid=(m // tm,),
        in_specs=[spec for _, spec in ins] + [
            _resident((k, d), lambda i: (0, 0)),
            pl.BlockSpec((tm, d), lambda i: (i, 0))],
        out_specs=pl.BlockSpec((tm, d), lambda i: (i, 0)),
        out_shape=jax.ShapeDtypeStruct((m, d), F32),
        compiler_params=_params(("parallel",), vmem_mib),
        name="out_matmul",
    )(*[a for a, _ in ins], w, res)


def _mlp_body(x_ref, g_ref, w1_ref, w2_ref, o_ref, xn_ref):
    @pl.when(pl.program_id(1) == 0)
    def _():
        x = x_ref[...]
        xn_ref[...] = _rms(x, g_ref[...]).astype(BF16)
        o_ref[...] = x

    a = jnp.dot(xn_ref[...], w1_ref[...].astype(BF16), preferred_element_type=F32)
    a = jnp.square(jnp.maximum(a, 0.0)).astype(BF16)
    o_ref[...] += jnp.dot(a, w2_ref[...].astype(BF16), preferred_element_type=F32)


def mlp_layer(h, gain, w1, w2, layer, tm=STREAM_ROWS, tf=COL_TILE):
    m, d = h.shape
    dff = w1.shape[2]
    tm = min(tm, m)
    return pl.pallas_call(
        _mlp_body,
        grid=(m // tm, dff // tf),
        in_specs=[pl.BlockSpec((tm, d), lambda i, f: (i, 0)),
                  pl.BlockSpec((1, d), lambda i, f: (0, 0)),
                  pl.BlockSpec((None, d, tf), lambda i, f: (layer, 0, f)),
                  pl.BlockSpec((None, tf, d), lambda i, f: (layer, f, 0))],
        out_specs=pl.BlockSpec((tm, d), lambda i, f: (i, 0)),
        out_shape=jax.ShapeDtypeStruct((m, d), F32),
        scratch_shapes=[pltpu.VMEM((tm, d), BF16)],
        compiler_params=_params(("parallel", "arbitrary"), VMEM_BIG_MIB),
        name="mlp",
    )(h, gain.reshape(1, d), w1, w2)


def _ple_body(x_ref, g_ref, wg_ref, p_ref, wp_ref, o_ref):
    x = x_ref[...]
    xn = _rms(x, g_ref[...]).astype(BF16)
    gate = jax.nn.sigmoid(jnp.dot(xn, wg_ref[...], preferred_element_type=F32))
    emb = jnp.dot(p_ref[...].astype(BF16), wp_ref[...], preferred_element_type=F32)
    o_ref[...] = x + gate * emb


def ple_layer(h, gain, wg, p, wp, layer, tm=RESIDENT_ROWS):
    m, d = h.shape
    pd = p.shape[2]
    tm = min(tm, m)
    return pl.pallas_call(
        _ple_body,
        grid=(m // tm,),
        in_specs=[pl.BlockSpec((tm, d), lambda i: (i, 0)),
                  pl.BlockSpec((1, d), lambda i: (0, 0)),
                  _resident((None, d, d), lambda i: (layer, 0, 0)),
                  pl.BlockSpec((None, tm, pd), lambda i: (layer, i, 0)),
                  _resident((None, pd, d), lambda i: (layer, 0, 0))],
        out_specs=pl.BlockSpec((tm, d), lambda i: (i, 0)),
        out_shape=jax.ShapeDtypeStruct((m, d), F32),
        compiler_params=_params(("parallel",), VMEM_MID_MIB),
        name="ple",
    )(h, gain.reshape(1, d), wg, p, wp)


def _flash_buffers(chains, keys, queries):
    return [pltpu.VMEM((chains, keys, queries), F32), pltpu.VMEM((chains, keys, queries), BF16),
            pltpu.VMEM((chains, HEAD_DIM, queries), F32)]


def _flash_keep(s_ref, s, bias):
    if bias.shape[0] == 1:
        s_ref[...] = s
        raw_max = jnp.max(s, axis=0, keepdims=True)
        return raw_max + bias, raw_max - 2.0 * bias
    s = s + bias
    s_ref[...] = s
    s_max = jnp.max(s, axis=0, keepdims=True)
    return s_max, s_max


def _flash_start(buf, k, q_t, bias):
    s_ref, p_ref, acc_ref = buf
    p_ref[...] = jnp.zeros_like(p_ref)
    acc_ref[...] = jnp.zeros_like(acc_ref)
    queries = s_ref.shape[1]
    return _flash_keep(s_ref, jnp.dot(k, q_t, preferred_element_type=F32), bias) + (
        jnp.ones((1, queries), F32), jnp.full((1, queries), MASKED_LOGIT, F32), jnp.zeros((1, queries), F32))


def _flash_stage(bufs, states, v_prev, following):
    products = [jnp.dot(v, buf[1][...], preferred_element_type=F32) for v, buf in zip(v_prev, bufs)]
    if following is not None:
        upcoming = [jnp.dot(k, q_t, preferred_element_type=F32) for k, q_t, _ in following]
    out = []
    for c, ((s_ref, p_ref, acc_ref), state, pv) in enumerate(zip(bufs, states, products)):
        s_max, floor, alpha_prev, m, l = state
        m_new = jnp.maximum(m, s_max)
        p = jnp.exp2(s_ref[...] - jnp.maximum(m_new, floor))
        alpha = jnp.exp((m - m_new) * LN2)
        l = alpha * l + jnp.sum(p, axis=0, keepdims=True)
        p_ref[...] = p.astype(BF16)
        acc_ref[...] = alpha_prev * acc_ref[...] + pv
        kept = (s_max, floor) if following is None else _flash_keep(s_ref, upcoming[c], following[c][2])
        out.append(kept + (alpha, m_new, l))
    return tuple(out)


def _paired_loop(lo, hi, stage, states):
    count = jnp.maximum(hi - lo, 0)

    def two(i, st):
        c = lo + 2 * i
        return stage(c + 1, stage(c, st))

    states = lax.fori_loop(0, count // 2, two, states)
    return lax.cond(count % 2 == 1, lambda st: stage(hi - 1, st), lambda st: st, states)


def _flash_finish(buf, state, v_last):
    _, p_ref, acc_ref = buf
    alpha, l = state[2], state[4]
    return (alpha * acc_ref[...] + jnp.dot(v_last, p_ref[...], preferred_element_type=F32)) / l


def _store_transposed(v_ref, vt_ref, chunk):
    for n in range(vt_ref.shape[0]):
        vt_ref[n] = v_ref[n * chunk:(n + 1) * chunk, :].astype(F32).T.astype(BF16)


def _top_rank(v, row):
    n = v.shape[0]
    if n % 8:
        rank = jnp.zeros(v.shape, jnp.int32)
        for j in range(n):
            r = v[j:j + 1, :]
            rank = rank + jnp.where(row > j, jnp.where(r >= v, 1, 0), jnp.where(r > v, 1, 0))
        return rank
    starts = range(0, n, 8)
    groups = [v[lo:lo + 8, :] for lo in starts]
    ranks = [jnp.zeros((8, v.shape[1]), jnp.int32) for _ in starts]
    for j in range(n):
        r = jnp.broadcast_to(v[j:j + 1, :], (8, v.shape[1]))
        for g, lo in enumerate(starts):
            if lo > j:
                ranks[g] = ranks[g] + jnp.where(r >= groups[g], 1, 0)
            elif lo + 7 <= j:
                ranks[g] = ranks[g] + jnp.where(r > groups[g], 1, 0)
            else:
                below = lax.broadcasted_iota(jnp.int32, (8, v.shape[1]), 0) > j - lo
                ranks[g] = ranks[g] + jnp.where(below, jnp.where(r >= groups[g], 1, 0),
                                                jnp.where(r > groups[g], 1, 0))
    return jnp.concatenate(ranks, axis=0)


def _moba_body(q_ref, k_ref, v_ref, o_ref, kmean_ref, vt_ref, bias_ref, s_ref, p_ref, acc_ref,
               *, n_blocks, heads):
    blk = MOBA_BLOCK
    own = pl.program_id(2)
    cols = [slice(e * HEAD_DIM, (e + 1) * HEAD_DIM) for e in range(heads)]
    bufs = [(s_ref.at[e], p_ref.at[e], acc_ref.at[e]) for e in range(heads)]

    @pl.when(own == 0)
    def _():
        for e in range(heads):
            for n in range(n_blocks):
                kb = k_ref[n * blk:(n + 1) * blk, cols[e]].astype(F32)
                kmean_ref[e, n:n + 1, :] = jnp.mean(kb, axis=0, keepdims=True)
                vt_ref[e, n] = v_ref[n * blk:(n + 1) * blk, cols[e]].astype(F32).T.astype(BF16)

    q_ts = [q_ref[:, cols[e]].astype(F32).T.astype(BF16) for e in range(heads)]
    for e in range(heads):
        gate = jnp.dot(kmean_ref[e].astype(BF16), q_ts[e], preferred_element_type=F32)
        row = lax.broadcasted_iota(jnp.int32, gate.shape, 0)
        gate = jnp.where(row < own, gate, -jnp.inf)
        rank = _top_rank(gate, row)
        bias = jnp.where(row < own, jnp.where(rank < MOBA_TOPK, 0.0, MASKED_LOGIT), MASKED_LOGIT)
        for n in range(n_blocks):
            bias_ref[e, n] = jnp.broadcast_to(bias[n:n + 1, :], (8, blk))

    def keys(j):
        start = pl.multiple_of(j * blk, blk)
        return [k_ref[pl.ds(start, blk), cols[e]] for e in range(heads)]

    def values(j):
        return [vt_ref[e, j] for e in range(heads)]

    def stage(j, states, next_biases):
        following = None if next_biases is None else list(zip(keys(j + 1), q_ts, next_biases))
        return _flash_stage(bufs, states, values(jnp.maximum(j - 1, 0)), following)

    def chosen_bias(j):
        return [jnp.max(bias_ref[e, j], axis=0, keepdims=True) for e in range(heads)]

    causal = jnp.where(lax.broadcasted_iota(jnp.int32, (blk, blk), 0)
                       <= lax.broadcasted_iota(jnp.int32, (blk, blk), 1), 0.0, MASKED_LOGIT)
    first_bias = [jnp.where(own == 0, causal, b) for b in chosen_bias(0)]
    states = tuple(_flash_start(buf, k, q_t, b) for buf, k, q_t, b in zip(bufs, keys(0), q_ts, first_bias))
    states = _paired_loop(0, own - 1, lambda j, st: stage(j, st, chosen_bias(j + 1)), states)
    states = lax.cond(own > 0, lambda st: stage(own - 1, st, [causal] * heads), lambda st: st, states)
    states = stage(own, states, None)
    for e, v_last in enumerate(values(own)):
        o_ref[:, cols[e]] = _flash_finish(bufs[e], states[e], v_last).T.astype(BF16)


def moba_attention(qkv, batch, seq, n_heads, heads_per_step=4):
    blk = MOBA_BLOCK
    nq = seq // blk
    hp = heads_per_step
    width = hp * HEAD_DIM
    groups = n_heads // hp
    return pl.pallas_call(
        functools.partial(_moba_body, n_blocks=nq, heads=hp),
        grid=(batch, groups, nq),
        in_specs=[pl.BlockSpec((blk, width), lambda b, h, i: (b * nq + i, h)),
                  pl.BlockSpec((seq, width), lambda b, h, i: (b, groups + h)),
                  pl.BlockSpec((seq, width), lambda b, h, i: (b, 2 * groups + h))],
        out_specs=pl.BlockSpec((blk, width), lambda b, h, i: (b * nq + i, h)),
        out_shape=jax.ShapeDtypeStruct((batch * seq, n_heads * HEAD_DIM), BF16),
        scratch_shapes=[pltpu.VMEM((hp, nq, HEAD_DIM), F32),
                        pltpu.VMEM((hp, nq, HEAD_DIM, blk), BF16),
                        pltpu.VMEM((hp, nq, 8, blk), F32)] + _flash_buffers(hp, blk, blk),
        compiler_params=_params(("parallel", "parallel", "arbitrary"), VMEM_SMALL_MIB),
        name="moba_attention",
    )(qkv, qkv, qkv)


def moba_layer(h, gain, w_qkv, q_gain, k_gain, w_o, tables, batch, seq):
    m, d = h.shape
    n_heads = d // HEAD_DIM
    tn = 2 * COL_TILE
    n_qk_tiles = 2 * d // tn
    gain_row = jnp.concatenate([jnp.tile(q_gain, n_heads), jnp.tile(k_gain * SOFTMAX_EXP2_SCALE, n_heads),
                                jnp.ones((d,), F32)])[None]

    def qk_epilogue(accs, cols, extra, outs):
        _head_norm_rope_store(accs[0], cols, extra[0], extra[1:], None, outs[0])

    def v_epilogue(accs, cols, extra, outs):
        outs[0][:, cols] = accs[0].astype(BF16)

    tm = min(STREAM_ROWS, m)
    tab = pl.BlockSpec((tm, HEAD_DIM), lambda i, j: (i, 0))
    (qkv,) = norm_matmul(
        h, gain, w_qkv, [lambda i, j: (0, j)], 3 * d // tn, tn,
        extra=[(gain_row, pl.BlockSpec((1, tn), lambda i, j: (0, j)))] + [(t, tab) for t in tables],
        outs=[(jax.ShapeDtypeStruct((m, 3 * d), BF16), pl.BlockSpec((tm, tn), lambda i, j: (i, j)))],
        variants=[(lambda j: j < n_qk_tiles, qk_epilogue), (lambda j: j >= n_qk_tiles, v_epilogue)],
        sub=4, tm=tm)
    o = moba_attention(qkv, batch, seq, n_heads)
    tmo = min(RESIDENT_ROWS, m)
    return out_matmul([(o, pl.BlockSpec((tmo, d), lambda i: (i, 0)))],
                      lambda ins: ins[0][...], w_o, h, tm=tmo)


def _pool_body(x_ref, halo_ref, g_ref, w_ref, s_ref, o_ref, *, tiles_per_seq, halo):
    i = pl.program_id(0)
    tm, d = x_ref.shape
    group = d // len(POOL_WINDOWS)
    x = x_ref[...]
    gain = g_ref[...]
    xn = _rms(x, gain)
    prev = jnp.where(i % tiles_per_seq == 0, 0.0, _rms(halo_ref[...], gain))
    pos = (i % tiles_per_seq) * tm + lax.broadcasted_iota(jnp.int32, (tm, 1), 0)
    for g, win in enumerate(POOL_WINDOWS):
        sl = slice(g * group, (g + 1) * group)
        run = jnp.concatenate([prev[:, sl], xn[:, sl]], axis=0)
        span = 1
        while span < win:
            run = run + pltpu.roll(run, span, 0)
            span *= 2
        cnt = jnp.minimum(pos + 1, win).astype(F32)
        mean = run[halo:, :] / cnt
        mix = jnp.dot((mean - xn[:, sl]).astype(BF16), w_ref[g], preferred_element_type=F32)
        o_ref[:, sl] = x[:, sl] + mix * s_ref[:, sl]


def pool_layer(h, gain, w_groups, scale, seq, tm=RESIDENT_ROWS):
    m, d = h.shape
    halo = 16
    assert max(POOL_WINDOWS) <= halo
    tm = min(tm, seq)
    group = d // len(POOL_WINDOWS)
    return pl.pallas_call(
        functools.partial(_pool_body, tiles_per_seq=seq // tm, halo=halo),
        grid=(m // tm,),
        in_specs=[pl.BlockSpec((tm, d), lambda i: (i, 0)),
                  pl.BlockSpec((halo, d), lambda i: (jnp.maximum(i * (tm // halo) - 1, 0), 0)),
                  pl.BlockSpec((1, d), lambda i: (0, 0)),
                  _resident((len(POOL_WINDOWS), group, group), lambda i: (0, 0, 0)),
                  pl.BlockSpec((1, d), lambda i: (0, 0))],
        out_specs=pl.BlockSpec((tm, d), lambda i: (i, 0)),
        out_shape=jax.ShapeDtypeStruct((m, d), F32),
        compiler_params=_params(("parallel",), VMEM_MID_MIB),
        name="pool_mixer",
    )(h, h, gain.reshape(1, d), w_groups, scale.reshape(1, d))


def conv_layer(h, gain, w_in, conv_w, conv_b, w_o, seq):
    m, d = h.shape
    tn = COL_TILE
    nj = d // tn

    tm = min(STREAM_ROWS, seq)
    tiles_per_seq = seq // tm
    halo = 8

    def on_row_start(extra):
        @pl.when(pl.program_id(0) % tiles_per_seq == 0)
        def _():
            extra[2][...] = jnp.zeros_like(extra[2])

    def epilogue(accs, cols, extra, outs):
        cw_ref, cb_ref, carry_ref = extra
        j = pl.program_id(1)
        u0 = accs[1] * accs[2]
        prev = carry_ref[j, :, cols]
        row = lax.broadcasted_iota(jnp.int32, u0.shape, 0)
        u1 = jnp.where(row == 0, prev[halo - 1:halo, :], pltpu.roll(u0, 1, 0))
        u2 = jnp.where(row == 0, prev[halo - 2:halo - 1, :],
                       jnp.where(row == 1, prev[halo - 1:halo, :], pltpu.roll(u0, 2, 0)))
        conv = (cw_ref[0:1, cols] * u2 + cw_ref[1:2, cols] * u1 + cw_ref[2:3, cols] * u0
                + cb_ref[:, cols])
        outs[0][:, cols] = (accs[0] * conv).astype(BF16)
        carry_ref[j, :, cols] = u0[tm - halo:, :]

    (y,) = norm_matmul(
        h, gain, w_in, [lambda i, j: (0, j), lambda i, j: (0, j + nj), lambda i, j: (0, j + 2 * nj)],
        nj, tn,
        extra=[(conv_w, pl.BlockSpec((CONV_WIDTH, tn), lambda i, j: (0, j))),
               (conv_b.reshape(1, d), pl.BlockSpec((1, tn), lambda i, j: (0, j)))],
        outs=[(jax.ShapeDtypeStruct((m, d), BF16), pl.BlockSpec((tm, tn), lambda i, j: (i, j)))],
        variants=[(None, epilogue)], sub=2, tm=tm,
        scratch=[pltpu.VMEM((nj, halo, tn), F32)], on_row_start=on_row_start)

    tmo = min(RESIDENT_ROWS, m)
    return out_matmul([(y, pl.BlockSpec((tmo, d), lambda i: (i, 0)))],
                      lambda ins: ins[0][...], w_o, h, tm=tmo)


def _compress_body(x_ref, pos_ref, w1_ref, w2_ref, g_ref, kc_ref, vc_ref, lo_ref, hi_ref, *, groups):
    step = pl.program_id(1)
    stride = NSA_CMP_STRIDE

    @pl.when(step == 0)
    def _():
        lo_ref[...] = jnp.zeros_like(lo_ref)
        hi_ref[...] = jnp.zeros_like(hi_ref)

    for kv in range(2):
        p_lo = pos_ref[kv, pl.ds(step, 1), :]
        p_hi = pos_ref[kv, pl.ds(stride + step, 1), :]
        w_lo = w1_ref[kv, step]
        w_hi = w1_ref[kv, stride + step]
        for g in range(groups):
            c = kv * groups + g
            t = x_ref[:, c * HEAD_DIM:(c + 1) * HEAD_DIM]
            lo_ref[c] += jnp.dot((t + p_lo).astype(BF16), w_lo, preferred_element_type=F32)
            hi_ref[c] += jnp.dot((t + p_hi).astype(BF16), w_hi, preferred_element_type=F32)

    @pl.when(step == stride - 1)
    def _():
        n_half = lo_ref.shape[1]
        for kv in range(2):
            for g in range(groups):
                c = kv * groups + g
                pre = lo_ref[c] + pltpu.roll(hi_ref[c], n_half - 1, 0)
                hid = jax.nn.gelu(pre)
                if kv == 0:
                    out = jnp.dot(hid.astype(BF16), w2_ref[0], preferred_element_type=F32)
                    kc_ref[g] = _rms(out, g_ref[...]).astype(BF16)
                else:
                    vc_ref[g] = jnp.dot(w2_ref[1], hid.T.astype(BF16),
                                        preferred_element_type=F32).astype(BF16)


def nsa_compress(raw, cmp_pos, cmp_w1, cmp_w2, k_gain0, batch, seq):
    assert NSA_CMP_LEN == 2 * NSA_CMP_STRIDE
    stride = NSA_CMP_STRIDE
    groups = NSA_KV_GROUPS
    width = raw.shape[1]
    n_half = seq // stride
    x = raw.reshape(batch * n_half, stride * width)
    w1 = cmp_w1.reshape(2, NSA_CMP_LEN, HEAD_DIM, HEAD_DIM)
    cmp_w2 = jnp.stack([cmp_w2[0], cmp_w2[1].T])
    k_out = jax.ShapeDtypeStruct((batch, groups, n_half, HEAD_DIM), BF16)
    v_out = jax.ShapeDtypeStruct((batch, groups, HEAD_DIM, n_half), BF16)
    k_spec = pl.BlockSpec((None, groups, n_half, HEAD_DIM), lambda b, s: (b, 0, 0, 0))
    v_spec = pl.BlockSpec((None, groups, HEAD_DIM, n_half), lambda b, s: (b, 0, 0, 0))
    return pl.pallas_call(
        functools.partial(_compress_body, groups=groups),
        grid=(batch, stride),
        in_specs=[pl.BlockSpec((n_half, width), lambda b, s: (b, s)),
                  pl.BlockSpec((2, NSA_CMP_LEN, HEAD_DIM), lambda b, s: (0, 0, 0)),
                  pl.BlockSpec((2, NSA_CMP_LEN, HEAD_DIM, HEAD_DIM), lambda b, s: (0, 0, 0, 0)),
                  pl.BlockSpec((2, HEAD_DIM, HEAD_DIM), lambda b, s: (0, 0, 0)),
                  pl.BlockSpec((1, HEAD_DIM), lambda b, s: (0, 0))],
        out_specs=[k_spec, v_spec],
        out_shape=[k_out, v_out],
        scratch_shapes=[pltpu.VMEM((2 * groups, n_half, HEAD_DIM), F32)] * 2,
        compiler_params=_params(("parallel", "arbitrary"), VMEM_SMALL_MIB),
        name="nsa_compress",
    )(x, cmp_pos, w1, cmp_w2, k_gain0.reshape(1, HEAD_DIM))


def _cmp_select_body(q_ref, kc_ref, vct_ref, gate_ref, o_ref, sel_ref, *, n_cmp, n_top, q_per_kv):
    tq = q_ref.shape[0]
    first_head = pl.program_id(1) * q_per_kv
    n_pad = kc_ref.shape[0]
    n_slc = sel_ref.shape[0]
    scale = HEAD_DIM ** -0.5
    t = pl.program_id(2) * tq + lax.broadcasted_iota(jnp.int32, (1, tq), 1)
    n = lax.broadcasted_iota(jnp.int32, (n_pad, 1), 0)
    bias = jnp.where(n < n_cmp, jnp.where(n * NSA_CMP_STRIDE + (NSA_CMP_LEN - 1) <= t, 0.0, MASKED_LOGIT),
                     MASKED_LOGIT)
    seen = t >= NSA_CMP_LEN - 1
    kc = kc_ref[...]
    vct = vct_ref[...]
    heads = [slice(r * HEAD_DIM, (r + 1) * HEAD_DIM) for r in range(q_per_kv)]
    logits = [_nt_dot(kc, q_ref[:, sl]) for sl in heads]
    weights, norms = [], []
    for s in logits:
        s = s * scale + bias
        e = jnp.exp(s - jnp.max(s, axis=0, keepdims=True))
        weights.append(e)
        norms.append(jnp.where(seen, 1.0 / jnp.sum(e, axis=0, keepdims=True), 0.0))
    products = [jnp.dot(vct, e.astype(BF16), preferred_element_type=F32) for e in weights]
    p_sum = jnp.zeros((n_pad, tq), F32)
    for r, sl in enumerate(heads):
        gate = gate_ref[pl.ds(3 * (first_head + r), 1), :]
        o_ref[:, sl] = (products[r] * (norms[r] * gate)).T.astype(BF16)
        p_sum = p_sum + weights[r] * norms[r]

    jj = lax.broadcasted_iota(jnp.int32, (n_slc, n_pad), 0) * NSA_SLC_LEN
    nn = lax.broadcasted_iota(jnp.int32, (n_slc, n_pad), 1) * NSA_CMP_STRIDE
    overlap = jnp.where((nn < jj + NSA_SLC_LEN) & (jj < nn + NSA_CMP_LEN)
                        & (nn < n_cmp * NSA_CMP_STRIDE), 1.0, 0.0).astype(BF16)
    p_hi = p_sum.astype(BF16)
    p_lo = (p_sum - p_hi.astype(F32)).astype(BF16)
    imp = (jnp.dot(overlap, p_hi, preferred_element_type=F32)
           + jnp.dot(overlap, p_lo, preferred_element_type=F32))

    cur = t // NSA_SLC_LEN
    jb = lax.broadcasted_iota(jnp.int32, (n_slc, tq), 0)
    val = jnp.where(jb == cur, jnp.inf,
                    jnp.where(jb == 0, jnp.inf, jnp.where(jb < cur, imp, -jnp.inf)))
    rank = _top_rank(val, jb)
    sel_ref[...] = jnp.where(rank < n_top, jnp.where(val > -jnp.inf, 1.0, 0.0), 0.0)


def nsa_cmp_select(q_cmp, k_cmp, v_cmp_t, gates_t, batch, seq, tq=256):
    m, d = q_cmp.shape
    groups = NSA_KV_GROUPS
    q_per_kv = d // HEAD_DIM // groups
    gw = q_per_kv * HEAD_DIM
    tq = min(tq, seq)
    nq = seq // tq
    n_pad = k_cmp.shape[2]
    n_cmp = (seq - NSA_CMP_LEN) // NSA_CMP_STRIDE + 1
    n_slc = seq // NSA_SLC_LEN
    return pl.pallas_call(
        functools.partial(_cmp_select_body, n_cmp=n_cmp, n_top=min(NSA_SLC_TOPK, n_slc),
                          q_per_kv=q_per_kv),
        grid=(batch, groups, nq),
        in_specs=[pl.BlockSpec((tq, gw), lambda b, g, i: (b * nq + i, g)),
                  pl.BlockSpec((None, None, n_pad, HEAD_DIM), lambda b, g, i: (b, g, 0, 0)),
                  pl.BlockSpec((None, None, HEAD_DIM, n_pad), lambda b, g, i: (b, g, 0, 0)),
                  pl.BlockSpec((gates_t.shape[0], tq), lambda b, g, i: (0, b * nq + i))],
        out_specs=[pl.BlockSpec((tq, gw), lambda b, g, i: (b * nq + i, g)),
                   pl.BlockSpec((None, None, n_slc, tq), lambda b, g, i: (b, g, 0, i))],
        out_shape=[jax.ShapeDtypeStruct((m, d), BF16),
                   jax.ShapeDtypeStruct((batch, groups, n_slc, seq), F32)],
        compiler_params=_params(("parallel", "parallel", "parallel"), VMEM_SMALL_MIB),
        name="nsa_cmp_select",
    )(q_cmp, k_cmp, v_cmp_t, gates_t)


def _nsa_attn_body(q_ref, ks_ref, vs_ref, kw_ref, vw_ref, sel_ref, gate_ref, oc_ref, o_ref,
                   vst_ref, vwt_ref, bias_ref, s_ref, p_ref, acc_ref, *, q_per_kv, per):
    tq = q_ref.shape[0]
    kc = KV_CHUNK
    sub = kc // NSA_SLC_LEN
    step = pl.program_id(2)
    q0 = step * tq

    @pl.when(step == 0)
    def _():
        _store_transposed(vs_ref, vst_ref, kc)
        _store_transposed(vw_ref, vwt_ref, kc)

    n_chain = q_per_kv // per
    q_ts = [jnp.concatenate([q_ref[:, r * HEAD_DIM:(r + 1) * HEAD_DIM]
                             for r in range(c * per, (c + 1) * per)], axis=0).astype(F32).T.astype(BF16)
            for c in range(n_chain)]
    t = q0 + lax.broadcasted_iota(jnp.int32, (1, tq), 1)
    bias_ref[...] = jnp.where(sel_ref[...] > 0.5, 0.0, MASKED_LOGIT)

    def chain_bias(bias):
        return jnp.concatenate([bias] * per, axis=1)

    def selected_bias(c, kpos):
        picked = jnp.concatenate(
            [jnp.broadcast_to(bias_ref[pl.ds(c * sub + a, 1), :], (NSA_SLC_LEN, tq)) for a in range(sub)],
            axis=0)
        return chain_bias(jnp.where(kpos <= t, picked, MASKED_LOGIT))

    def window_bias(kpos):
        return chain_bias(jnp.where(kpos <= t, jnp.where(t - kpos < NSA_WINDOW, 0.0, MASKED_LOGIT),
                                    MASKED_LOGIT))

    def selected_logits(c):
        start = pl.multiple_of(c * kc, kc)
        kpos = start + lax.broadcasted_iota(jnp.int32, (kc, 1), 0)
        k, bias = ks_ref[pl.ds(start, kc), :], selected_bias(c, kpos)
        return [(k, q_t, bias) for q_t in q_ts]

    def window_logits(c):
        start = pl.multiple_of(c * kc, kc)
        kpos = start + lax.broadcasted_iota(jnp.int32, (kc, 1), 0)
        k, bias = kw_ref[pl.ds(start, kc), :], window_bias(kpos)
        return [(k, q_t, bias) for q_t in q_ts]

    bufs = [(s_ref.at[c], p_ref.at[c], acc_ref.at[c]) for c in range(2 * n_chain)]

    def selected_only(c, states):
        return _flash_stage(bufs[:n_chain], states, [vst_ref[jnp.maximum(c - 1, 0)]] * n_chain,
                            selected_logits(c + 1))

    def selected_and_window(c, states, is_last=False):
        prev = jnp.maximum(c - 1, 0)
        following = None if is_last else selected_logits(c + 1) + window_logits(c + 1)
        return _flash_stage(bufs, states, [vst_ref[prev]] * n_chain + [vwt_ref[prev]] * n_chain, following)

    first = jnp.maximum(q0 - (NSA_WINDOW - 1), 0) // kc
    last = (q0 + tq - 1) // kc
    states = tuple(_flash_start(buf, *f) for buf, f in zip(bufs[:n_chain], selected_logits(0)))
    states = _paired_loop(0, first, selected_only, states)
    states = states + tuple(_flash_start(buf, *f) for buf, f in zip(bufs[n_chain:], window_logits(first)))
    states = lax.fori_loop(first, last, selected_and_window, states)
    states = selected_and_window(last, states, is_last=True)

    first_head = pl.program_id(1) * q_per_kv
    for c in range(n_chain):
        selected = _flash_finish(bufs[c], states[c], vst_ref[last])
        window = _flash_finish(bufs[n_chain + c], states[n_chain + c], vwt_ref[last])
        for i in range(per):
            r = c * per + i
            lanes = slice(i * tq, (i + 1) * tq)
            cols = slice(r * HEAD_DIM, (r + 1) * HEAD_DIM)
            g_slc = gate_ref[pl.ds(3 * (first_head + r) + 1, 1), :]
            g_win = gate_ref[pl.ds(3 * (first_head + r) + 2, 1), :]
            mixed = (selected[:, lanes] * g_slc + window[:, lanes] * g_win).T
            o_ref[:, cols] = (mixed + oc_ref[:, cols].astype(F32)).astype(BF16)


def nsa_attention(q_rot, kv, sel, gates_t, o_cmp, batch, seq, tq=256):
    m, d = q_rot.shape
    groups = NSA_KV_GROUPS
    q_per_kv = d // HEAD_DIM // groups
    gw = q_per_kv * HEAD_DIM
    tq = min(tq, seq)
    nq = seq // tq
    n_slc = sel.shape[2]
    assert seq % KV_CHUNK == 0 and KV_CHUNK % NSA_SLC_LEN == 0

    def kv_spec(part):
        return pl.BlockSpec((seq, HEAD_DIM), lambda b, g, i: (b, part * groups + g))

    q_spec = pl.BlockSpec((tq, gw), lambda b, g, i: (b * nq + i, g))
    out = jax.ShapeDtypeStruct((m, d), BF16)
    v_t = pltpu.VMEM((seq // KV_CHUNK, HEAD_DIM, KV_CHUNK), BF16)
    per = 1
    return pl.pallas_call(
        functools.partial(_nsa_attn_body, q_per_kv=q_per_kv, per=per),
        grid=(batch, groups, nq),
        in_specs=[q_spec, kv_spec(0), kv_spec(1), kv_spec(2), kv_spec(3),
                  pl.BlockSpec((None, None, n_slc, tq), lambda b, g, i: (b, g, 0, i)),
                  pl.BlockSpec((gates_t.shape[0], tq), lambda b, g, i: (0, b * nq + i)),
                  q_spec],
        out_specs=q_spec,
        out_shape=out,
        scratch_shapes=[v_t, v_t, pltpu.VMEM((n_slc, tq), F32)]
        + _flash_buffers(2 * q_per_kv // per, KV_CHUNK, per * tq),
        compiler_params=_params(("parallel", "parallel", "arbitrary"), VMEM_SMALL_MIB),
        name="nsa_attention",
    )(q_rot, kv, kv, kv, kv, sel, gates_t, o_cmp)


def nsa_layer(h, gain, w_proj, q_gain, k_gain, cmp_pos, cmp_w1, cmp_w2, w_o, tables, batch, seq):
    m, d = h.shape
    n_heads = d // HEAD_DIM
    groups = NSA_KV_GROUPS
    gwk = groups * HEAD_DIM
    tn = COL_TILE
    tm = min(STREAM_ROWS, m)
    assert tn == gwk and 3 * n_heads <= HEAD_DIM
    raw0, kv0, gate0 = d // tn, d // tn + 2, d // tn + 6
    ones = jnp.ones((gwk,), F32)
    gain_row = jnp.concatenate([jnp.tile(q_gain, n_heads), ones, ones,
                                jnp.tile(k_gain[1] * SOFTMAX_EXP2_SCALE, groups), ones,
                                jnp.tile(k_gain[2] * SOFTMAX_EXP2_SCALE, groups), ones, ones])[None]

    def q_epilogue(accs, cols, extra, outs):
        _head_norm_rope_store(accs[0], cols, extra[0], extra[1:], outs[0], outs[1])

    def raw_epilogue(accs, cols, extra, outs):
        outs[2][:, cols] = accs[0]

    def k_epilogue(accs, cols, extra, outs):
        _head_norm_rope_store(accs[0], cols, extra[0], extra[1:], None, outs[3])

    def v_epilogue(accs, cols, extra, outs):
        outs[3][:, cols] = accs[0].astype(BF16)

    def gate_epilogue(accs, cols, extra, outs):
        if cols.start == 0:
            outs[4][...] = jax.nn.sigmoid(accs[0][:, :HEAD_DIM]).T

    def block(first, count):
        return pl.BlockSpec((tm, tn), lambda i, j: (i, jnp.clip(j - first, 0, count - 1)))

    tab = pl.BlockSpec((tm, HEAD_DIM), lambda i, j: (i, 0))
    q_shape = jax.ShapeDtypeStruct((m, d), BF16)
    q_cmp, q_rot, raw, kv, gates_t = norm_matmul(
        h, gain, w_proj, [lambda i, j: (0, j)], gate0 + 1, tn,
        extra=[(gain_row, pl.BlockSpec((1, tn), lambda i, j: (0, j)))] + [(t, tab) for t in tables],
        outs=[(q_shape, block(0, raw0)), (q_shape, block(0, raw0)),
              (jax.ShapeDtypeStruct((m, 2 * gwk), F32), block(raw0, 2)),
              (jax.ShapeDtypeStruct((m, 4 * gwk), BF16), block(kv0, 4)),
              (jax.ShapeDtypeStruct((HEAD_DIM, m), F32), pl.BlockSpec((HEAD_DIM, tm), lambda i, j: (0, i)))],
        variants=[(lambda j: j < raw0, q_epilogue),
                  (lambda j: (j >= raw0) & (j < kv0), raw_epilogue),
                  (lambda j: (j >= kv0) & (j < gate0) & ((j - kv0) % 2 == 0), k_epilogue),
                  (lambda j: (j >= kv0) & (j < gate0) & ((j - kv0) % 2 == 1), v_epilogue),
                  (lambda j: j == gate0, gate_epilogue)],
        sub=2, tm=tm)

    k_cmp, v_cmp = nsa_compress(raw, cmp_pos, cmp_w1, cmp_w2, k_gain[0], batch, seq)
    o_cmp, sel = nsa_cmp_select(q_cmp, k_cmp, v_cmp, gates_t, batch, seq)
    o = nsa_attention(q_rot, kv, sel, gates_t, o_cmp, batch, seq)
    tmo = min(RESIDENT_ROWS, m)
    return out_matmul([(o, pl.BlockSpec((tmo, d), lambda i: (i, 0)))],
                      lambda ins: ins[0][...], w_o, h, tm=tmo)


def kernel(x, p, positions, mixer_norm, mlp_norm, mlp_w1, mlp_w2, ple_norm, ple_gate, ple_proj,
           moba_w_qkv, moba_q_gain, moba_k_gain, moba_w_o, pool_w, pool_scale,
           nsa_w_q, nsa_w_kv, nsa_q_gain, nsa_k_gain, nsa_cmp_pos, nsa_cmp_w1, nsa_cmp_w2,
           nsa_w_gate, nsa_w_o, conv_w_in, conv_w, conv_b, conv_w_o):
    batch, seq, d = x.shape
    depth = p.shape[0]
    m = batch * seq
    n_heads = d // HEAD_DIM
    gwk = NSA_KV_GROUPS * HEAD_DIM
    tables = rope_tables(positions)
    bf = lambda w: w.astype(BF16)
    ple_gate_bf, ple_proj_bf = bf(ple_gate), bf(ple_proj)
    p_rows = p.reshape(depth, m, -1)

    h = x.reshape(m, d)
    for i in range(depth):
        kind, j = i % 4, i // 4
        if kind == 0:
            h = moba_layer(h, mixer_norm[i], bf(moba_w_qkv[j]), moba_q_gain[j], moba_k_gain[j],
                           bf(moba_w_o[j]), tables, batch, seq)
        elif kind == 1:
            h = pool_layer(h, mixer_norm[i], bf(pool_w[j]), pool_scale[j], seq)
        elif kind == 2:
            w_gate = jnp.pad(nsa_w_gate[j], ((0, 0), (0, gwk - 3 * n_heads)))
            w_proj = bf(jnp.concatenate([nsa_w_q[j], nsa_w_kv[j], w_gate], axis=1))
            h = nsa_layer(h, mixer_norm[i], w_proj, nsa_q_gain[j], nsa_k_gain[j], nsa_cmp_pos[j],
                          bf(nsa_cmp_w1[j]), bf(nsa_cmp_w2[j]), bf(nsa_w_o[j]), tables, batch, seq)
        else:
            h = conv_layer(h, mixer_norm[i], bf(conv_w_in[j]), conv_w[j], conv_b[j], bf(conv_w_o[j]), seq)
        h = mlp_layer(h, mlp_norm[i], mlp_w1, mlp_w2, i)
        h = ple_layer(h, ple_norm[i], ple_gate_bf, p_rows, ple_proj_bf, i)
    return h.reshape(batch, seq, d)
```
